```python
import math
import jax
import jax.numpy as jnp
from jax import lax
import numpy as np

D_MODEL = 1024
BATCH = 16
SEQ = 2048
DEPTH = 1

HEAD_DIM = 64
NSA_WIDTH = D_MODEL // 2
RWKV_WIDTH = D_MODEL - NSA_WIDTH
NSA_HEADS = NSA_WIDTH // HEAD_DIM
NSA_KV_HEADS = 2
NSA_HPG = NSA_HEADS // NSA_KV_HEADS
KV_WIDTH = NSA_KV_HEADS * HEAD_DIM
CMP_BLOCK = 32
CMP_STRIDE = 16
CMP_HIDDEN = 256
SEL_BLOCK = 64
SEL_TOP_N = 16
SEL_LOCAL = 2
WINDOW = 512
Q_BLOCK = 128
ATTN_SCALE = HEAD_DIM ** -0.5
NEG_INF = -1e30
FORCE_SCORE = 1e9
REL_BUCKETS = 32
REL_MAX_DIST = 128
RWKV_HEAD_DIM = 64
RWKV_HEADS = RWKV_WIDTH // RWKV_HEAD_DIM
DECAY_LORA = 64
ICLR_LORA = 64
GATE_LORA = 160
GN_EPS = 64e-5
D_FF = -(-(8 * D_MODEL) // (3 * 256)) * 256
NSA_COLS = NSA_WIDTH + 6 * KV_WIDTH + 3 * NSA_HEADS
RWKV_COLS = 3 * RWKV_WIDTH + DECAY_LORA + ICLR_LORA + GATE_LORA
IN_WIDTH = NSA_COLS + RWKV_COLS
NSA_SPLITS = [NSA_WIDTH + i * KV_WIDTH for i in range(7)]
RWKV_SPLITS = [RWKV_WIDTH, 2 * RWKV_WIDTH, 3 * RWKV_WIDTH, 3 * RWKV_WIDTH + DECAY_LORA, 3 * RWKV_WIDTH + DECAY_LORA + ICLR_LORA]

kernel_name = 'hybrid_nsa_rwkv7_block'


def rms_norm(x, gain, eps=1e-6):
    xf = x.astype(jnp.float32)
    y = xf * lax.rsqrt(jnp.mean(xf * xf, axis=-1, keepdims=True) + eps)
    return (y * gain.astype(jnp.float32)).astype(x.dtype)


def rel_bucket(dist):
    max_exact = REL_BUCKETS // 2
    d = jnp.maximum(dist, 0)
    log_ratio = jnp.log(jnp.maximum(d, 1).astype(jnp.float32) / max_exact) / math.log(REL_MAX_DIST / max_exact)
    large = jnp.minimum(max_exact + (log_ratio * (REL_BUCKETS - max_exact)).astype(jnp.int32), REL_BUCKETS - 1)
    return jnp.where(d < max_exact, d, large)


def compress_blocks(kv, pe, w1, b1, w2, b2):
    bsz, seq, n_g, dh = kv.shape
    n_c = (seq - CMP_BLOCK) // CMP_STRIDE + 1
    idx = jnp.arange(n_c)[:, None] * CMP_STRIDE + jnp.arange(CMP_BLOCK)[None, :]
    blk = kv[:, idx] + pe[:, None, :].astype(kv.dtype)
    blk = jnp.transpose(blk, (0, 1, 3, 2, 4)).reshape(bsz, n_c, n_g, CMP_BLOCK * dh)
    hid = jax.nn.gelu(blk @ w1 + b1)
    return hid @ w2 + b2


def compressed_attention(q, k_c, v_c, rel_table):
    seq = q.shape[1]
    n_c = k_c.shape[1]
    t = jnp.arange(seq)[:, None]
    end = jnp.arange(n_c)[None, :] * CMP_STRIDE + CMP_BLOCK - 1
    valid = end <= t
    bias = rel_table[rel_bucket(t - end)].astype(jnp.float32)
    bias = jnp.transpose(bias, (2, 0, 1)).reshape(NSA_KV_HEADS, NSA_HPG, seq, n_c)
    s = jnp.einsum('bsgpd,bcgd->bgpsc', q, k_c).astype(jnp.float32) * ATTN_SCALE + bias
    s = jnp.where(valid, s, NEG_INF)
    p = jax.nn.softmax(s, axis=-1) * valid.astype(jnp.float32)
    o = jnp.einsum('bgpsc,bcgd->bsgpd', p.astype(v_c.dtype), v_c)
    return o, p


def select_blocks(p_cmp, seq):
    n_c = p_cmp.shape[-1]
    n_sel = seq // SEL_BLOCK
    c_start = jnp.arange(n_c)[:, None] * CMP_STRIDE
    j_start = jnp.arange(n_sel)[None, :] * SEL_BLOCK
    overlap = jnp.clip(jnp.minimum(c_start + CMP_BLOCK, j_start + SEL_BLOCK) - jnp.maximum(c_start, j_start), 0, None)
    overlap = overlap.astype(jnp.float32) / CMP_STRIDE
    imp = jnp.einsum('bgpsc,cj->bgsj', p_cmp, overlap)
    qb = (jnp.arange(seq) // SEL_BLOCK)[:, None]
    j = jnp.arange(n_sel)[None, :]
    forced = (j == 0) | ((j <= qb) & (j > qb - SEL_LOCAL))
    score = jnp.where(forced, FORCE_SCORE, imp)
    score = jnp.where(j <= qb, score, -1.0)
    _, idx = lax.top_k(score, min(SEL_TOP_N, n_sel))
    return idx


def selected_attention(q, k_s, v_s, idx, rel_table):
    bsz, seq, n_g, n_p, dh = q.shape
    n_sel = seq // SEL_BLOCK
    n_qb = seq // Q_BLOCK
    top = idx.shape[-1]
    kb = k_s.reshape(bsz, n_sel, SEL_BLOCK, n_g, dh).transpose(0, 3, 1, 2, 4)
    vb = v_s.reshape(bsz, n_sel, SEL_BLOCK, n_g, dh).transpose(0, 3, 1, 2, 4)
    q_x = q.reshape(bsz * n_qb, Q_BLOCK, n_g, n_p, dh)
    i_x = idx.reshape(bsz, n_g, n_qb, Q_BLOCK, top).transpose(0, 2, 1, 3, 4).reshape(bsz * n_qb, n_g, Q_BLOCK, top)
    steps = jnp.arange(bsz * n_qb)
    tab = rel_table.reshape(REL_BUCKETS, n_g, n_p).transpose(1, 0, 2)
    gather_g = jax.vmap(lambda arr, ix: arr[ix])

    def one_block(args):
        q_blk, i_blk, st = args
        b = st // n_qb
        t = (st % n_qb) * Q_BLOCK + jnp.arange(Q_BLOCK)
        kg = gather_g(lax.dynamic_index_in_dim(kb, b, 0, keepdims=False), i_blk)
        vg = gather_g(lax.dynamic_index_in_dim(vb, b, 0, keepdims=False), i_blk)
        pos = i_blk[..., None] * SEL_BLOCK + jnp.arange(SEL_BLOCK)
        dist = t[None, :, None, None] - pos
        bias = gather_g(tab, rel_bucket(dist)).astype(jnp.float32)
        bias = jnp.moveaxis(bias, -1, 1).reshape(n_g, n_p, Q_BLOCK, top * SEL_BLOCK)
        s = jnp.einsum('sgpd,gskld->gpskl', q_blk, kg).astype(jnp.float32)
        s = s.reshape(n_g, n_p, Q_BLOCK, top * SEL_BLOCK) * ATTN_SCALE + bias
        valid = (dist >= 0).reshape(n_g, 1, Q_BLOCK, top * SEL_BLOCK)
        p = jax.nn.softmax(jnp.where(valid, s, NEG_INF), axis=-1)
        return jnp.einsum('gpsm,gsmd->sgpd', p.astype(vg.dtype), vg.reshape(n_g, Q_BLOCK, top * SEL_BLOCK, dh))

    o = lax.map(one_block, (q_x, i_x, steps))
    return o.reshape(bsz, seq, n_g, n_p, dh)


def window_attention(q, k_w, v_w, rel_table):
    bsz, seq, n_g, n_p, dh = q.shape
    n_qb = seq // Q_BLOCK
    span = WINDOW + Q_BLOCK
    pad = ((0, 0), (WINDOW, 0), (0, 0), (0, 0))
    idx = jnp.arange(n_qb)[:, None] * Q_BLOCK + jnp.arange(span)[None, :]
    k_band = jnp.pad(k_w, pad)[:, idx]
    v_band = jnp.pad(v_w, pad)[:, idx]
    t = jnp.arange(n_qb)[:, None, None] * Q_BLOCK + jnp.arange(Q_BLOCK)[None, :, None]
    s_pos = idx[:, None, :] - WINDOW
    dist = t - s_pos
    valid = (s_pos >= 0) & (dist >= 0) & (dist < WINDOW)
    bias = rel_table[rel_bucket(dist)].astype(jnp.float32)
    bias = jnp.moveaxis(bias, -1, 1).reshape(n_qb, n_g, n_p, Q_BLOCK, span)
    q_b = q.reshape(bsz, n_qb, Q_BLOCK, n_g, n_p, dh)
    s = jnp.einsum('bnqgpd,bnkgd->bngpqk', q_b, k_band).astype(jnp.float32) * ATTN_SCALE + bias
    p = jax.nn.softmax(jnp.where(valid[:, None, None], s, NEG_INF), axis=-1)
    o = jnp.einsum('bngpqk,bnkgd->bnqgpd', p.astype(v_band.dtype), v_band)
    return o.reshape(bsz, seq, n_g, n_p, dh)


def nsa_mixer(z, q_gain, k_gain, cmp_pe, cmp_w1, cmp_b1, cmp_w2, cmp_b2, rel_table, out_gain):
    bsz, seq, _ = z.shape
    q, kc, vc, ks, vs, kw, vw, gl = jnp.split(z, NSA_SPLITS, axis=-1)
    q = rms_norm(q.reshape(bsz, seq, NSA_KV_HEADS, NSA_HPG, HEAD_DIM), q_gain)
    kv_heads = lambda u: u.reshape(bsz, seq, NSA_KV_HEADS, HEAD_DIM)
    k_cmp = rms_norm(compress_blocks(kv_heads(kc), cmp_pe[0], cmp_w1[0], cmp_b1[0], cmp_w2[0], cmp_b2[0]), k_gain[0])
    v_cmp = compress_blocks(kv_heads(vc), cmp_pe[1], cmp_w1[1], cmp_b1[1], cmp_w2[1], cmp_b2[1])
    o_cmp, p_cmp = compressed_attention(q, k_cmp, v_cmp, rel_table)
    sel_idx = select_blocks(p_cmp, seq)
    o_sel = selected_attention(q, rms_norm(kv_heads(ks), k_gain[1]), kv_heads(vs), sel_idx, rel_table)
    o_win = window_attention(q, rms_norm(kv_heads(kw), k_gain[2]), kv_heads(vw), rel_table)
    g = jax.nn.sigmoid(gl).reshape(bsz, seq, NSA_KV_HEADS, NSA_HPG, 3)
    o = g[..., 0:1] * o_cmp + g[..., 1:2] * o_sel + g[..., 2:3] * o_win
    return rms_norm(o.reshape(bsz, seq, NSA_WIDTH), out_gain)


def wkv7_scan(r, w, k, v, a, b):
    bsz, seq, n_h, n = r.shape
    xs = tuple(jnp.moveaxis(u.astype(jnp.float32), 1, 0) for u in (r, w, k, v, a, b))

    def step(state, inp):
        r_t, w_t, k_t, v_t, a_t, b_t = inp
        sa = jnp.einsum('bhij,bhj->bhi', state, a_t)
        state = state * w_t[:, :, None, :] + sa[..., None] * b_t[:, :, None, :] + v_t[..., None] * k_t[:, :, None, :]
        return state, jnp.einsum('bhij,bhj->bhi', state, r_t)

    _, y = lax.scan(step, jnp.zeros((bsz, n_h, n, n), jnp.float32), xs)
    return jnp.moveaxis(y, 0, 1)


def rwkv7_mixer(z, shift_mix, w0, w2, a0, a2, g2, k_k, k_a, r_k, ln_gain, ln_bias):
    bsz, seq, _ = z.shape
    z = z + shift_mix * (jnp.pad(z, ((0, 0), (1, 0), (0, 0)))[:, :-1] - z)
    r, k, v, wl, al, gl = jnp.split(z, RWKV_SPLITS, axis=-1)
    w_log = -jax.nn.softplus(-(w0 + jnp.tanh(wl) @ w2)) - 0.5
    decay = jnp.exp(-jnp.exp(w_log.astype(jnp.float32)))
    a = jax.nn.sigmoid(a0 + al @ a2)
    g = jax.nn.sigmoid(gl) @ g2
    heads = lambda u: u.reshape(bsz, seq, RWKV_HEADS, RWKV_HEAD_DIM)
    kk = heads(k * k_k).astype(jnp.float32)
    kk = kk / jnp.maximum(jnp.sqrt(jnp.sum(kk * kk, axis=-1, keepdims=True)), 1e-12)
    k = k * (1.0 + (a - 1.0) * k_a)
    a_h = heads(a).astype(jnp.float32)
    y = wkv7_scan(heads(r), heads(decay), heads(k), heads(v), -kk, kk * a_h)
    mu = jnp.mean(y, axis=-1, keepdims=True)
    var = jnp.mean(jnp.square(y - mu), axis=-1, keepdims=True)
    y = ((y - mu) * lax.rsqrt(var + GN_EPS)).reshape(bsz, seq, RWKV_WIDTH) * ln_gain + ln_bias
    bonus = jnp.sum(heads(r) * heads(k) * r_k, axis=-1, keepdims=True) * heads(v)
    return ((y + bonus.reshape(bsz, seq, RWKV_WIDTH)) * g).astype(z.dtype)


def setup_inputs(seed: int = 0) -> dict:
    key = jax.random.key(seed)
    ks = jax.random.split(key, 32)
    nrm = lambda k, shape, scale: jax.random.normal(k, shape, jnp.float32) * scale
    L = DEPTH
    return {
        'x': nrm(ks[0], (BATCH, SEQ, D_MODEL), 1.0),
        'attn_norm_gain': 1.0 + nrm(ks[1], (L, D_MODEL), 0.02),
        'w_in': nrm(ks[2], (L, D_MODEL, IN_WIDTH), D_MODEL ** -0.5),
        'nsa_q_gain': 1.0 + nrm(ks[3], (L, HEAD_DIM), 0.02),
        'nsa_k_gain': 1.0 + nrm(ks[4], (L, 3, HEAD_DIM), 0.02),
        'cmp_pe': nrm(ks[5], (L, 2, CMP_BLOCK, HEAD_DIM), 0.1),
        'cmp_w1': nrm(ks[6], (L, 2, CMP_BLOCK * HEAD_DIM, CMP_HIDDEN), (CMP_BLOCK * HEAD_DIM) ** -0.5),
        'cmp_b1': nrm(ks[7], (L, 2, CMP_HIDDEN), 0.01),
        'cmp_w2': nrm(ks[8], (L, 2, CMP_HIDDEN, HEAD_DIM), CMP_HIDDEN ** -0.5),
        'cmp_b2': nrm(ks[9], (L, 2, HEAD_DIM), 0.01),
        'nsa_out_gain': 1.0 + nrm(ks[10], (L, NSA_WIDTH), 0.02),
        'rwkv_shift_mix': jax.random.uniform(ks[11], (L, RWKV_COLS), jnp.float32),
        'rwkv_w0': -1.0 + nrm(ks[12], (L, RWKV_WIDTH), 0.5),
        'rwkv_w2': nrm(ks[13], (L, DECAY_LORA, RWKV_WIDTH), 0.1),
        'rwkv_a0': nrm(ks[14], (L, RWKV_WIDTH), 0.1),
        'rwkv_a2': nrm(ks[15], (L, ICLR_LORA, RWKV_WIDTH), 0.5 * ICLR_LORA ** -0.5),
        'rwkv_g2': nrm(ks[16], (L, GATE_LORA, RWKV_WIDTH), GATE_LORA ** -0.5),
        'rwkv_k_k': 0.85 + nrm(ks[17], (L, RWKV_WIDTH), 0.02),
        'rwkv_k_a': 1.0 + nrm(ks[18], (L, RWKV_WIDTH), 0.02),
        'rwkv_r_k': nrm(ks[19], (L, RWKV_HEADS, RWKV_HEAD_DIM), 0.1),
        'rwkv_ln_gain': 1.0 + nrm(ks[20], (L, RWKV_WIDTH), 0.02),
        'rwkv_ln_bias': nrm(ks[21], (L, RWKV_WIDTH), 0.01),
        'w_out': nrm(ks[22], (L, D_MODEL, D_MODEL), D_MODEL ** -0.5),
        'ffn_norm_gain': 1.0 + nrm(ks[23], (L, D_MODEL), 0.02),
        'w_gate': nrm(ks[24], (L, D_MODEL, D_FF), D_MODEL ** -0.5),
        'w_up': nrm(ks[25], (L, D_MODEL, D_FF), D_MODEL ** -0.5),
        'w_down': nrm(ks[26], (L, D_FF, D_MODEL), D_FF ** -0.5),
        'rel_bias_table': nrm(ks[27], (REL_BUCKETS, NSA_HEADS), 0.3),
    }


def reference(x, attn_norm_gain, w_in, nsa_q_gain, nsa_k_gain, cmp_pe, cmp_w1, cmp_b1, cmp_w2, cmp_b2,
              nsa_out_gain, rwkv_shift_mix, rwkv_w0, rwkv_w2, rwkv_a0, rwkv_a2, rwkv_g2, rwkv_k_k, rwkv_k_a,
              rwkv_r_k, rwkv_ln_gain, rwkv_ln_bias, w_out, ffn_norm_gain, w_gate, w_up, w_down, rel_bias_table):
    h = x
    for l in range(DEPTH):
        xn = rms_norm(h, attn_norm_gain[l])
        z = xn @ w_in[l]
        y_nsa = nsa_mixer(z[..., :NSA_COLS], nsa_q_gain[l], nsa_k_gain[l], cmp_pe[l], cmp_w1[l], cmp_b1[l],
                          cmp_w2[l], cmp_b2[l], rel_bias_table, nsa_out_gain[l])
        y_rwkv = rwkv7_mixer(z[..., NSA_COLS:], rwkv_shift_mix[l], rwkv_w0[l], rwkv_w2[l], rwkv_a0[l], rwkv_a2[l],
                             rwkv_g2[l], rwkv_k_k[l], rwkv_k_a[l], rwkv_r_k[l], rwkv_ln_gain[l], rwkv_ln_bias[l])
        h = h + jnp.concatenate([y_nsa, y_rwkv], axis=-1) @ w_out[l]
        hn = rms_norm(h, ffn_norm_gain[l])
        h = h + (jax.nn.silu(hn @ w_gate[l]) * (hn @ w_up[l])) @ w_down[l]
    return h
```

```python
import functools
import math

import numpy as np
import jax
import jax.numpy as jnp
from jax import lax
from jax.experimental import pallas as pl
from jax.experimental.pallas import tpu as pltpu

HEAD_DIM = 64
NSA_HEADS = 8
NSA_KV_HEADS = 2
NSA_HPG = NSA_HEADS // NSA_KV_HEADS
NSA_WIDTH = NSA_HEADS * HEAD_DIM
KV_WIDTH = NSA_KV_HEADS * HEAD_DIM
CMP_BLOCK = 32
CMP_STRIDE = 16
CMP_HIDDEN = 256
SEL_BLOCK = 64
SEL_TOP_N = 16
SEL_LOCAL = 2
WINDOW = 512
ATTN_SCALE = HEAD_DIM ** -0.5
NEG_INF = -1e30
FORCE_SCORE = 1e9
REL_BUCKETS = 32
REL_MAX_DIST = 128
RWKV_HEADS = 8
RWKV_HEAD_DIM = 64
RWKV_WIDTH = RWKV_HEADS * RWKV_HEAD_DIM
DECAY_LORA = 64
ICLR_LORA = 64
GATE_LORA = 160
GATE_LORA_PAD = 256
GN_EPS = 64e-5
RMS_EPS = 1e-6

LANES = 128
Q_TILE = 128
K_TILE = 128
WKV_TILE = 256
WKV_CHUNK = 64
ROW_TILE = 512
VMEM_LIMIT_BYTES = 56 * 1024 * 1024

F32 = jnp.float32
BF16 = jnp.bfloat16


def _dot(a, b):
    return jnp.dot(a.astype(BF16), b.astype(BF16), preferred_element_type=F32)


def _dot_nt(a, b):
    return lax.dot_general(a.astype(BF16), b.astype(BF16), (((1,), (1,)), ((), ())),
                           preferred_element_type=F32)


def _dot_tn(a, b):
    return lax.dot_general(a.astype(BF16), b.astype(BF16), (((0,), (0,)), ((), ())),
                           preferred_element_type=F32)


def _split3(x):
    hi = x.astype(BF16)
    r1 = x - hi.astype(F32)
    mid = r1.astype(BF16)
    lo = (r1 - mid.astype(F32)).astype(BF16)
    return hi, mid, lo


def _dot_exact_rhs(x, m_bf16):
    hi, mid, lo = _split3(x)
    acc = jnp.dot(lo, m_bf16, preferred_element_type=F32)
    acc = acc + jnp.dot(mid, m_bf16, preferred_element_type=F32)
    return acc + jnp.dot(hi, m_bf16, preferred_element_type=F32)


def _dot_exact_lhs(m_bf16, x):
    hi, mid, lo = _split3(x)
    acc = jnp.dot(m_bf16, lo, preferred_element_type=F32)
    acc = acc + jnp.dot(m_bf16, mid, preferred_element_type=F32)
    return acc + jnp.dot(m_bf16, hi, preferred_element_type=F32)


def _rel_bucket_np(dist):
    max_exact = REL_BUCKETS // 2
    d = np.maximum(dist, 0)
    ratio = np.maximum(d, 1).astype(np.float32) / np.float32(max_exact)
    log_ratio = np.log(ratio).astype(np.float32) / np.float32(math.log(REL_MAX_DIST / max_exact))
    large = np.minimum(max_exact + (log_ratio * np.float32(REL_BUCKETS - max_exact)).astype(np.int32),
                       REL_BUCKETS - 1)
    return np.where(d < max_exact, d, large).astype(np.int32)


def _inproj_kernel(x_ref, gain_ref, w_ref, *out_refs, widths):
    x = x_ref[...]
    ms = jnp.mean(x * x, axis=-1, keepdims=True)
    xn = (x * lax.rsqrt(ms + RMS_EPS) * gain_ref[...]).astype(BF16)
    off = 0
    for o_ref, w in zip(out_refs, widths):
        o_ref[...] = jnp.dot(xn, w_ref[:, off:off + w], preferred_element_type=F32)
        off += w


def _in_projection(x2, gain, w_cat, widths):
    n, d = x2.shape
    tm = min(ROW_TILE, n)
    total = sum(widths)
    return pl.pallas_call(
        functools.partial(_inproj_kernel, widths=widths),
        grid=(n // tm,),
        in_specs=[
            pl.BlockSpec((tm, d), lambda i: (i, 0)),
            pl.BlockSpec((1, d), lambda i: (0, 0)),
            pl.BlockSpec((d, total), lambda i: (0, 0)),
        ],
        out_specs=[pl.BlockSpec((tm, w), lambda i: (i, 0)) for w in widths],
        out_shape=[jax.ShapeDtypeStruct((n, w), F32) for w in widths],
        compiler_params=pltpu.CompilerParams(
            dimension_semantics=("arbitrary",), vmem_limit_bytes=VMEM_LIMIT_BYTES),
        name="in_projection",
    )(x2, gain, w_cat)


def _head_rms(u, gain):
    ms = jnp.mean(u * u, axis=-1, keepdims=True)
    return u * lax.rsqrt(ms + RMS_EPS) * gain


def _flash_step(s, v_blk, m, l, acc):
    m_new = jnp.maximum(m, jnp.max(s, axis=-1, keepdims=True))
    alpha = jnp.exp(m - m_new)
    p = jnp.exp(s - m_new)
    l_new = alpha * l + jnp.sum(p, axis=-1, keepdims=True)
    acc_new = alpha * acc + jnp.dot(p.astype(BF16), v_blk, preferred_element_type=F32)
    return m_new, l_new, acc_new


def _nsa_kernel(tab_ref,
                q_ref, kv_ref, gate_ref,
                qg_ref, kg_ref, pe_ref, w1_ref, b1_ref, w2_ref, b2_ref, og_ref,
                bk0_ref, bk1_ref, bkc_ref, ovl_ref, exp_ref,
                out_ref,
                kvpad_ref, kcmp_ref, vcmp_ref, ksel_ref, vsel_ref, kwin_ref, vwin_ref,
                btab_ref, ccan_ref, nmask_ref, ybuf_ref,
                *, seq):
    b = pl.program_id(0)
    qt = pl.program_id(1)
    n_cb = seq // CMP_STRIDE
    n_c = (seq - CMP_BLOCK) // CMP_STRIDE + 1
    n_sel = seq // SEL_BLOCK
    rows = NSA_HPG * Q_TILE

    @pl.when((b == 0) & (qt == 0))
    def _build_bias_tables():
        bk0 = bk0_ref[...]
        bk1 = bk1_ref[...]
        bkc = bkc_ref[...]
        ii = lax.broadcasted_iota(jnp.int32, (Q_TILE, K_TILE), 0)
        jj = lax.broadcasted_iota(jnp.int32, (Q_TILE, K_TILE), 1)
        causal = jnp.where(jj <= ii, 0.0, NEG_INF).astype(F32)
        anti = jnp.where(jj > ii, 0.0, NEG_INF).astype(F32)
        for h in range(NSA_HEADS):
            t0 = jnp.zeros((Q_TILE, K_TILE), F32)
            t1 = jnp.zeros((Q_TILE, K_TILE), F32)
            tc = jnp.zeros((Q_TILE, 2 * LANES), F32)
            for k in range(REL_BUCKETS):
                val = tab_ref[k, h]
                t0 = jnp.where(bk0 == k, val, t0)
                t1 = jnp.where(bk1 == k, val, t1)
                tc = jnp.where(bkc == k, val, tc)
            far = jnp.full((Q_TILE, K_TILE), tab_ref[REL_BUCKETS - 1, h], F32)
            g, p = divmod(h, NSA_HPG)
            rs = slice(p * Q_TILE, (p + 1) * Q_TILE)
            btab_ref[g, 0, rs, :] = far
            btab_ref[g, 1, rs, :] = t1
            btab_ref[g, 2, rs, :] = t0 + causal
            btab_ref[g, 3, rs, :] = far + anti
            ccan_ref[h] = tc

    @pl.when(qt == 0)
    def _per_batch():
        for which in range(2):
            kvpad_ref[which, seq:seq + 2 * CMP_STRIDE, :] = jnp.zeros((2 * CMP_STRIDE, KV_WIDTH), F32)
            kvpad_ref[which, 0:seq, :] = kv_ref[0, :, which * KV_WIDTH:(which + 1) * KV_WIDTH]
            for g in range(NSA_KV_HEADS):
                c0 = g * HEAD_DIM
                hid = jnp.zeros((n_cb, CMP_HIDDEN), F32)
                for l in range(CMP_BLOCK):
                    rows_l = kvpad_ref[which, pl.ds(l, n_cb, stride=CMP_STRIDE), :]
                    piece = rows_l[:, c0:c0 + HEAD_DIM] + pe_ref[which, l:l + 1, :]
                    hid = hid + _dot(piece, w1_ref[which, l])
                hid = jax.nn.gelu(hid + b1_ref[which], approximate=True)
                cmp = _dot(hid, w2_ref[which]) + b2_ref[which]
                if which == 0:
                    kcmp_ref[g] = _head_rms(cmp, kg_ref[0:1, :]).astype(BF16)
                else:
                    vcmp_ref[g] = cmp.astype(BF16)
        kv = kv_ref[0]
        for g in range(NSA_KV_HEADS):
            def col(i):
                c = (2 + i) * KV_WIDTH + g * HEAD_DIM
                return kv[:, c:c + HEAD_DIM]
            ksel_ref[g] = _head_rms(col(0), kg_ref[1:2, :]).astype(BF16)
            vsel_ref[g] = col(1).astype(BF16)
            kwin_ref[g] = _head_rms(col(2), kg_ref[2:3, :]).astype(BF16)
            vwin_ref[g] = col(3).astype(BF16)

    t0 = qt * Q_TILE
    q = q_ref[0]
    gates = jax.nn.sigmoid(gate_ref[0])
    row_t = t0 + lax.broadcasted_iota(jnp.int32, (Q_TILE, 1), 0)
    ssq = jnp.zeros((Q_TILE, 1), F32)

    for g in range(NSA_KV_HEADS):
        qs = []
        for p in range(NSA_HPG):
            h = g * NSA_HPG + p
            qh = _head_rms(q[:, h * HEAD_DIM:(h + 1) * HEAD_DIM], qg_ref[...])
            qs.append((qh * ATTN_SCALE).astype(BF16))
        q_stack = jnp.concatenate(qs, axis=0)

        s = _dot_nt(q_stack, kcmp_ref[g])
        cbias = []
        for p in range(NSA_HPG):
            h = g * NSA_HPG + p
            rolled = pltpu.roll(ccan_ref[h], (CMP_STRIDE // 2) * qt, 1)
            cbias.append(rolled[:, LANES:2 * LANES])
        cbias = jnp.concatenate(cbias, axis=0)
        cc = lax.broadcasted_iota(jnp.int32, (Q_TILE, n_cb), 1)
        valid = (cc * CMP_STRIDE + (CMP_BLOCK - 1) <= row_t) & (cc < n_c)
        valid4 = jnp.concatenate([valid] * NSA_HPG, axis=0)
        s = jnp.where(valid4, s + cbias, NEG_INF)
        m = jnp.max(s, axis=-1, keepdims=True)
        e = jnp.exp(s - m)
        pc = e / jnp.sum(e, axis=-1, keepdims=True) * valid4.astype(F32)
        o_cmp = jnp.dot(pc.astype(BF16), vcmp_ref[g], preferred_element_type=F32)
        psum = pc[0:Q_TILE]
        for p in range(1, NSA_HPG):
            psum = psum + pc[p * Q_TILE:(p + 1) * Q_TILE]

        imp = _dot_exact_rhs(psum, ovl_ref[...])
        jb = lax.broadcasted_iota(jnp.int32, (Q_TILE, LANES), 1)
        qb = row_t // SEL_BLOCK
        forced = (jb == 0) | ((jb <= qb) & (jb > qb - SEL_LOCAL))
        score = jnp.where(forced, FORCE_SCORE, imp)
        score = jnp.where(jb <= qb, score, -1.0)
        score = jnp.where(jb < n_sel, score, -2.0)
        rank = jnp.zeros((Q_TILE, LANES), F32)
        for i in range(n_sel):
            si = score[:, i:i + 1]
            ahead = (si > score) | ((si == score) & (jb > i))
            rank = rank + ahead.astype(F32)
        sel = (rank < float(min(SEL_TOP_N, n_sel))) & (jb < n_sel)
        nmask_ref[...] = (_dot(sel.astype(F32), exp_ref[...]) - 1.0) * (-NEG_INF)

        def sel_body(kb, carry):
            m, l, acc = carry
            k0 = pl.multiple_of(kb * K_TILE, K_TILE)
            kind = jnp.clip(kb - qt + 2, 0, 2)
            s = _dot_nt(q_stack, ksel_ref[g, pl.ds(k0, K_TILE), :]) + btab_ref[g, kind]
            nm = nmask_ref[:, pl.ds(k0, K_TILE)]
            s = s + jnp.concatenate([nm] * NSA_HPG, axis=0)
            return _flash_step(s, vsel_ref[g, pl.ds(k0, K_TILE), :], m, l, acc)

        init = (jnp.full((rows, 1), NEG_INF, F32), jnp.zeros((rows, 1), F32),
                jnp.zeros((rows, HEAD_DIM), F32))
        _, l_s, acc_s = lax.fori_loop(0, qt + 1, sel_body, init)
        o_sel = acc_s / l_s

        n_wb = WINDOW // K_TILE

        def win_body(kb, carry):
            m, l, acc = carry
            k0 = pl.multiple_of(kb * K_TILE, K_TILE)
            rel = qt - kb
            kind = jnp.where(rel == 0, 2, jnp.where(rel == 1, 1, jnp.where(rel == n_wb, 3, 0)))
            s = _dot_nt(q_stack, kwin_ref[g, pl.ds(k0, K_TILE), :]) + btab_ref[g, kind]
            return _flash_step(s, vwin_ref[g, pl.ds(k0, K_TILE), :], m, l, acc)

        _, l_w, acc_w = lax.fori_loop(jnp.maximum(qt - n_wb, 0), qt + 1, win_body, init)
        o_win = acc_w / l_w

        for p in range(NSA_HPG):
            h = g * NSA_HPG + p
            rs = slice(p * Q_TILE, (p + 1) * Q_TILE)
            o = (gates[:, h:h + 1] * o_cmp[rs]
                 + gates[:, NSA_HEADS + h:NSA_HEADS + h + 1] * o_sel[rs]
                 + gates[:, 2 * NSA_HEADS + h:2 * NSA_HEADS + h + 1] * o_win[rs])
            ssq = ssq + jnp.sum(o * o, axis=-1, keepdims=True)
            ybuf_ref[:, h * HEAD_DIM:(h + 1) * HEAD_DIM] = o

    y = ybuf_ref[...] * lax.rsqrt(ssq / NSA_WIDTH + RMS_EPS) * og_ref[...]
    out_ref[0] = y.astype(out_ref.dtype)


def _nsa_mixer(q, kv, gates, q_gain, k_gain, cmp_pe, cmp_w1, cmp_b1, cmp_w2, cmp_b2, out_gain, rel_table):
    bsz, seq, _ = q.shape
    assert seq % Q_TILE == 0 and seq // CMP_STRIDE == LANES and seq // SEL_BLOCK <= LANES
    n_qt = seq // Q_TILE
    n_cb = seq // CMP_STRIDE
    n_sel = seq // SEL_BLOCK
    rows = NSA_HPG * Q_TILE

    ii = np.arange(Q_TILE)[:, None]
    jj = np.arange(K_TILE)[None, :]
    bk0 = _rel_bucket_np(ii - jj)
    bk1 = _rel_bucket_np(K_TILE + ii - jj)
    cprime = np.arange(2 * LANES)[None, :] - LANES
    bkc = _rel_bucket_np(ii - (CMP_BLOCK - 1) - CMP_STRIDE * cprime)
    c_start = np.arange(n_cb)[:, None] * CMP_STRIDE
    j_start = np.arange(LANES)[None, :] * SEL_BLOCK
    ovl = np.clip(np.minimum(c_start + CMP_BLOCK, j_start + SEL_BLOCK) - np.maximum(c_start, j_start), 0, None)
    ovl = (ovl.astype(np.float32) / CMP_STRIDE)
    ovl[:, n_sel:] = 0.0
    expand = (np.arange(LANES)[:, None] == (np.arange(seq)[None, :] // SEL_BLOCK)).astype(np.float32)

    w1 = cmp_w1.reshape(2, CMP_BLOCK, HEAD_DIM, CMP_HIDDEN).astype(BF16)
    full = lambda shape: pl.BlockSpec(shape, lambda b, t: (0,) * len(shape))
    kernel = functools.partial(_nsa_kernel, seq=seq)
    return pl.pallas_call(
        kernel,
        grid=(bsz, n_qt),
        in_specs=[
            pl.BlockSpec(memory_space=pltpu.SMEM),
            pl.BlockSpec((1, Q_TILE, NSA_WIDTH), lambda b, t: (b, t, 0)),
            pl.BlockSpec((1, seq, 6 * KV_WIDTH), lambda b, t: (b, 0, 0)),
            pl.BlockSpec((1, Q_TILE, LANES), lambda b, t: (b, t, 0)),
            full((1, HEAD_DIM)),
            full((3, HEAD_DIM)),
            full((2, CMP_BLOCK, HEAD_DIM)),
            full((2, CMP_BLOCK, HEAD_DIM, CMP_HIDDEN)),
            full((2, 1, CMP_HIDDEN)),
            full((2, CMP_HIDDEN, HEAD_DIM)),
            full((2, 1, HEAD_DIM)),
            full((1, NSA_WIDTH)),
            full((Q_TILE, K_TILE)),
            full((Q_TILE, K_TILE)),
            full((Q_TILE, 2 * LANES)),
            full((n_cb, LANES)),
            full((LANES, seq)),
        ],
        out_specs=pl.BlockSpec((1, Q_TILE, NSA_WIDTH), lambda b, t: (b, t, 0)),
        out_shape=jax.ShapeDtypeStruct((bsz, seq, NSA_WIDTH), BF16),
        scratch_shapes=[
            pltpu.VMEM((2, seq + 2 * CMP_STRIDE, KV_WIDTH), F32),
            pltpu.VMEM((NSA_KV_HEADS, n_cb, HEAD_DIM), BF16),
            pltpu.VMEM((NSA_KV_HEADS, n_cb, HEAD_DIM), BF16),
            pltpu.VMEM((NSA_KV_HEADS, seq, HEAD_DIM), BF16),
            pltpu.VMEM((NSA_KV_HEADS, seq, HEAD_DIM), BF16),
            pltpu.VMEM((NSA_KV_HEADS, seq, HEAD_DIM), BF16),
            pltpu.VMEM((NSA_KV_HEADS, seq, HEAD_DIM), BF16),
            pltpu.VMEM((NSA_KV_HEADS, 4, rows, K_TILE), F32),
            pltpu.VMEM((NSA_HEADS, Q_TILE, 2 * LANES), F32),
            pltpu.VMEM((Q_TILE, seq), F32),
            pltpu.VMEM((Q_TILE, NSA_WIDTH), F32),
        ],
        compiler_params=pltpu.CompilerParams(
            dimension_semantics=("arbitrary", "arbitrary"), vmem_limit_bytes=VMEM_LIMIT_BYTES),
        name="nsa_mixer",
    )(rel_table, q, kv, gates, q_gain.reshape(1, HEAD_DIM), k_gain, cmp_pe, w1,
      cmp_b1.reshape(2, 1, CMP_HIDDEN), cmp_w2.astype(BF16), cmp_b2.reshape(2, 1, HEAD_DIM),
      out_gain.reshape(1, NSA_WIDTH), jnp.asarray(bk0), jnp.asarray(bk1), jnp.asarray(bkc),
      jnp.asarray(ovl, BF16), jnp.asarray(expand, BF16))


def _rwkv_kernel(zr_ref, zl_ref, zg_ref, mr_ref, ml_ref, mg_ref, w0_ref, w2_ref, a0_ref, a2_ref, g2_ref,
                 kk_ref, ka_ref, rk_ref, lng_ref, lnb_ref, tri_ref,
                 out_ref,
                 state_ref, cr_ref, cl_ref, cg_ref):
    t = pl.program_id(1)
    tt = WKV_TILE
    n = RWKV_HEAD_DIM
    nck = tt // WKV_CHUNK

    @pl.when(t == 0)
    def _reset():
        state_ref[...] = jnp.zeros_like(state_ref)
        cr_ref[...] = jnp.zeros_like(cr_ref)
        cl_ref[...] = jnp.zeros_like(cl_ref)
        cg_ref[...] = jnp.zeros_like(cg_ref)

    row = lax.broadcasted_iota(jnp.int32, (tt, 1), 0)

    def shifted(z_ref, mix_ref, carry_ref):
        z = z_ref[0]
        prev = jnp.where(row == 0, carry_ref[...], pltpu.roll(z, 1, 0))
        carry_ref[...] = z_ref[0, tt - 1:tt, :]
        return z + mix_ref[...] * (prev - z)

    zr = shifted(zr_ref, mr_ref, cr_ref)
    zl = shifted(zl_ref, ml_ref, cl_ref)
    zg = shifted(zg_ref, mg_ref, cg_ref)
    r = zr[:, 0:RWKV_WIDTH]
    k = zr[:, RWKV_WIDTH:2 * RWKV_WIDTH]
    v = zr[:, 2 * RWKV_WIDTH:3 * RWKV_WIDTH]

    xw = w0_ref[...] + _dot(jnp.tanh(zl[:, 0:DECAY_LORA]), w2_ref[...])
    neg = -xw
    softplus = jnp.maximum(neg, 0.0) + jnp.log(1.0 + jnp.exp(-jnp.abs(neg)))
    lw = -jnp.exp(-softplus - 0.5)
    a = jax.nn.sigmoid(a0_ref[...] + _dot(zl[:, DECAY_LORA:DECAY_LORA + ICLR_LORA], a2_ref[...]))
    gate = _dot(jax.nn.sigmoid(zg), g2_ref[...])

    cum = _dot_exact_lhs(tri_ref[...], lw)
    cum_end = cum.reshape(nck, WKV_CHUNK, RWKV_WIDTH)[:, WKV_CHUNK - 1:WKV_CHUNK, :]
    cum_end_b = jnp.broadcast_to(cum_end, (nck, WKV_CHUNK, RWKV_WIDTH)).reshape(tt, RWKV_WIDTH)
    e_incl = jnp.exp(cum)
    e_excl = jnp.exp(cum - lw)
    e_neg = jnp.exp(-cum)
    e_rev = jnp.exp(cum_end_b - cum)
    g_end = jnp.exp(cum_end)

    ii = lax.broadcasted_iota(jnp.int32, (tt, tt), 0)
    jj = lax.broadcasted_iota(jnp.int32, (tt, tt), 1)
    same = (ii // WKV_CHUNK) == (jj // WKV_CHUNK)
    strict = same & (ii > jj)
    incl = same & (ii >= jj)
    eye_t = (ii == jj).astype(F32)
    i64 = lax.broadcasted_iota(jnp.int32, (n, n), 0)
    j64 = lax.broadcasted_iota(jnp.int32, (n, n), 1)
    eye_n = (i64 == j64).astype(F32)

    for h in range(RWKV_HEADS):
        sl = slice(h * n, (h + 1) * n)
        rh, kh, vh, ah = r[:, sl], k[:, sl], v[:, sl], a[:, sl]
        kkh = kh * kk_ref[:, sl]
        nrm = jnp.sqrt(jnp.sum(kkh * kkh, axis=-1, keepdims=True))
        kkn = kkh / jnp.maximum(nrm, 1e-12)
        kmod = kh * (1.0 + (ah - 1.0) * ka_ref[:, sl])
        av = -kkn
        bv = kkn * ah
        a_t = av * e_excl[:, sl]
        r_t = rh * e_incl[:, sl]
        ar = jnp.concatenate([a_t, r_t], axis=0).astype(BF16)
        bk = jnp.concatenate([bv * e_neg[:, sl], kmod * e_neg[:, sl]], axis=0).astype(BF16)
        mm = _dot_nt(ar, bk)
        l_ab = jnp.where(strict, mm[0:tt, 0:tt], 0.0)
        l_ak = jnp.where(strict, mm[0:tt, tt:2 * tt], 0.0)
        a_rb = jnp.where(incl, mm[tt:2 * tt, 0:tt], 0.0)
        a_rk = jnp.where(incl, mm[tt:2 * tt, tt:2 * tt], 0.0)
        x = l_ab
        tinv = eye_t + l_ab
        for _ in range(int(math.log2(WKV_CHUNK)) - 1):
            x = _dot(x, x)
            tinv = tinv + _dot(tinv, x)
        vb = vh.astype(BF16)
        lv = _dot(jnp.concatenate([l_ak, a_rk], axis=0), vb)
        wu = _dot(tinv, jnp.concatenate([a_t, lv[0:tt]], axis=1))
        qy = _dot(a_rb, wu)
        qp = r_t + qy[:, 0:n]
        y0 = qy[:, n:2 * n] + lv[tt:2 * tt]
        b_end = bv * e_rev[:, sl]
        k_end = kmod * e_rev[:, sl]
        hstate = state_ref[h]
        ys = []
        for c in range(nck):
            cs = slice(c * WKV_CHUNK, (c + 1) * WKV_CHUNK)
            lhs = jnp.concatenate([b_end[cs], k_end[cs]], axis=0)
            rhs = jnp.concatenate(
                [wu[cs], jnp.concatenate([jnp.zeros((WKV_CHUNK, n), F32), vh[cs]], axis=1)], axis=0)
            ph = _dot_tn(lhs, rhs)
            pmat = eye_n * g_end[c, :, sl] + ph[:, 0:n]
            qph = _dot(jnp.concatenate([qp[cs], pmat], axis=0), hstate)
            ys.append(qph[0:WKV_CHUNK] + y0[cs])
            hstate = qph[WKV_CHUNK:WKV_CHUNK + n] + ph[:, n:2 * n]
        state_ref[h] = hstate
        y = jnp.concatenate(ys, axis=0)
        mu = jnp.mean(y, axis=-1, keepdims=True)
        var = jnp.mean(jnp.square(y - mu), axis=-1, keepdims=True)
        yn = (y - mu) * lax.rsqrt(var + GN_EPS) * lng_ref[:, sl] + lnb_ref[:, sl]
        bonus = jnp.sum(rh * kmod * rk_ref[:, sl], axis=-1, keepdims=True) * vh
        out_ref[0, :, sl] = ((yn + bonus) * gate[:, sl]).astype(out_ref.dtype)


def _rwkv_mixer(zr, zl, zg, mix_r, mix_l, mix_g, w0, w2, a0, a2, g2p, k_k, k_a, r_k, ln_g, ln_b):
    bsz, seq, _ = zr.shape
    tt = WKV_TILE
    assert seq % tt == 0
    idx = np.arange(tt)
    tri = ((idx[:, None] // WKV_CHUNK == idx[None, :] // WKV_CHUNK) & (idx[:, None] >= idx[None, :]))
    full = lambda shape: pl.BlockSpec(shape, lambda b, t: (0,) * len(shape))
    row = lambda a: a.reshape(1, -1)
    wl = DECAY_LORA + ICLR_LORA
    return pl.pallas_call(
        _rwkv_kernel,
        grid=(bsz, seq // tt),
        in_specs=[
            pl.BlockSpec((1, tt, 3 * RWKV_WIDTH), lambda b, t: (b, t, 0)),
            pl.BlockSpec((1, tt, wl), lambda b, t: (b, t, 0)),
            pl.BlockSpec((1, tt, GATE_LORA_PAD), lambda b, t: (b, t, 0)),
            full((1, 3 * RWKV_WIDTH)), full((1, wl)), full((1, GATE_LORA_PAD)),
            full((1, RWKV_WIDTH)), full((DECAY_LORA, RWKV_WIDTH)),
            full((1, RWKV_WIDTH)), full((ICLR_LORA, RWKV_WIDTH)),
            full((GATE_LORA_PAD, RWKV_WIDTH)),
            full((1, RWKV_WIDTH)), full((1, RWKV_WIDTH)), full((1, RWKV_WIDTH)),
            full((1, RWKV_WIDTH)), full((1, RWKV_WIDTH)),
            full((tt, tt)),
        ],
        out_specs=pl.BlockSpec((1, tt, RWKV_WIDTH), lambda b, t: (b, t, 0)),
        out_shape=jax.ShapeDtypeStruct((bsz, seq, RWKV_WIDTH), BF16),
        scratch_shapes=[
            pltpu.VMEM((RWKV_HEADS, RWKV_HEAD_DIM, RWKV_HEAD_DIM), F32),
            pltpu.VMEM((1, 3 * RWKV_WIDTH), F32),
            pltpu.VMEM((1, wl), F32),
            pltpu.VMEM((1, GATE_LORA_PAD), F32),
        ],
        compiler_params=pltpu.CompilerParams(
            dimension_semantics=("arbitrary", "arbitrary"), vmem_limit_bytes=VMEM_LIMIT_BYTES),
        name="rwkv7_mixer",
    )(zr, zl, zg, row(mix_r), row(mix_l), row(mix_g), row(w0), w2.astype(BF16), row(a0), a2.astype(BF16),
      g2p.astype(BF16), row(k_k), row(k_a), row(r_k), row(ln_g), row(ln_b), jnp.asarray(tri, BF16))


def _ffn_kernel(x_ref, yn_ref, yr_ref, won_ref, wor_ref, gain_ref, wg_ref, wu_ref, wd_ref, o_ref, *, ff_chunk):
    h = x_ref[...]
    h = h + jnp.dot(yn_ref[...], won_ref[...], preferred_element_type=F32)
    h = h + jnp.dot(yr_ref[...], wor_ref[...], preferred_element_type=F32)
    ms = jnp.mean(h * h, axis=-1, keepdims=True)
    hn = (h * lax.rsqrt(ms + RMS_EPS) * gain_ref[...]).astype(BF16)
    d_ff = wg_ref.shape[1]
    acc = None
    for c in range(d_ff // ff_chunk):
        cs = slice(c * ff_chunk, (c + 1) * ff_chunk)
        gte = jnp.dot(hn, wg_ref[:, cs], preferred_element_type=F32)
        up = jnp.dot(hn, wu_ref[:, cs], preferred_element_type=F32)
        act = (gte * jax.nn.sigmoid(gte) * up).astype(BF16)
        down = jnp.dot(act, wd_ref[cs, :], preferred_element_type=F32)
        acc = down if acc is None else acc + down
    o_ref[...] = h + acc


def _out_ffn(x2, y_nsa, y_rwkv, w_out, ffn_gain, w_gate, w_up, w_down):
    n, d = x2.shape
    d_ff = w_gate.shape[1]
    tm = min(ROW_TILE // 2, n)
    ff_chunk = d_ff // 2 if (d_ff // 2) % LANES == 0 else d_ff
    const = lambda shape: pl.BlockSpec(shape, lambda i: (0, 0))
    return pl.pallas_call(
        functools.partial(_ffn_kernel, ff_chunk=ff_chunk),
        grid=(n // tm,),
        in_specs=[
            pl.BlockSpec((tm, d), lambda i: (i, 0)),
            pl.BlockSpec((tm, NSA_WIDTH), lambda i: (i, 0)),
            pl.BlockSpec((tm, RWKV_WIDTH), lambda i: (i, 0)),
            const((NSA_WIDTH, d)), const((RWKV_WIDTH, d)), const((1, d)),
            const((d, d_ff)), const((d, d_ff)), const((d_ff, d)),
        ],
        out_specs=pl.BlockSpec((tm, d), lambda i: (i, 0)),
        out_shape=jax.ShapeDtypeStruct((n, d), F32),
        compiler_params=pltpu.CompilerParams(
            dimension_semantics=("arbitrary",), vmem_limit_bytes=VMEM_LIMIT_BYTES),
        name="out_proj_ffn",
    )(x2, y_nsa, y_rwkv, w_out[:NSA_WIDTH].astype(BF16), w_out[NSA_WIDTH:].astype(BF16),
      ffn_gain.reshape(1, d), w_gate.astype(BF16), w_up.astype(BF16), w_down.astype(BF16))


def _layer(h, attn_gain, w_in, q_gain, k_gain, cmp_pe, cmp_w1, cmp_b1, cmp_w2, cmp_b2, out_gain, shift_mix,
           w0, w2, a0, a2, g2, k_k, k_a, r_k, ln_g, ln_b, w_out, ffn_gain, w_gate, w_up, w_down, rel_table):
    bsz, seq, d = h.shape
    nsa_cols = NSA_WIDTH + 6 * KV_WIDTH + 3 * NSA_HEADS
    gate0 = NSA_WIDTH + 6 * KV_WIDTH

    w_nsa, w_rw = w_in[:, :nsa_cols], w_in[:, nsa_cols:]
    gate_src = np.array([gate0 + h_ * 3 + br for br in range(3) for h_ in range(NSA_HEADS)])
    w_gates = jnp.zeros((d, LANES), F32).at[:, :3 * NSA_HEADS].set(w_nsa[:, gate_src])
    lora0 = 3 * RWKV_WIDTH
    g0 = lora0 + DECAY_LORA + ICLR_LORA
    w_g = jnp.zeros((d, GATE_LORA_PAD), F32).at[:, :GATE_LORA].set(w_rw[:, g0:g0 + GATE_LORA])
    widths = (NSA_WIDTH, 6 * KV_WIDTH, LANES, 3 * RWKV_WIDTH, DECAY_LORA + ICLR_LORA, GATE_LORA_PAD)
    w_cat = jnp.concatenate([w_nsa[:, :gate0], w_gates, w_rw[:, :g0], w_g], axis=1).astype(BF16)
    mix_g = jnp.zeros((GATE_LORA_PAD,), F32).at[:GATE_LORA].set(shift_mix[g0:g0 + GATE_LORA])
    g2p = jnp.zeros((GATE_LORA_PAD, RWKV_WIDTH), F32).at[:GATE_LORA].set(g2)

    x2 = h.reshape(bsz * seq, d)
    q, kv, gates, zr, zl, zg = _in_projection(x2, attn_gain.reshape(1, d), w_cat, widths)
    r3 = lambda u: u.reshape(bsz, seq, u.shape[-1])
    y_nsa = _nsa_mixer(r3(q), r3(kv), r3(gates), q_gain, k_gain, cmp_pe, cmp_w1, cmp_b1, cmp_w2, cmp_b2,
                       out_gain, rel_table)
    y_rwkv = _rwkv_mixer(r3(zr), r3(zl), r3(zg), shift_mix[:lora0], shift_mix[lora0:g0], mix_g,
                         w0, w2, a0, a2, g2p, k_k, k_a, r_k.reshape(-1), ln_g, ln_b)
    out = _out_ffn(x2, y_nsa.reshape(bsz * seq, NSA_WIDTH), y_rwkv.reshape(bsz * seq, RWKV_WIDTH),
                   w_out, ffn_gain, w_gate, w_up, w_down)
    return out.reshape(bsz, seq, d)


def kernel(x, attn_norm_gain, w_in, nsa_q_gain, nsa_k_gain, cmp_pe, cmp_w1, cmp_b1, cmp_w2, cmp_b2, nsa_out_gain, rwkv_shift_mix, rwkv_w0, rwkv_w2, rwkv_a0, rwkv_a2, rwkv_g2, rwkv_k_k, rwkv_k_a, rwkv_r_k, rwkv_ln_gain, rwkv_ln_bias, w_out, ffn_norm_gain, w_gate, w_up, w_down, rel_bias_table):
    h = x
    for l in range(attn_norm_gain.shape[0]):
        h = _layer(h, attn_norm_gain[l], w_in[l], nsa_q_gain[l], nsa_k_gain[l], cmp_pe[l], cmp_w1[l], cmp_b1[l],
                   cmp_w2[l], cmp_b2[l], nsa_out_gain[l], rwkv_shift_mix[l], rwkv_w0[l], rwkv_w2[l], rwkv_a0[l],
                   rwkv_a2[l], rwkv_g2[l], rwkv_k_k[l], rwkv_k_a[l], rwkv_r_k[l], rwkv_ln_gain[l],
                   rwkv_ln_bias[l], w_out[l], ffn_norm_gain[l], w_gate[l], w_up[l], w_down[l], rel_bias_table)
    return h
```

```python
import functools
import math

import numpy as np
import jax
import jax.numpy as jnp
from jax import lax
from jax.experimental import pallas as pl
from jax.experimental.pallas import tpu as pltpu

HEAD_DIM = 64
NSA_HEADS = 8
NSA_KV_HEADS = 2
NSA_HPG = NSA_HEADS // NSA_KV_HEADS
NSA_WIDTH = NSA_HEADS * HEAD_DIM
KV_WIDTH = NSA_KV_HEADS * HEAD_DIM
CMP_BLOCK = 32
CMP_STRIDE = 16
CMP_HIDDEN = 256
SEL_BLOCK = 64
SEL_TOP_N = 16
SEL_LOCAL = 2
WINDOW = 512
ATTN_SCALE = HEAD_DIM ** -0.5
NEG_INF = -1e30
FORCE_SCORE = 1e9
REL_BUCKETS = 32
REL_MAX_DIST = 128
RWKV_HEADS = 8
RWKV_HEAD_DIM = 64
RWKV_WIDTH = RWKV_HEADS * RWKV_HEAD_DIM
DECAY_LORA = 64
ICLR_LORA = 64
GATE_LORA = 160
GATE_LORA_PAD = 256
GN_EPS = 64e-5
RMS_EPS = 1e-6

LANES = 128
Q_TILE = 128
K_TILE = 128
WKV_TILE = 256
WKV_CHUNK = 64
ROW_TILE = 512
VMEM_LIMIT_BYTES = 56 * 1024 * 1024

F32 = jnp.float32
BF16 = jnp.bfloat16


def _dot(a, b):
    return jnp.dot(a.astype(BF16), b.astype(BF16), preferred_element_type=F32)


def _dot_nt(a, b):
    return lax.dot_general(a.astype(BF16), b.astype(BF16), (((1,), (1,)), ((), ())),
                           preferred_element_type=F32)


def _dot_tn(a, b):
    return lax.dot_general(a.astype(BF16), b.astype(BF16), (((0,), (0,)), ((), ())),
                           preferred_element_type=F32)


def _split3(x):
    hi = x.astype(BF16)
    r1 = x - hi.astype(F32)
    mid = r1.astype(BF16)
    lo = (r1 - mid.astype(F32)).astype(BF16)
    return hi, mid, lo


def _dot_exact_rhs(x, m_bf16):
    hi, mid, lo = _split3(x)
    acc = jnp.dot(lo, m_bf16, preferred_element_type=F32)
    acc = acc + jnp.dot(mid, m_bf16, preferred_element_type=F32)
    return acc + jnp.dot(hi, m_bf16, preferred_element_type=F32)


def _dot_exact_lhs(m_bf16, x):
    hi, mid, lo = _split3(x)
    acc = jnp.dot(m_bf16, lo, preferred_element_type=F32)
    acc = acc + jnp.dot(m_bf16, mid, preferred_element_type=F32)
    return acc + jnp.dot(m_bf16, hi, preferred_element_type=F32)


def _rel_bucket_np(dist):
    max_exact = REL_BUCKETS // 2
    d = np.maximum(dist, 0)
    ratio = np.maximum(d, 1).astype(np.float32) / np.float32(max_exact)
    log_ratio = np.log(ratio).astype(np.float32) / np.float32(math.log(REL_MAX_DIST / max_exact))
    large = np.minimum(max_exact + (log_ratio * np.float32(REL_BUCKETS - max_exact)).astype(np.int32),
                       REL_BUCKETS - 1)
    return np.where(d < max_exact, d, large).astype(np.int32)


def _inproj_kernel(x_ref, gain_ref, w_ref, *out_refs, widths):
    x = x_ref[...]
    ms = jnp.mean(x * x, axis=-1, keepdims=True)
    xn = (x * lax.rsqrt(ms + RMS_EPS) * gain_ref[...]).astype(BF16)
    off = 0
    for o_ref, w in zip(out_refs, widths):
        o_ref[...] = jnp.dot(xn, w_ref[:, off:off + w], preferred_element_type=F32)
        off += w


def _in_projection(x2, gain, w_cat, widths):
    n, d = x2.shape
    tm = min(ROW_TILE, n)
    total = sum(widths)
    return pl.pallas_call(
        functools.partial(_inproj_kernel, widths=widths),
        grid=(n // tm,),
        in_specs=[
            pl.BlockSpec((tm, d), lambda i: (i, 0)),
            pl.BlockSpec((1, d), lambda i: (0, 0)),
            pl.BlockSpec((d, total), lambda i: (0, 0)),
        ],
        out_specs=[pl.BlockSpec((tm, w), lambda i: (i, 0)) for w in widths],
        out_shape=[jax.ShapeDtypeStruct((n, w), F32) for w in widths],
        compiler_params=pltpu.CompilerParams(
            dimension_semantics=("arbitrary",), vmem_limit_bytes=VMEM_LIMIT_BYTES),
        name="in_projection",
    )(x2, gain, w_cat)


def _head_rms(u, gain):
    ms = jnp.mean(u * u, axis=-1, keepdims=True)
    return u * lax.rsqrt(ms + RMS_EPS) * gain


def _flash_step(s, vt_blk, m, l, acc):
    m_new = jnp.maximum(m, jnp.max(s, axis=0, keepdims=True))
    alpha = jnp.exp(m - m_new)
    p = jnp.exp(s - m_new)
    l_new = alpha * l + jnp.sum(p, axis=0, keepdims=True)
    acc_new = alpha * acc + jnp.dot(vt_blk, p.astype(BF16), preferred_element_type=F32)
    return m_new, l_new, acc_new


def _nsa_kernel(tab_ref,
                q_ref, kv_ref, gate_ref,
                qg_ref, kg_ref, pe_ref, w1_ref, b1_ref, w2_ref, b2_ref, og_ref,
                bk0_ref, bk1_ref, bkc_ref, ovl_ref,
                out_ref,
                kvpad_ref, kcmp_ref, vcmp_ref, ksel_ref, vsel_ref, kwin_ref, vwin_ref,
                btab_ref, ccan_ref, nmask_ref, ybuf_ref,
                *, seq):
    b = pl.program_id(0)
    qt = pl.program_id(1)
    n_cb = seq // CMP_STRIDE
    n_c = (seq - CMP_BLOCK) // CMP_STRIDE + 1
    n_sel = seq // SEL_BLOCK
    cols = NSA_HPG * Q_TILE

    @pl.when((b == 0) & (qt == 0))
    def _build_bias_tables():
        bk0 = bk0_ref[...]
        bk1 = bk1_ref[...]
        bkc = bkc_ref[...]
        kk = lax.broadcasted_iota(jnp.int32, (K_TILE, Q_TILE), 0)
        qq = lax.broadcasted_iota(jnp.int32, (K_TILE, Q_TILE), 1)
        causal = jnp.where(kk <= qq, 0.0, NEG_INF).astype(F32)
        anti = jnp.where(kk > qq, 0.0, NEG_INF).astype(F32)
        for h in range(NSA_HEADS):
            t0 = jnp.zeros((K_TILE, Q_TILE), F32)
            t1 = jnp.zeros((K_TILE, Q_TILE), F32)
            tc = jnp.zeros((2 * LANES, Q_TILE), F32)
            for k in range(REL_BUCKETS):
                val = tab_ref[k, h]
                t0 = jnp.where(bk0 == k, val, t0)
                t1 = jnp.where(bk1 == k, val, t1)
                tc = jnp.where(bkc == k, val, tc)
            far = tab_ref[REL_BUCKETS - 1, h]
            g, p = divmod(h, NSA_HPG)
            cs = slice(p * Q_TILE, (p + 1) * Q_TILE)
            btab_ref[g, 0, :, cs] = jnp.zeros((K_TILE, Q_TILE), F32)
            btab_ref[g, 1, :, cs] = t1 - far
            btab_ref[g, 2, :, cs] = t0 - far + causal
            btab_ref[g, 3, :, cs] = anti
            ccan_ref[h] = tc

    @pl.when(qt == 0)
    def _per_batch():
        for which in range(2):
            kvpad_ref[which, seq:seq + 2 * CMP_STRIDE, :] = jnp.zeros((2 * CMP_STRIDE, KV_WIDTH), F32)
            kvpad_ref[which, 0:seq, :] = kv_ref[0, :, which * KV_WIDTH:(which + 1) * KV_WIDTH]
            cmps = []
            for g in range(NSA_KV_HEADS):
                c0 = g * HEAD_DIM
                hid = jnp.zeros((n_cb, CMP_HIDDEN), F32)
                for l in range(CMP_BLOCK):
                    rows_l = kvpad_ref[which, pl.ds(l, n_cb, stride=CMP_STRIDE), :]
                    piece = rows_l[:, c0:c0 + HEAD_DIM] + pe_ref[which, l:l + 1, :]
                    hid = hid + _dot(piece, w1_ref[which, l])
                hid = jax.nn.gelu(hid + b1_ref[which], approximate=True)
                cmps.append(_dot(hid, w2_ref[which]) + b2_ref[which])
            if which == 0:
                for g in range(NSA_KV_HEADS):
                    kcmp_ref[g] = _head_rms(cmps[g], kg_ref[0:1, :]).astype(BF16)
            else:
                vct = jnp.concatenate(cmps, axis=1).T
                for g in range(NSA_KV_HEADS):
                    vcmp_ref[g] = vct[g * HEAD_DIM:(g + 1) * HEAD_DIM].astype(BF16)
        kv = kv_ref[0]
        for g in range(NSA_KV_HEADS):
            def col(i):
                c = (2 + i) * KV_WIDTH + g * HEAD_DIM
                return kv[:, c:c + HEAD_DIM]
            ksel_ref[g] = _head_rms(col(0), kg_ref[1:2, :]).astype(BF16)
            kwin_ref[g] = _head_rms(col(2), kg_ref[2:3, :]).astype(BF16)
        for i, vt_ref in ((1, vsel_ref), (3, vwin_ref)):
            vt = kv[:, (2 + i) * KV_WIDTH:(3 + i) * KV_WIDTH].T
            for g in range(NSA_KV_HEADS):
                vt_ref[g] = vt[g * HEAD_DIM:(g + 1) * HEAD_DIM].astype(BF16)

    t0 = qt * Q_TILE
    qT = q_ref[0].T
    gates = jax.nn.sigmoid(gate_ref[0].T)
    tq = t0 + lax.broadcasted_iota(jnp.int32, (1, Q_TILE), 1)
    ssq = jnp.zeros((1, Q_TILE), F32)
    n_selp = -(-n_sel // 8) * 8

    for g in range(NSA_KV_HEADS):
        qs = []
        for p in range(NSA_HPG):
            h = g * NSA_HPG + p
            qh = qT[h * HEAD_DIM:(h + 1) * HEAD_DIM]
            ms = jnp.mean(qh * qh, axis=0, keepdims=True)
            qs.append((qh * lax.rsqrt(ms + RMS_EPS) * qg_ref[...] * ATTN_SCALE).astype(BF16))
        q_cat = jnp.concatenate(qs, axis=1)

        s = jnp.dot(kcmp_ref[g], q_cat, preferred_element_type=F32)
        c_start = pl.multiple_of(LANES - (CMP_STRIDE // 2) * qt, 8)
        cbias = jnp.concatenate(
            [ccan_ref[g * NSA_HPG + p, pl.ds(c_start, n_cb), :] for p in range(NSA_HPG)], axis=1)
        cc = lax.broadcasted_iota(jnp.int32, (n_cb, Q_TILE), 0)
        valid = (cc * CMP_STRIDE + (CMP_BLOCK - 1) <= tq) & (cc < n_c)
        valid4 = jnp.concatenate([valid] * NSA_HPG, axis=1)
        s = jnp.where(valid4, s + cbias, NEG_INF)
        m = jnp.max(s, axis=0, keepdims=True)
        e = jnp.exp(s - m)
        pc = e / jnp.sum(e, axis=0, keepdims=True) * valid4.astype(F32)
        o_cmp = jnp.dot(vcmp_ref[g], pc.astype(BF16), preferred_element_type=F32)
        psum = pc[:, 0:Q_TILE]
        for p in range(1, NSA_HPG):
            psum = psum + pc[:, p * Q_TILE:(p + 1) * Q_TILE]

        imp = _dot_exact_lhs(ovl_ref[...], psum)[0:n_selp]
        jb = lax.broadcasted_iota(jnp.int32, (n_selp, Q_TILE), 0)
        qb = tq // SEL_BLOCK
        forced = (jb == 0) | ((jb <= qb) & (jb > qb - SEL_LOCAL))
        score = jnp.where(forced, FORCE_SCORE, imp)
        score = jnp.where(jb <= qb, score, -1.0)
        score = jnp.where(jb < n_sel, score, -2.0)
        rank = jnp.zeros((n_selp, Q_TILE), F32)
        for i in range(n_sel):
            si = score[i:i + 1, :]
            ahead = (si > score) | ((si == score) & (jb > i))
            rank = rank + ahead.astype(F32)
        sel = (rank < float(min(SEL_TOP_N, n_sel))) & (jb < n_sel)
        nmask_ref[...] = (sel.astype(F32) - 1.0) * (-NEG_INF)

        half = K_TILE // SEL_BLOCK

        def sel_body(kb, carry):
            m, l, acc = carry
            k0 = pl.multiple_of(kb * K_TILE, K_TILE)
            kind = jnp.clip(kb - qt + 2, 0, 2)
            s = jnp.dot(ksel_ref[g, pl.ds(k0, K_TILE), :], q_cat, preferred_element_type=F32)
            nm = jnp.concatenate(
                [jnp.broadcast_to(nmask_ref[pl.ds(kb * half + j, 1), :], (SEL_BLOCK, Q_TILE))
                 for j in range(half)], axis=0)
            s = s + btab_ref[g, kind] + jnp.concatenate([nm] * NSA_HPG, axis=1)
            return _flash_step(s, vsel_ref[g, :, pl.ds(k0, K_TILE)], m, l, acc)

        init = (jnp.full((1, cols), NEG_INF, F32), jnp.zeros((1, cols), F32),
                jnp.zeros((HEAD_DIM, cols), F32))
        _, l_s, acc_s = lax.fori_loop(0, qt + 1, sel_body, init)
        o_sel = acc_s / l_s

        n_wb = WINDOW // K_TILE

        def win_body(kb, carry):
            m, l, acc = carry
            k0 = pl.multiple_of(kb * K_TILE, K_TILE)
            rel = qt - kb
            kind = jnp.where(rel == 0, 2, jnp.where(rel == 1, 1, jnp.where(rel == n_wb, 3, 0)))
            s = jnp.dot(kwin_ref[g, pl.ds(k0, K_TILE), :], q_cat, preferred_element_type=F32)
            return _flash_step(s + btab_ref[g, kind], vwin_ref[g, :, pl.ds(k0, K_TILE)], m, l, acc)

        _, l_w, acc_w = lax.fori_loop(jnp.maximum(qt - n_wb, 0), qt + 1, win_body, init)
        o_win = acc_w / l_w

        for p in range(NSA_HPG):
            h = g * NSA_HPG + p
            cs = slice(p * Q_TILE, (p + 1) * Q_TILE)
            o = (gates[h:h + 1] * o_cmp[:, cs]
                 + gates[NSA_HEADS + h:NSA_HEADS + h + 1] * o_sel[:, cs]
                 + gates[2 * NSA_HEADS + h:2 * NSA_HEADS + h + 1] * o_win[:, cs])
            ssq = ssq + jnp.sum(o * o, axis=0, keepdims=True)
            ybuf_ref[h * HEAD_DIM:(h + 1) * HEAD_DIM, :] = o

    yT = ybuf_ref[...] * lax.rsqrt(ssq / NSA_WIDTH + RMS_EPS) * og_ref[...]
    out_ref[0] = yT.T.astype(out_ref.dtype)


def _nsa_mixer(q, kv, gates, q_gain, k_gain, cmp_pe, cmp_w1, cmp_b1, cmp_w2, cmp_b2, out_gain, rel_table):
    bsz, seq, _ = q.shape
    assert seq % Q_TILE == 0 and seq // CMP_STRIDE == LANES and seq // SEL_BLOCK <= LANES
    n_qt = seq // Q_TILE
    n_cb = seq // CMP_STRIDE
    n_sel = seq // SEL_BLOCK
    n_selp = -(-n_sel // 8) * 8
    cols = NSA_HPG * Q_TILE

    kk = np.arange(K_TILE)[:, None]
    qq = np.arange(Q_TILE)[None, :]
    bk0 = _rel_bucket_np(qq - kk)
    bk1 = _rel_bucket_np(K_TILE + qq - kk)
    cprime = np.arange(2 * LANES)[:, None] - LANES
    bkc = _rel_bucket_np(qq - (CMP_BLOCK - 1) - CMP_STRIDE * cprime)
    c_start = np.arange(n_cb)[None, :] * CMP_STRIDE
    j_start = np.arange(LANES)[:, None] * SEL_BLOCK
    ovl = np.clip(np.minimum(c_start + CMP_BLOCK, j_start + SEL_BLOCK) - np.maximum(c_start, j_start), 0, None)
    ovl = (ovl.astype(np.float32) / CMP_STRIDE)
    ovl[n_sel:, :] = 0.0

    w1 = cmp_w1.reshape(2, CMP_BLOCK, HEAD_DIM, CMP_HIDDEN).astype(BF16)
    full = lambda shape: pl.BlockSpec(shape, lambda b, t: (0,) * len(shape))
    kernel = functools.partial(_nsa_kernel, seq=seq)
    return pl.pallas_call(
        kernel,
        grid=(bsz, n_qt),
        in_specs=[
            pl.BlockSpec(memory_space=pltpu.SMEM),
            pl.BlockSpec((1, Q_TILE, NSA_WIDTH), lambda b, t: (b, t, 0)),
            pl.BlockSpec((1, seq, 6 * KV_WIDTH), lambda b, t: (b, 0, 0)),
            pl.BlockSpec((1, Q_TILE, LANES), lambda b, t: (b, t, 0)),
            full((HEAD_DIM, Q_TILE)),
            full((3, HEAD_DIM)),
            full((2, CMP_BLOCK, HEAD_DIM)),
            full((2, CMP_BLOCK, HEAD_DIM, CMP_HIDDEN)),
            full((2, 1, CMP_HIDDEN)),
            full((2, CMP_HIDDEN, HEAD_DIM)),
            full((2, 1, HEAD_DIM)),
            full((NSA_WIDTH, Q_TILE)),
            full((K_TILE, Q_TILE)),
            full((K_TILE, Q_TILE)),
            full((2 * LANES, Q_TILE)),
            full((LANES, n_cb)),
        ],
        out_specs=pl.BlockSpec((1, Q_TILE, NSA_WIDTH), lambda b, t: (b, t, 0)),
        out_shape=jax.ShapeDtypeStruct((bsz, seq, NSA_WIDTH), BF16),
        scratch_shapes=[
            pltpu.VMEM((2, seq + 2 * CMP_STRIDE, KV_WIDTH), F32),
            pltpu.VMEM((NSA_KV_HEADS, n_cb, HEAD_DIM), BF16),
            pltpu.VMEM((NSA_KV_HEADS, HEAD_DIM, n_cb), BF16),
            pltpu.VMEM((NSA_KV_HEADS, seq, HEAD_DIM), BF16),
            pltpu.VMEM((NSA_KV_HEADS, HEAD_DIM, seq), BF16),
            pltpu.VMEM((NSA_KV_HEADS, seq, HEAD_DIM), BF16),
            pltpu.VMEM((NSA_KV_HEADS, HEAD_DIM, seq), BF16),
            pltpu.VMEM((NSA_KV_HEADS, 4, K_TILE, cols), F32),
            pltpu.VMEM((NSA_HEADS, 2 * LANES, Q_TILE), F32),
            pltpu.VMEM((n_selp, Q_TILE), F32),
            pltpu.VMEM((NSA_WIDTH, Q_TILE), F32),
        ],
        compiler_params=pltpu.CompilerParams(
            dimension_semantics=("arbitrary", "arbitrary"), vmem_limit_bytes=VMEM_LIMIT_BYTES),
        name="nsa_mixer",
    )(rel_table, q, kv, gates, jnp.broadcast_to(q_gain[:, None], (HEAD_DIM, Q_TILE)), k_gain, cmp_pe, w1,
      cmp_b1.reshape(2, 1, CMP_HIDDEN), cmp_w2.astype(BF16), cmp_b2.reshape(2, 1, HEAD_DIM),
      jnp.broadcast_to(out_gain[:, None], (NSA_WIDTH, Q_TILE)), jnp.asarray(bk0), jnp.asarray(bk1),
      jnp.asarray(bkc), jnp.asarray(ovl, BF16))


def _rwkv_kernel(zr_ref, zl_ref, zg_ref, mr_ref, ml_ref, mg_ref, w0_ref, w2_ref, a0_ref, a2_ref, g2_ref,
                 kk_ref, ka_ref, rk_ref, lng_ref, lnb_ref, tri_ref,
                 out_ref,
                 state_ref, cr_ref, cl_ref, cg_ref):
    t = pl.program_id(1)
    tt = WKV_TILE
    n = RWKV_HEAD_DIM
    nck = tt // WKV_CHUNK

    @pl.when(t == 0)
    def _reset():
        state_ref[...] = jnp.zeros_like(state_ref)
        cr_ref[...] = jnp.zeros_like(cr_ref)
        cl_ref[...] = jnp.zeros_like(cl_ref)
        cg_ref[...] = jnp.zeros_like(cg_ref)

    row = lax.broadcasted_iota(jnp.int32, (tt, 1), 0)

    def shifted(z_ref, mix_ref, carry_ref):
        z = z_ref[0]
        prev = jnp.where(row == 0, carry_ref[...], pltpu.roll(z, 1, 0))
        carry_ref[...] = z_ref[0, tt - 1:tt, :]
        return z + mix_ref[...] * (prev - z)

    zr = shifted(zr_ref, mr_ref, cr_ref)
    zl = shifted(zl_ref, ml_ref, cl_ref)
    zg = shifted(zg_ref, mg_ref, cg_ref)
    r = zr[:, 0:RWKV_WIDTH]
    k = zr[:, RWKV_WIDTH:2 * RWKV_WIDTH]
    v = zr[:, 2 * RWKV_WIDTH:3 * RWKV_WIDTH]

    xw = w0_ref[...] + _dot(jnp.tanh(zl[:, 0:DECAY_LORA]), w2_ref[...])
    neg = -xw
    softplus = jnp.maximum(neg, 0.0) + jnp.log(1.0 + jnp.exp(-jnp.abs(neg)))
    lw = -jnp.exp(-softplus - 0.5)
    a = jax.nn.sigmoid(a0_ref[...] + _dot(zl[:, DECAY_LORA:DECAY_LORA + ICLR_LORA], a2_ref[...]))
    gate = _dot(jax.nn.sigmoid(zg), g2_ref[...])

    cum = _dot_exact_lhs(tri_ref[...], lw)
    cum_end = cum.reshape(nck, WKV_CHUNK, RWKV_WIDTH)[:, WKV_CHUNK - 1:WKV_CHUNK, :]
    cum_end_b = jnp.broadcast_to(cum_end, (nck, WKV_CHUNK, RWKV_WIDTH)).reshape(tt, RWKV_WIDTH)
    e_incl = jnp.exp(cum)
    e_excl = jnp.exp(cum - lw)
    e_neg = jnp.exp(-cum)
    e_rev = jnp.exp(cum_end_b - cum)
    g_end = jnp.exp(cum_end)

    ii = lax.broadcasted_iota(jnp.int32, (tt, tt), 0)
    jj = lax.broadcasted_iota(jnp.int32, (tt, tt), 1)
    same = (ii // WKV_CHUNK) == (jj // WKV_CHUNK)
    strict = same & (ii > jj)
    incl = same & (ii >= jj)
    eye_t = (ii == jj).astype(F32)
    i64 = lax.broadcasted_iota(jnp.int32, (n, n), 0)
    j64 = lax.broadcasted_iota(jnp.int32, (n, n), 1)
    eye_n = (i64 == j64).astype(F32)

    for h in range(RWKV_HEADS):
        sl = slice(h * n, (h + 1) * n)
        rh, kh, vh, ah = r[:, sl], k[:, sl], v[:, sl], a[:, sl]
        kkh = kh * kk_ref[:, sl]
        nrm = jnp.sqrt(jnp.sum(kkh * kkh, axis=-1, keepdims=True))
        kkn = kkh / jnp.maximum(nrm, 1e-12)
        kmod = kh * (1.0 + (ah - 1.0) * ka_ref[:, sl])
        av = -kkn
        bv = kkn * ah
        a_t = av * e_excl[:, sl]
        r_t = rh * e_incl[:, sl]
        ar = jnp.concatenate([a_t, r_t], axis=0).astype(BF16)
        bk = jnp.concatenate([bv * e_neg[:, sl], kmod * e_neg[:, sl]], axis=0).astype(BF16)
        mm = _dot_nt(ar, bk)
        l_ab = jnp.where(strict, mm[0:tt, 0:tt], 0.0)
        l_ak = jnp.where(strict, mm[0:tt, tt:2 * tt], 0.0)
        a_rb = jnp.where(incl, mm[tt:2 * tt, 0:tt], 0.0)
        a_rk = jnp.where(incl, mm[tt:2 * tt, tt:2 * tt], 0.0)
        x = l_ab
        tinv = eye_t + l_ab
        for _ in range(int(math.log2(WKV_CHUNK)) - 1):
            x = _dot(x, x)
            tinv = tinv + _dot(tinv, x)
        vb = vh.astype(BF16)
        lv = _dot(jnp.concatenate([l_ak, a_rk], axis=0), vb)
        wu = _dot(tinv, jnp.concatenate([a_t, lv[0:tt]], axis=1))
        qy = _dot(a_rb, wu)
        qp = r_t + qy[:, 0:n]
        y0 = qy[:, n:2 * n] + lv[tt:2 * tt]
        b_end = bv * e_rev[:, sl]
        k_end = kmod * e_rev[:, sl]
        hstate = state_ref[h]
        ys = []
        for c in range(nck):
            cs = slice(c * WKV_CHUNK, (c + 1) * WKV_CHUNK)
            lhs = jnp.concatenate([b_end[cs], k_end[cs]], axis=0)
            rhs = jnp.concatenate(
                [wu[cs], jnp.concatenate([jnp.zeros((WKV_CHUNK, n), F32), vh[cs]], axis=1)], axis=0)
            ph = _dot_tn(lhs, rhs)
            pmat = eye_n * g_end[c, :, sl] + ph[:, 0:n]
            qph = _dot(jnp.concatenate([qp[cs], pmat], axis=0), hstate)
            ys.append(qph[0:WKV_CHUNK] + y0[cs])
            hstate = qph[WKV_CHUNK:WKV_CHUNK + n] + ph[:, n:2 * n]
        state_ref[h] = hstate
        y = jnp.concatenate(ys, axis=0)
        mu = jnp.mean(y, axis=-1, keepdims=True)
        var = jnp.mean(jnp.square(y - mu), axis=-1, keepdims=True)
        yn = (y - mu) * lax.rsqrt(var + GN_EPS) * lng_ref[:, sl] + lnb_ref[:, sl]
        bonus = jnp.sum(rh * kmod * rk_ref[:, sl], axis=-1, keepdims=True) * vh
        out_ref[0, :, sl] = ((yn + bonus) * gate[:, sl]).astype(out_ref.dtype)


def _rwkv_mixer(zr, zl, zg, mix_r, mix_l, mix_g, w0, w2, a0, a2, g2p, k_k, k_a, r_k, ln_g, ln_b):
    bsz, seq, _ = zr.shape
    tt = WKV_TILE
    assert seq % tt == 0
    idx = np.arange(tt)
    tri = ((idx[:, None] // WKV_CHUNK == idx[None, :] // WKV_CHUNK) & (idx[:, None] >= idx[None, :]))
    full = lambda shape: pl.BlockSpec(shape, lambda b, t: (0,) * len(shape))
    row = lambda a: a.reshape(1, -1)
    wl = DECAY_LORA + ICLR_LORA
    return pl.pallas_call(
        _rwkv_kernel,
        grid=(bsz, seq // tt),
        in_specs=[
            pl.BlockSpec((1, tt, 3 * RWKV_WIDTH), lambda b, t: (b, t, 0)),
            pl.BlockSpec((1, tt, wl), lambda b, t: (b, t, 0)),
            pl.BlockSpec((1, tt, GATE_LORA_PAD), lambda b, t: (b, t, 0)),
            full((1, 3 * RWKV_WIDTH)), full((1, wl)), full((1, GATE_LORA_PAD)),
            full((1, RWKV_WIDTH)), full((DECAY_LORA, RWKV_WIDTH)),
            full((1, RWKV_WIDTH)), full((ICLR_LORA, RWKV_WIDTH)),
            full((GATE_LORA_PAD, RWKV_WIDTH)),
            full((1, RWKV_WIDTH)), full((1, RWKV_WIDTH)), full((1, RWKV_WIDTH)),
            full((1, RWKV_WIDTH)), full((1, RWKV_WIDTH)),
            full((tt, tt)),
        ],
        out_specs=pl.BlockSpec((1, tt, RWKV_WIDTH), lambda b, t: (b, t, 0)),
        out_shape=jax.ShapeDtypeStruct((bsz, seq, RWKV_WIDTH), BF16),
        scratch_shapes=[
            pltpu.VMEM((RWKV_HEADS, RWKV_HEAD_DIM, RWKV_HEAD_DIM), F32),
            pltpu.VMEM((1, 3 * RWKV_WIDTH), F32),
            pltpu.VMEM((1, wl), F32),
            pltpu.VMEM((1, GATE_LORA_PAD), F32),
        ],
        compiler_params=pltpu.CompilerParams(
            dimension_semantics=("arbitrary", "arbitrary"), vmem_limit_bytes=VMEM_LIMIT_BYTES),
        name="rwkv7_mixer",
    )(zr, zl, zg, row(mix_r), row(mix_l), row(mix_g), row(w0), w2.astype(BF16), row(a0), a2.astype(BF16),
      g2p.astype(BF16), row(k_k), row(k_a), row(r_k), row(ln_g), row(ln_b), jnp.asarray(tri, BF16))


def _ffn_kernel(x_ref, yn_ref, yr_ref, won_ref, wor_ref, gain_ref, wg_ref, wu_ref, wd_ref, o_ref, *, ff_chunk):
    h = x_ref[...]
    h = h + jnp.dot(yn_ref[...], won_ref[...], preferred_element_type=F32)
    h = h + jnp.dot(yr_ref[...], wor_ref[...], preferred_element_type=F32)
    ms = jnp.mean(h * h, axis=-1, keepdims=True)
    hn = (h * lax.rsqrt(ms + RMS_EPS) * gain_ref[...]).astype(BF16)
    d_ff = wg_ref.shape[1]
    acc = None
    for c in range(d_ff // ff_chunk):
        cs = slice(c * ff_chunk, (c + 1) * ff_chunk)
        gte = jnp.dot(hn, wg_ref[:, cs], preferred_element_type=F32)
        up = jnp.dot(hn, wu_ref[:, cs], preferred_element_type=F32)
        act = (gte * jax.nn.sigmoid(gte) * up).astype(BF16)
        down = jnp.dot(act, wd_ref[cs, :], preferred_element_type=F32)
        acc = down if acc is None else acc + down
    o_ref[...] = h + acc


def _out_ffn(x2, y_nsa, y_rwkv, w_out, ffn_gain, w_gate, w_up, w_down):
    n, d = x2.shape
    d_ff = w_gate.shape[1]
    tm = min(ROW_TILE // 2, n)
    ff_chunk = d_ff // 2 if (d_ff // 2) % LANES == 0 else d_ff
    const = lambda shape: pl.BlockSpec(shape, lambda i: (0, 0))
    return pl.pallas_call(
        functools.partial(_ffn_kernel, ff_chunk=ff_chunk),
        grid=(n // tm,),
        in_specs=[
            pl.BlockSpec((tm, d), lambda i: (i, 0)),
            pl.BlockSpec((tm, NSA_WIDTH), lambda i: (i, 0)),
            pl.BlockSpec((tm, RWKV_WIDTH), lambda i: (i, 0)),
            const((NSA_WIDTH, d)), const((RWKV_WIDTH, d)), const((1, d)),
            const((d, d_ff)), const((d, d_ff)), const((d_ff, d)),
        ],
        out_specs=pl.BlockSpec((tm, d), lambda i: (i, 0)),
        out_shape=jax.ShapeDtypeStruct((n, d), F32),
        compiler_params=pltpu.CompilerParams(
            dimension_semantics=("arbitrary",), vmem_limit_bytes=VMEM_LIMIT_BYTES),
        name="out_proj_ffn",
    )(x2, y_nsa, y_rwkv, w_out[:NSA_WIDTH].astype(BF16), w_out[NSA_WIDTH:].astype(BF16),
      ffn_gain.reshape(1, d), w_gate.astype(BF16), w_up.astype(BF16), w_down.astype(BF16))


def _layer(h, attn_gain, w_in, q_gain, k_gain, cmp_pe, cmp_w1, cmp_b1, cmp_w2, cmp_b2, out_gain, shift_mix,
           w0, w2, a0, a2, g2, k_k, k_a, r_k, ln_g, ln_b, w_out, ffn_gain, w_gate, w_up, w_down, rel_table):
    bsz, seq, d = h.shape
    nsa_cols = NSA_WIDTH + 6 * KV_WIDTH + 3 * NSA_HEADS
    gate0 = NSA_WIDTH + 6 * KV_WIDTH

    w_nsa, w_rw = w_in[:, :nsa_cols], w_in[:, nsa_cols:]
    gate_src = np.array([gate0 + h_ * 3 + br for br in range(3) for h_ in range(NSA_HEADS)])
    w_gates = jnp.zeros((d, LANES), F32).at[:, :3 * NSA_HEADS].set(w_nsa[:, gate_src])
    lora0 = 3 * RWKV_WIDTH
    g0 = lora0 + DECAY_LORA + ICLR_LORA
    w_g = jnp.zeros((d, GATE_LORA_PAD), F32).at[:, :GATE_LORA].set(w_rw[:, g0:g0 + GATE_LORA])
    widths = (NSA_WIDTH, 6 * KV_WIDTH, LANES, 3 * RWKV_WIDTH, DECAY_LORA + ICLR_LORA, GATE_LORA_PAD)
    w_cat = jnp.concatenate([w_nsa[:, :gate0], w_gates, w_rw[:, :g0], w_g], axis=1).astype(BF16)
    mix_g = jnp.zeros((GATE_LORA_PAD,), F32).at[:GATE_LORA].set(shift_mix[g0:g0 + GATE_LORA])
    g2p = jnp.zeros((GATE_LORA_PAD, RWKV_WIDTH), F32).at[:GATE_LORA].set(g2)

    x2 = h.reshape(bsz * seq, d)
    q, kv, gates, zr, zl, zg = _in_projection(x2, attn_gain.reshape(1, d), w_cat, widths)
    r3 = lambda u: u.reshape(bsz, seq, u.shape[-1])
    y_nsa = _nsa_mixer(r3(q), r3(kv), r3(gates), q_gain, k_gain, cmp_pe, cmp_w1, cmp_b1, cmp_w2, cmp_b2,
                       out_gain, rel_table)
    y_rwkv = _rwkv_mixer(r3(zr), r3(zl), r3(zg), shift_mix[:lora0], shift_mix[lora0:g0], mix_g,
                         w0, w2, a0, a2, g2p, k_k, k_a, r_k.reshape(-1), ln_g, ln_b)
    out = _out_ffn(x2, y_nsa.reshape(bsz * seq, NSA_WIDTH), y_rwkv.reshape(bsz * seq, RWKV_WIDTH),
                   w_out, ffn_gain, w_gate, w_up, w_down)
    return out.reshape(bsz, seq, d)


def kernel(x, attn_norm_gain, w_in, nsa_q_gain, nsa_k_gain, cmp_pe, cmp_w1, cmp_b1, cmp_w2, cmp_b2, nsa_out_gain, rwkv_shift_mix, rwkv_w0, rwkv_w2, rwkv_a0, rwkv_a2, rwkv_g2, rwkv_k_k, rwkv_k_a, rwkv_r_k, rwkv_ln_gain, rwkv_ln_bias, w_out, ffn_norm_gain, w_gate, w_up, w_down, rel_bias_table):
    h = x
    for l in range(attn_norm_gain.shape[0]):
        h = _layer(h, attn_norm_gain[l], w_in[l], nsa_q_gain[l], nsa_k_gain[l], cmp_pe[l], cmp_w1[l], cmp_b1[l],
                   cmp_w2[l], cmp_b2[l], nsa_out_gain[l], rwkv_shift_mix[l], rwkv_w0[l], rwkv_w2[l], rwkv_a0[l],
                   rwkv_a2[l], rwkv_g2[l], rwkv_k_k[l], rwkv_k_a[l], rwkv_r_k[l], rwkv_ln_gain[l],
                   rwkv_ln_bias[l], w_out[l], ffn_norm_gain[l], w_gate[l], w_up[l], w_down[l], rel_bias_table)
    return h
```

```python
import functools
import math

import numpy as np
import jax
import jax.numpy as jnp
from jax import lax
from jax.experimental import pallas as pl
from jax.experimental.pallas import tpu as pltpu

HEAD_DIM = 64
NSA_HEADS = 8
NSA_KV_HEADS = 2
NSA_HPG = NSA_HEADS // NSA_KV_HEADS
NSA_WIDTH = NSA_HEADS * HEAD_DIM
KV_WIDTH = NSA_KV_HEADS * HEAD_DIM
CMP_BLOCK = 32
CMP_STRIDE = 16
CMP_HIDDEN = 256
SEL_BLOCK = 64
SEL_TOP_N = 16
SEL_LOCAL = 2
WINDOW = 512
ATTN_SCALE = HEAD_DIM ** -0.5
NEG_INF = -1e30
FORCE_SCORE = 1e9
REL_BUCKETS = 32
REL_MAX_DIST = 128
RWKV_HEADS = 8
RWKV_HEAD_DIM = 64
RWKV_WIDTH = RWKV_HEADS * RWKV_HEAD_DIM
DECAY_LORA = 64
ICLR_LORA = 64
GATE_LORA = 160
GATE_LORA_PAD = 256
GN_EPS = 64e-5
RMS_EPS = 1e-6

LANES = 128
Q_TILE = 128
K_TILE = 128
WKV_TILE = 256
WKV_CHUNK = 64
ROW_TILE = 512
VMEM_LIMIT_BYTES = 56 * 1024 * 1024

F32 = jnp.float32
BF16 = jnp.bfloat16


def _dot(a, b):
    return jnp.dot(a.astype(BF16), b.astype(BF16), preferred_element_type=F32)


def _dot_nt(a, b):
    return lax.dot_general(a.astype(BF16), b.astype(BF16), (((1,), (1,)), ((), ())),
                           preferred_element_type=F32)


def _dot_tn(a, b):
    return lax.dot_general(a.astype(BF16), b.astype(BF16), (((0,), (0,)), ((), ())),
                           preferred_element_type=F32)


def _split3(x):
    hi = x.astype(BF16)
    r1 = x - hi.astype(F32)
    mid = r1.astype(BF16)
    lo = (r1 - mid.astype(F32)).astype(BF16)
    return hi, mid, lo


def _dot_exact_rhs(x, m_bf16):
    hi, mid, lo = _split3(x)
    acc = jnp.dot(lo, m_bf16, preferred_element_type=F32)
    acc = acc + jnp.dot(mid, m_bf16, preferred_element_type=F32)
    return acc + jnp.dot(hi, m_bf16, preferred_element_type=F32)


def _dot_exact_lhs(m_bf16, x):
    hi, mid, lo = _split3(x)
    acc = jnp.dot(m_bf16, lo, preferred_element_type=F32)
    acc = acc + jnp.dot(m_bf16, mid, preferred_element_type=F32)
    return acc + jnp.dot(m_bf16, hi, preferred_element_type=F32)


def _rel_bucket_np(dist):
    max_exact = REL_BUCKETS // 2
    d = np.maximum(dist, 0)
    ratio = np.maximum(d, 1).astype(np.float32) / np.float32(max_exact)
    log_ratio = np.log(ratio).astype(np.float32) / np.float32(math.log(REL_MAX_DIST / max_exact))
    large = np.minimum(max_exact + (log_ratio * np.float32(REL_BUCKETS - max_exact)).astype(np.int32),
                       REL_BUCKETS - 1)
    return np.where(d < max_exact, d, large).astype(np.int32)


def _inproj_kernel(x_ref, gain_ref, w_ref, *out_refs, widths):
    x = x_ref[...]
    ms = jnp.mean(x * x, axis=-1, keepdims=True)
    xn = (x * lax.rsqrt(ms + RMS_EPS) * gain_ref[...]).astype(BF16)
    off = 0
    for o_ref, w in zip(out_refs, widths):
        o_ref[...] = jnp.dot(xn, w_ref[:, off:off + w], preferred_element_type=F32)
        off += w


def _in_projection(x2, gain, w_cat, widths):
    n, d = x2.shape
    tm = min(ROW_TILE, n)
    total = sum(widths)
    return pl.pallas_call(
        functools.partial(_inproj_kernel, widths=widths),
        grid=(n // tm,),
        in_specs=[
            pl.BlockSpec((tm, d), lambda i: (i, 0)),
            pl.BlockSpec((1, d), lambda i: (0, 0)),
            pl.BlockSpec((d, total), lambda i: (0, 0)),
        ],
        out_specs=[pl.BlockSpec((tm, w), lambda i: (i, 0)) for w in widths],
        out_shape=[jax.ShapeDtypeStruct((n, w), F32) for w in widths],
        compiler_params=pltpu.CompilerParams(
            dimension_semantics=("arbitrary",), vmem_limit_bytes=VMEM_LIMIT_BYTES),
        name="in_projection",
    )(x2, gain, w_cat)


def _head_rms(u, gain):
    ms = jnp.mean(u * u, axis=-1, keepdims=True)
    return u * lax.rsqrt(ms + RMS_EPS) * gain


def _flash_step(s, vt_blk, m, l, acc):
    m_new = jnp.maximum(m, jnp.max(s, axis=0, keepdims=True))
    alpha = jnp.exp(m - m_new)
    p = jnp.exp(s - m_new)
    l_new = alpha * l + jnp.sum(p, axis=0, keepdims=True)
    acc_new = alpha * acc + jnp.dot(vt_blk, p.astype(BF16), preferred_element_type=F32)
    return m_new, l_new, acc_new


def _nsa_kernel(tab_ref,
                q_ref, kv_ref, gate_ref,
                qg_ref, kg_ref, pe_ref, w1_ref, b1_ref, w2_ref, b2_ref, og_ref,
                bk0_ref, bk1_ref, bkc_ref, ovl_ref,
                out_ref,
                kvpad_ref, kcmp_ref, vcmp_ref, ksel_ref, vsel_ref, kwin_ref, vwin_ref,
                btab_ref, ccan_ref, nmask_ref, ybuf_ref,
                *, seq):
    b = pl.program_id(0)
    qt = pl.program_id(1)
    n_cb = seq // CMP_STRIDE
    n_c = (seq - CMP_BLOCK) // CMP_STRIDE + 1
    n_sel = seq // SEL_BLOCK
    cols = NSA_HPG * Q_TILE

    @pl.when((b == 0) & (qt == 0))
    def _build_bias_tables():
        bk0 = bk0_ref[...]
        bk1 = bk1_ref[...]
        bkc = bkc_ref[...]
        kk = lax.broadcasted_iota(jnp.int32, (K_TILE, Q_TILE), 0)
        qq = lax.broadcasted_iota(jnp.int32, (K_TILE, Q_TILE), 1)
        causal = jnp.where(kk <= qq, 0.0, NEG_INF).astype(F32)
        anti = jnp.where(kk > qq, 0.0, NEG_INF).astype(F32)
        for h in range(NSA_HEADS):
            t0 = jnp.zeros((K_TILE, Q_TILE), F32)
            t1 = jnp.zeros((K_TILE, Q_TILE), F32)
            tc = jnp.zeros((2 * LANES, Q_TILE), F32)
            for k in range(REL_BUCKETS):
                val = tab_ref[k, h]
                t0 = jnp.where(bk0 == k, val, t0)
                t1 = jnp.where(bk1 == k, val, t1)
                tc = jnp.where(bkc == k, val, tc)
            far = tab_ref[REL_BUCKETS - 1, h]
            g, p = divmod(h, NSA_HPG)
            cs = slice(p * Q_TILE, (p + 1) * Q_TILE)
            btab_ref[g, 0, :, cs] = jnp.zeros((K_TILE, Q_TILE), F32)
            btab_ref[g, 1, :, cs] = t1 - far
            btab_ref[g, 2, :, cs] = t0 - far + causal
            btab_ref[g, 3, :, cs] = anti
            btab_ref[g, 4, :, cs] = jnp.full((K_TILE, Q_TILE), NEG_INF, F32)
            ccan_ref[h] = tc
        for g in range(NSA_KV_HEADS):
            kwin_ref[g, 0:WINDOW, :] = jnp.zeros((WINDOW, HEAD_DIM), BF16)
            vwin_ref[g, :, 0:WINDOW] = jnp.zeros((HEAD_DIM, WINDOW), BF16)

    @pl.when(qt == 0)
    def _per_batch():
        for which in range(2):
            kvpad_ref[which, seq:seq + 2 * CMP_STRIDE, :] = jnp.zeros((2 * CMP_STRIDE, KV_WIDTH), F32)
            kvpad_ref[which, 0:seq, :] = kv_ref[0, :, which * KV_WIDTH:(which + 1) * KV_WIDTH]
            cmps = []
            for g in range(NSA_KV_HEADS):
                c0 = g * HEAD_DIM
                hid = jnp.zeros((n_cb, CMP_HIDDEN), F32)
                for l in range(CMP_BLOCK):
                    rows_l = kvpad_ref[which, pl.ds(l, n_cb, stride=CMP_STRIDE), :]
                    piece = rows_l[:, c0:c0 + HEAD_DIM] + pe_ref[which, l:l + 1, :]
                    hid = hid + _dot(piece, w1_ref[which, l])
                hid = jax.nn.gelu(hid + b1_ref[which], approximate=True)
                cmps.append(_dot(hid, w2_ref[which]) + b2_ref[which])
            if which == 0:
                for g in range(NSA_KV_HEADS):
                    kcmp_ref[g] = _head_rms(cmps[g], kg_ref[0:1, :]).astype(BF16)
            else:
                vct = jnp.concatenate(cmps, axis=1).T
                for g in range(NSA_KV_HEADS):
                    vcmp_ref[g] = vct[g * HEAD_DIM:(g + 1) * HEAD_DIM].astype(BF16)
        kv = kv_ref[0]
        for g in range(NSA_KV_HEADS):
            def col(i):
                c = (2 + i) * KV_WIDTH + g * HEAD_DIM
                return kv[:, c:c + HEAD_DIM]
            ksel_ref[g] = _head_rms(col(0), kg_ref[1:2, :]).astype(BF16)
            kwin_ref[g, WINDOW:WINDOW + seq, :] = _head_rms(col(2), kg_ref[2:3, :]).astype(BF16)
        for i, vt_ref, pad in ((1, vsel_ref, 0), (3, vwin_ref, WINDOW)):
            vt = kv[:, (2 + i) * KV_WIDTH:(3 + i) * KV_WIDTH].T
            for g in range(NSA_KV_HEADS):
                vt_ref[g, :, pad:pad + seq] = vt[g * HEAD_DIM:(g + 1) * HEAD_DIM].astype(BF16)

    t0 = qt * Q_TILE
    qT = q_ref[0].T
    gates = jax.nn.sigmoid(gate_ref[0].T)
    tq = t0 + lax.broadcasted_iota(jnp.int32, (1, Q_TILE), 1)
    ssq = jnp.zeros((1, Q_TILE), F32)
    n_selp = -(-n_sel // 8) * 8

    groups = range(NSA_KV_HEADS)
    q_cats, o_cmps = [], []
    for g in groups:
        qs = []
        for p in range(NSA_HPG):
            h = g * NSA_HPG + p
            qh = qT[h * HEAD_DIM:(h + 1) * HEAD_DIM]
            ms = jnp.mean(qh * qh, axis=0, keepdims=True)
            qs.append((qh * lax.rsqrt(ms + RMS_EPS) * qg_ref[...] * ATTN_SCALE).astype(BF16))
        q_cat = jnp.concatenate(qs, axis=1)
        q_cats.append(q_cat)

        s = jnp.dot(kcmp_ref[g], q_cat, preferred_element_type=F32)
        c_start = pl.multiple_of(LANES - (CMP_STRIDE // 2) * qt, 8)
        cbias = jnp.concatenate(
            [ccan_ref[g * NSA_HPG + p, pl.ds(c_start, n_cb), :] for p in range(NSA_HPG)], axis=1)
        cc = lax.broadcasted_iota(jnp.int32, (n_cb, Q_TILE), 0)
        valid = (cc * CMP_STRIDE + (CMP_BLOCK - 1) <= tq) & (cc < n_c)
        valid4 = jnp.concatenate([valid] * NSA_HPG, axis=1)
        s = jnp.where(valid4, s + cbias, NEG_INF)
        m = jnp.max(s, axis=0, keepdims=True)
        e = jnp.exp(s - m)
        pc = e / jnp.sum(e, axis=0, keepdims=True) * valid4.astype(F32)
        o_cmp = jnp.dot(vcmp_ref[g], pc.astype(BF16), preferred_element_type=F32)
        psum = pc[:, 0:Q_TILE]
        for p in range(1, NSA_HPG):
            psum = psum + pc[:, p * Q_TILE:(p + 1) * Q_TILE]

        imp = _dot_exact_lhs(ovl_ref[...], psum)[0:n_selp]
        jb = lax.broadcasted_iota(jnp.int32, (n_selp, Q_TILE), 0)
        qb = tq // SEL_BLOCK
        forced = (jb == 0) | ((jb <= qb) & (jb > qb - SEL_LOCAL))
        score = jnp.where(forced, FORCE_SCORE, imp)
        score = jnp.where(jb <= qb, score, -1.0)
        score = jnp.where(jb < n_sel, score, -2.0)
        rank = jnp.zeros((n_selp, Q_TILE), F32)
        for i in range(n_sel):
            si = score[i:i + 1, :]
            ahead = (si > score) | ((si == score) & (jb > i))
            rank = rank + ahead.astype(F32)
        sel = (rank < float(min(SEL_TOP_N, n_sel))) & (jb < n_sel)
        nmask_ref[g] = (sel.astype(F32) - 1.0) * (-NEG_INF)
        o_cmps.append(o_cmp)

    pair = 2 * K_TILE
    per_pair = pair // SEL_BLOCK

    def sel_body(kp, carry):
        k0 = pl.multiple_of(kp * pair, pair)
        rel = qt - 2 * kp
        kind_a = jnp.where(rel >= 2, 0, jnp.where(rel == 1, 1, 2))
        kind_b = jnp.where(rel >= 3, 0, jnp.where(rel == 2, 1, jnp.where(rel == 1, 2, 4)))
        out = []
        for g in groups:
            m, l, acc = carry[g]
            s = jnp.dot(ksel_ref[g, pl.ds(k0, pair), :], q_cats[g], preferred_element_type=F32)
            bias = jnp.concatenate([btab_ref[g, kind_a], btab_ref[g, kind_b]], axis=0)
            nm = jnp.concatenate(
                [jnp.broadcast_to(nmask_ref[g, pl.ds(kp * per_pair + j, 1), :], (SEL_BLOCK, Q_TILE))
                 for j in range(per_pair)], axis=0)
            s = s + bias + jnp.concatenate([nm] * NSA_HPG, axis=1)
            out.append(_flash_step(s, vsel_ref[g, :, pl.ds(k0, pair)], m, l, acc))
        return tuple(out)

    init = (jnp.full((1, cols), NEG_INF, F32), jnp.zeros((1, cols), F32),
            jnp.zeros((HEAD_DIM, cols), F32))
    sel_out = lax.fori_loop(0, qt // 2 + 1, sel_body, tuple(init for _ in groups))

    n_wb = WINDOW // K_TILE
    band = WINDOW + K_TILE
    band_kinds = (3,) + (0,) * (n_wb - 2) + (1, 2)
    w0 = pl.multiple_of(qt * K_TILE, K_TILE)
    for g in groups:
        s = jnp.dot(kwin_ref[g, pl.ds(w0, band), :], q_cats[g], preferred_element_type=F32)
        blocks = []
        for j, kind in enumerate(band_kinds):
            sj = s[j * K_TILE:(j + 1) * K_TILE]
            if kind != 0:
                sj = sj + btab_ref[g, kind]
            if j < n_wb:
                sj = sj + jnp.where(qt < n_wb - j, NEG_INF, 0.0)
            blocks.append(sj)
        s = jnp.concatenate(blocks, axis=0)
        m = jnp.max(s, axis=0, keepdims=True)
        pw = jnp.exp(s - m)
        l_w = jnp.sum(pw, axis=0, keepdims=True)
        o_win = jnp.dot(vwin_ref[g, :, pl.ds(w0, band)], pw.astype(BF16), preferred_element_type=F32) / l_w
        _, l_s, acc_s = sel_out[g]
        o_sel = acc_s / l_s
        o_cmp = o_cmps[g]

        for p in range(NSA_HPG):
            h = g * NSA_HPG + p
            cs = slice(p * Q_TILE, (p + 1) * Q_TILE)
            o = (gates[h:h + 1] * o_cmp[:, cs]
                 + gates[NSA_HEADS + h:NSA_HEADS + h + 1] * o_sel[:, cs]
                 + gates[2 * NSA_HEADS + h:2 * NSA_HEADS + h + 1] * o_win[:, cs])
            ssq = ssq + jnp.sum(o * o, axis=0, keepdims=True)
            ybuf_ref[h * HEAD_DIM:(h + 1) * HEAD_DIM, :] = o

    yT = ybuf_ref[...] * lax.rsqrt(ssq / NSA_WIDTH + RMS_EPS) * og_ref[...]
    out_ref[0] = yT.T.astype(out_ref.dtype)


def _nsa_mixer(q, kv, gates, q_gain, k_gain, cmp_pe, cmp_w1, cmp_b1, cmp_w2, cmp_b2, out_gain, rel_table):
    bsz, seq, _ = q.shape
    assert seq % (2 * K_TILE) == 0 and seq // CMP_STRIDE == LANES and seq // SEL_BLOCK <= LANES
    n_qt = seq // Q_TILE
    n_cb = seq // CMP_STRIDE
    n_sel = seq // SEL_BLOCK
    n_selp = -(-n_sel // 8) * 8
    cols = NSA_HPG * Q_TILE

    kk = np.arange(K_TILE)[:, None]
    qq = np.arange(Q_TILE)[None, :]
    bk0 = _rel_bucket_np(qq - kk)
    bk1 = _rel_bucket_np(K_TILE + qq - kk)
    cprime = np.arange(2 * LANES)[:, None] - LANES
    bkc = _rel_bucket_np(qq - (CMP_BLOCK - 1) - CMP_STRIDE * cprime)
    c_start = np.arange(n_cb)[None, :] * CMP_STRIDE
    j_start = np.arange(LANES)[:, None] * SEL_BLOCK
    ovl = np.clip(np.minimum(c_start + CMP_BLOCK, j_start + SEL_BLOCK) - np.maximum(c_start, j_start), 0, None)
    ovl = (ovl.astype(np.float32) / CMP_STRIDE)
    ovl[n_sel:, :] = 0.0

    w1 = cmp_w1.reshape(2, CMP_BLOCK, HEAD_DIM, CMP_HIDDEN).astype(BF16)
    full = lambda shape: pl.BlockSpec(shape, lambda b, t: (0,) * len(shape))
    kernel = functools.partial(_nsa_kernel, seq=seq)
    return pl.pallas_call(
        kernel,
        grid=(bsz, n_qt),
        in_specs=[
            pl.BlockSpec(memory_space=pltpu.SMEM),
            pl.BlockSpec((1, Q_TILE, NSA_WIDTH), lambda b, t: (b, t, 0)),
            pl.BlockSpec((1, seq, 6 * KV_WIDTH), lambda b, t: (b, 0, 0)),
            pl.BlockSpec((1, Q_TILE, LANES), lambda b, t: (b, t, 0)),
            full((HEAD_DIM, Q_TILE)),
            full((3, HEAD_DIM)),
            full((2, CMP_BLOCK, HEAD_DIM)),
            full((2, CMP_BLOCK, HEAD_DIM, CMP_HIDDEN)),
            full((2, 1, CMP_HIDDEN)),
            full((2, CMP_HIDDEN, HEAD_DIM)),
            full((2, 1, HEAD_DIM)),
            full((NSA_WIDTH, Q_TILE)),
            full((K_TILE, Q_TILE)),
            full((K_TILE, Q_TILE)),
            full((2 * LANES, Q_TILE)),
            full((LANES, n_cb)),
        ],
        out_specs=pl.BlockSpec((1, Q_TILE, NSA_WIDTH), lambda b, t: (b, t, 0)),
        out_shape=jax.ShapeDtypeStruct((bsz, seq, NSA_WIDTH), BF16),
        scratch_shapes=[
            pltpu.VMEM((2, seq + 2 * CMP_STRIDE, KV_WIDTH), F32),
            pltpu.VMEM((NSA_KV_HEADS, n_cb, HEAD_DIM), BF16),
            pltpu.VMEM((NSA_KV_HEADS, HEAD_DIM, n_cb), BF16),
            pltpu.VMEM((NSA_KV_HEADS, seq, HEAD_DIM), BF16),
            pltpu.VMEM((NSA_KV_HEADS, HEAD_DIM, seq), BF16),
            pltpu.VMEM((NSA_KV_HEADS, WINDOW + seq, HEAD_DIM), BF16),
            pltpu.VMEM((NSA_KV_HEADS, HEAD_DIM, WINDOW + seq), BF16),
            pltpu.VMEM((NSA_KV_HEADS, 5, K_TILE, cols), F32),
            pltpu.VMEM((NSA_HEADS, 2 * LANES, Q_TILE), F32),
            pltpu.VMEM((NSA_KV_HEADS, n_selp, Q_TILE), F32),
            pltpu.VMEM((NSA_WIDTH, Q_TILE), F32),
        ],
        compiler_params=pltpu.CompilerParams(
            dimension_semantics=("arbitrary", "arbitrary"), vmem_limit_bytes=VMEM_LIMIT_BYTES),
        name="nsa_mixer",
    )(rel_table, q, kv, gates, jnp.broadcast_to(q_gain[:, None], (HEAD_DIM, Q_TILE)), k_gain, cmp_pe, w1,
      cmp_b1.reshape(2, 1, CMP_HIDDEN), cmp_w2.astype(BF16), cmp_b2.reshape(2, 1, HEAD_DIM),
      jnp.broadcast_to(out_gain[:, None], (NSA_WIDTH, Q_TILE)), jnp.asarray(bk0), jnp.asarray(bk1),
      jnp.asarray(bkc), jnp.asarray(ovl, BF16))


def _rwkv_kernel(zr_ref, zl_ref, zg_ref, mr_ref, ml_ref, mg_ref, w0_ref, w2_ref, a0_ref, a2_ref, g2_ref,
                 kk_ref, ka_ref, rk_ref, lng_ref, lnb_ref, tri_ref,
                 out_ref,
                 state_ref, cr_ref, cl_ref, cg_ref):
    t = pl.program_id(1)
    tt = WKV_TILE
    n = RWKV_HEAD_DIM
    nck = tt // WKV_CHUNK

    @pl.when(t == 0)
    def _reset():
        state_ref[...] = jnp.zeros_like(state_ref)
        cr_ref[...] = jnp.zeros_like(cr_ref)
        cl_ref[...] = jnp.zeros_like(cl_ref)
        cg_ref[...] = jnp.zeros_like(cg_ref)

    row = lax.broadcasted_iota(jnp.int32, (tt, 1), 0)

    def shifted(z_ref, mix_ref, carry_ref):
        z = z_ref[0]
        prev = jnp.where(row == 0, carry_ref[...], pltpu.roll(z, 1, 0))
        carry_ref[...] = z_ref[0, tt - 1:tt, :]
        return z + mix_ref[...] * (prev - z)

    zr = shifted(zr_ref, mr_ref, cr_ref)
    zl = shifted(zl_ref, ml_ref, cl_ref)
    zg = shifted(zg_ref, mg_ref, cg_ref)
    r = zr[:, 0:RWKV_WIDTH]
    k = zr[:, RWKV_WIDTH:2 * RWKV_WIDTH]
    v = zr[:, 2 * RWKV_WIDTH:3 * RWKV_WIDTH]

    xw = w0_ref[...] + _dot(jnp.tanh(zl[:, 0:DECAY_LORA]), w2_ref[...])
    neg = -xw
    softplus = jnp.maximum(neg, 0.0) + jnp.log(1.0 + jnp.exp(-jnp.abs(neg)))
    lw = -jnp.exp(-softplus - 0.5)
    a = jax.nn.sigmoid(a0_ref[...] + _dot(zl[:, DECAY_LORA:DECAY_LORA + ICLR_LORA], a2_ref[...]))
    gate = _dot(jax.nn.sigmoid(zg), g2_ref[...])

    cum = _dot_exact_lhs(tri_ref[...], lw)
    cum_end = cum.reshape(nck, WKV_CHUNK, RWKV_WIDTH)[:, WKV_CHUNK - 1:WKV_CHUNK, :]
    cum_end_b = jnp.broadcast_to(cum_end, (nck, WKV_CHUNK, RWKV_WIDTH)).reshape(tt, RWKV_WIDTH)
    e_incl = jnp.exp(cum)
    e_excl = jnp.exp(cum - lw)
    e_neg = jnp.exp(-cum)
    e_rev = jnp.exp(cum_end_b - cum)
    g_end = jnp.exp(cum_end)

    ii = lax.broadcasted_iota(jnp.int32, (tt, tt), 0)
    jj = lax.broadcasted_iota(jnp.int32, (tt, tt), 1)
    same = (ii // WKV_CHUNK) == (jj // WKV_CHUNK)
    strict = same & (ii > jj)
    incl = same & (ii >= jj)
    eye_t = (ii == jj).astype(F32)
    i64 = lax.broadcasted_iota(jnp.int32, (n, n), 0)
    j64 = lax.broadcasted_iota(jnp.int32, (n, n), 1)
    eye_n = (i64 == j64).astype(F32)

    for h in range(RWKV_HEADS):
        sl = slice(h * n, (h + 1) * n)
        rh, kh, vh, ah = r[:, sl], k[:, sl], v[:, sl], a[:, sl]
        kkh = kh * kk_ref[:, sl]
        nrm = jnp.sqrt(jnp.sum(kkh * kkh, axis=-1, keepdims=True))
        kkn = kkh / jnp.maximum(nrm, 1e-12)
        kmod = kh * (1.0 + (ah - 1.0) * ka_ref[:, sl])
        av = -kkn
        bv = kkn * ah
        a_t = av * e_excl[:, sl]
        r_t = rh * e_incl[:, sl]
        ar = jnp.concatenate([a_t, r_t], axis=0).astype(BF16)
        bk = jnp.concatenate([bv * e_neg[:, sl], kmod * e_neg[:, sl]], axis=0).astype(BF16)
        mm = _dot_nt(ar, bk)
        l_ab = jnp.where(strict, mm[0:tt, 0:tt], 0.0)
        l_ak = jnp.where(strict, mm[0:tt, tt:2 * tt], 0.0)
        a_rb = jnp.where(incl, mm[tt:2 * tt, 0:tt], 0.0)
        a_rk = jnp.where(incl, mm[tt:2 * tt, tt:2 * tt], 0.0)
        x = l_ab
        tinv = eye_t + l_ab
        for _ in range(int(math.log2(WKV_CHUNK)) - 1):
            x = _dot(x, x)
            tinv = tinv + _dot(tinv, x)
        vb = vh.astype(BF16)
        lv = _dot(jnp.concatenate([l_ak, a_rk], axis=0), vb)
        wu = _dot(tinv, jnp.concatenate([a_t, lv[0:tt]], axis=1))
        qy = _dot(a_rb, wu)
        qp = r_t + qy[:, 0:n]
        y0 = qy[:, n:2 * n] + lv[tt:2 * tt]
        b_end = bv * e_rev[:, sl]
        k_end = kmod * e_rev[:, sl]
        hstate = state_ref[h]
        ys = []
        for c in range(nck):
            cs = slice(c * WKV_CHUNK, (c + 1) * WKV_CHUNK)
            lhs = jnp.concatenate([b_end[cs], k_end[cs]], axis=0)
            rhs = jnp.concatenate(
                [wu[cs], jnp.concatenate([jnp.zeros((WKV_CHUNK, n), F32), vh[cs]], axis=1)], axis=0)
            ph = _dot_tn(lhs, rhs)
            pmat = eye_n * g_end[c, :, sl] + ph[:, 0:n]
            qph = _dot(jnp.concatenate([qp[cs], pmat], axis=0), hstate)
            ys.append(qph[0:WKV_CHUNK] + y0[cs])
            hstate = qph[WKV_CHUNK:WKV_CHUNK + n] + ph[:, n:2 * n]
        state_ref[h] = hstate
        y = jnp.concatenate(ys, axis=0)
        mu = jnp.mean(y, axis=-1, keepdims=True)
        var = jnp.mean(jnp.square(y - mu), axis=-1, keepdims=True)
        yn = (y - mu) * lax.rsqrt(var + GN_EPS) * lng_ref[:, sl] + lnb_ref[:, sl]
        bonus = jnp.sum(rh * kmod * rk_ref[:, sl], axis=-1, keepdims=True) * vh
        out_ref[0, :, sl] = ((yn + bonus) * gate[:, sl]).astype(out_ref.dtype)


def _rwkv_mixer(zr, zl, zg, mix_r, mix_l, mix_g, w0, w2, a0, a2, g2p, k_k, k_a, r_k, ln_g, ln_b):
    bsz, seq, _ = zr.shape
    tt = WKV_TILE
    assert seq % tt == 0
    idx = np.arange(tt)
    tri = ((idx[:, None] // WKV_CHUNK == idx[None, :] // WKV_CHUNK) & (idx[:, None] >= idx[None, :]))
    full = lambda shape: pl.BlockSpec(shape, lambda b, t: (0,) * len(shape))
    row = lambda a: a.reshape(1, -1)
    wl = DECAY_LORA + ICLR_LORA
    return pl.pallas_call(
        _rwkv_kernel,
        grid=(bsz, seq // tt),
        in_specs=[
            pl.BlockSpec((1, tt, 3 * RWKV_WIDTH), lambda b, t: (b, t, 0)),
            pl.BlockSpec((1, tt, wl), lambda b, t: (b, t, 0)),
            pl.BlockSpec((1, tt, GATE_LORA_PAD), lambda b, t: (b, t, 0)),
            full((1, 3 * RWKV_WIDTH)), full((1, wl)), full((1, GATE_LORA_PAD)),
            full((1, RWKV_WIDTH)), full((DECAY_LORA, RWKV_WIDTH)),
            full((1, RWKV_WIDTH)), full((ICLR_LORA, RWKV_WIDTH)),
            full((GATE_LORA_PAD, RWKV_WIDTH)),
            full((1, RWKV_WIDTH)), full((1, RWKV_WIDTH)), full((1, RWKV_WIDTH)),
            full((1, RWKV_WIDTH)), full((1, RWKV_WIDTH)),
            full((tt, tt)),
        ],
        out_specs=pl.BlockSpec((1, tt, RWKV_WIDTH), lambda b, t: (b, t, 0)),
        out_shape=jax.ShapeDtypeStruct((bsz, seq, RWKV_WIDTH), BF16),
        scratch_shapes=[
            pltpu.VMEM((RWKV_HEADS, RWKV_HEAD_DIM, RWKV_HEAD_DIM), F32),
            pltpu.VMEM((1, 3 * RWKV_WIDTH), F32),
            pltpu.VMEM((1, wl), F32),
            pltpu.VMEM((1, GATE_LORA_PAD), F32),
        ],
        compiler_params=pltpu.CompilerParams(
            dimension_semantics=("arbitrary", "arbitrary"), vmem_limit_bytes=VMEM_LIMIT_BYTES),
        name="rwkv7_mixer",
    )(zr, zl, zg, row(mix_r), row(mix_l), row(mix_g), row(w0), w2.astype(BF16), row(a0), a2.astype(BF16),
      g2p.astype(BF16), row(k_k), row(k_a), row(r_k), row(ln_g), row(ln_b), jnp.asarray(tri, BF16))


def _ffn_kernel(x_ref, yn_ref, yr_ref, won_ref, wor_ref, gain_ref, wg_ref, wu_ref, wd_ref, o_ref, *, ff_chunk):
    h = x_ref[...]
    h = h + jnp.dot(yn_ref[...], won_ref[...], preferred_element_type=F32)
    h = h + jnp.dot(yr_ref[...], wor_ref[...], preferred_element_type=F32)
    ms = jnp.mean(h * h, axis=-1, keepdims=True)
    hn = (h * lax.rsqrt(ms + RMS_EPS) * gain_ref[...]).astype(BF16)
    d_ff = wg_ref.shape[1]
    acc = None
    for c in range(d_ff // ff_chunk):
        cs = slice(c * ff_chunk, (c + 1) * ff_chunk)
        gte = jnp.dot(hn, wg_ref[:, cs], preferred_element_type=F32)
        up = jnp.dot(hn, wu_ref[:, cs], preferred_element_type=F32)
        act = (gte * jax.nn.sigmoid(gte) * up).astype(BF16)
        down = jnp.dot(act, wd_ref[cs, :], preferred_element_type=F32)
        acc = down if acc is None else acc + down
    o_ref[...] = h + acc


def _out_ffn(x2, y_nsa, y_rwkv, w_out, ffn_gain, w_gate, w_up, w_down):
    n, d = x2.shape
    d_ff = w_gate.shape[1]
    tm = min(ROW_TILE // 2, n)
    ff_chunk = d_ff // 2 if (d_ff // 2) % LANES == 0 else d_ff
    const = lambda shape: pl.BlockSpec(shape, lambda i: (0, 0))
    return pl.pallas_call(
        functools.partial(_ffn_kernel, ff_chunk=ff_chunk),
        grid=(n // tm,),
        in_specs=[
            pl.BlockSpec((tm, d), lambda i: (i, 0)),
            pl.BlockSpec((tm, NSA_WIDTH), lambda i: (i, 0)),
            pl.BlockSpec((tm, RWKV_WIDTH), lambda i: (i, 0)),
            const((NSA_WIDTH, d)), const((RWKV_WIDTH, d)), const((1, d)),
            const((d, d_ff)), const((d, d_ff)), const((d_ff, d)),
        ],
        out_specs=pl.BlockSpec((tm, d), lambda i: (i, 0)),
        out_shape=jax.ShapeDtypeStruct((n, d), F32),
        compiler_params=pltpu.CompilerParams(
            dimension_semantics=("arbitrary",), vmem_limit_bytes=VMEM_LIMIT_BYTES),
        name="out_proj_ffn",
    )(x2, y_nsa, y_rwkv, w_out[:NSA_WIDTH].astype(BF16), w_out[NSA_WIDTH:].astype(BF16),
      ffn_gain.reshape(1, d), w_gate.astype(BF16), w_up.astype(BF16), w_down.astype(BF16))


def _layer(h, attn_gain, w_in, q_gain, k_gain, cmp_pe, cmp_w1, cmp_b1, cmp_w2, cmp_b2, out_gain, shift_mix,
           w0, w2, a0, a2, g2, k_k, k_a, r_k, ln_g, ln_b, w_out, ffn_gain, w_gate, w_up, w_down, rel_table):
    bsz, seq, d = h.shape
    nsa_cols = NSA_WIDTH + 6 * KV_WIDTH + 3 * NSA_HEADS
    gate0 = NSA_WIDTH + 6 * KV_WIDTH

    w_nsa, w_rw = w_in[:, :nsa_cols], w_in[:, nsa_cols:]
    gate_src = np.array([gate0 + h_ * 3 + br for br in range(3) for h_ in range(NSA_HEADS)])
    w_gates = jnp.zeros((d, LANES), F32).at[:, :3 * NSA_HEADS].set(w_nsa[:, gate_src])
    lora0 = 3 * RWKV_WIDTH
    g0 = lora0 + DECAY_LORA + ICLR_LORA
    w_g = jnp.zeros((d, GATE_LORA_PAD), F32).at[:, :GATE_LORA].set(w_rw[:, g0:g0 + GATE_LORA])
    widths = (NSA_WIDTH, 6 * KV_WIDTH, LANES, 3 * RWKV_WIDTH, DECAY_LORA + ICLR_LORA, GATE_LORA_PAD)
    w_cat = jnp.concatenate([w_nsa[:, :gate0], w_gates, w_rw[:, :g0], w_g], axis=1).astype(BF16)
    mix_g = jnp.zeros((GATE_LORA_PAD,), F32).at[:GATE_LORA].set(shift_mix[g0:g0 + GATE_LORA])
    g2p = jnp.zeros((GATE_LORA_PAD, RWKV_WIDTH), F32).at[:GATE_LORA].set(g2)

    x2 = h.reshape(bsz * seq, d)
    q, kv, gates, zr, zl, zg = _in_projection(x2, attn_gain.reshape(1, d), w_cat, widths)
    r3 = lambda u: u.reshape(bsz, seq, u.shape[-1])
    y_nsa = _nsa_mixer(r3(q), r3(kv), r3(gates), q_gain, k_gain, cmp_pe, cmp_w1, cmp_b1, cmp_w2, cmp_b2,
                       out_gain, rel_table)
    y_rwkv = _rwkv_mixer(r3(zr), r3(zl), r3(zg), shift_mix[:lora0], shift_mix[lora0:g0], mix_g,
                         w0, w2, a0, a2, g2p, k_k, k_a, r_k.reshape(-1), ln_g, ln_b)
    out = _out_ffn(x2, y_nsa.reshape(bsz * seq, NSA_WIDTH), y_rwkv.reshape(bsz * seq, RWKV_WIDTH),
                   w_out, ffn_gain, w_gate, w_up, w_down)
    return out.reshape(bsz, seq, d)


def kernel(x, attn_norm_gain, w_in, nsa_q_gain, nsa_k_gain, cmp_pe, cmp_w1, cmp_b1, cmp_w2, cmp_b2, nsa_out_gain, rwkv_shift_mix, rwkv_w0, rwkv_w2, rwkv_a0, rwkv_a2, rwkv_g2, rwkv_k_k, rwkv_k_a, rwkv_r_k, rwkv_ln_gain, rwkv_ln_bias, w_out, ffn_norm_gain, w_gate, w_up, w_down, rel_bias_table):
    h = x
    for l in range(attn_norm_gain.shape[0]):
        h = _layer(h, attn_norm_gain[l], w_in[l], nsa_q_gain[l], nsa_k_gain[l], cmp_pe[l], cmp_w1[l], cmp_b1[l],
                   cmp_w2[l], cmp_b2[l], nsa_out_gain[l], rwkv_shift_mix[l], rwkv_w0[l], rwkv_w2[l], rwkv_a0[l],
                   rwkv_a2[l], rwkv_g2[l], rwkv_k_k[l], rwkv_k_a[l], rwkv_r_k[l], rwkv_ln_gain[l],
                   rwkv_ln_bias[l], w_out[l], ffn_norm_gain[l], w_gate[l], w_up[l], w_down[l], rel_bias_table)
    return h
```

```python
import functools
import math

import numpy as np
import jax
import jax.numpy as jnp
from jax import lax
from jax.experimental import pallas as pl
from jax.experimental.pallas import tpu as pltpu

HEAD_DIM = 64
NSA_HEADS = 8
NSA_KV_HEADS = 2
NSA_HPG = NSA_HEADS // NSA_KV_HEADS
NSA_WIDTH = NSA_HEADS * HEAD_DIM
KV_WIDTH = NSA_KV_HEADS * HEAD_DIM
CMP_BLOCK = 32
CMP_STRIDE = 16
CMP_HIDDEN = 256
SEL_BLOCK = 64
SEL_TOP_N = 16
SEL_LOCAL = 2
WINDOW = 512
ATTN_SCALE = HEAD_DIM ** -0.5
NEG_INF = -1e30
FORCE_SCORE = 1e9
REL_BUCKETS = 32
REL_MAX_DIST = 128
RWKV_HEADS = 8
RWKV_HEAD_DIM = 64
RWKV_WIDTH = RWKV_HEADS * RWKV_HEAD_DIM
DECAY_LORA = 64
ICLR_LORA = 64
GATE_LORA = 160
GATE_LORA_PAD = 256
GN_EPS = 64e-5
RMS_EPS = 1e-6

LANES = 128
Q_TILE = 128
K_TILE = 128
WKV_TILE = 256
WKV_CHUNK = 64
ROW_TILE = 512
VMEM_LIMIT_BYTES = 56 * 1024 * 1024

F32 = jnp.float32
BF16 = jnp.bfloat16


def _dot(a, b):
    return jnp.dot(a.astype(BF16), b.astype(BF16), preferred_element_type=F32)


def _dot_nt(a, b):
    return lax.dot_general(a.astype(BF16), b.astype(BF16), (((1,), (1,)), ((), ())),
                           preferred_element_type=F32)


def _dot_tn(a, b):
    return lax.dot_general(a.astype(BF16), b.astype(BF16), (((0,), (0,)), ((), ())),
                           preferred_element_type=F32)


def _split3(x):
    hi = x.astype(BF16)
    r1 = x - hi.astype(F32)
    mid = r1.astype(BF16)
    lo = (r1 - mid.astype(F32)).astype(BF16)
    return hi, mid, lo


def _dot_exact_rhs(x, m_bf16):
    hi, mid, lo = _split3(x)
    acc = jnp.dot(lo, m_bf16, preferred_element_type=F32)
    acc = acc + jnp.dot(mid, m_bf16, preferred_element_type=F32)
    return acc + jnp.dot(hi, m_bf16, preferred_element_type=F32)


def _dot_exact_lhs(m_bf16, x):
    hi, mid, lo = _split3(x)
    acc = jnp.dot(m_bf16, lo, preferred_element_type=F32)
    acc = acc + jnp.dot(m_bf16, mid, preferred_element_type=F32)
    return acc + jnp.dot(m_bf16, hi, preferred_element_type=F32)


def _rel_bucket_np(dist):
    max_exact = REL_BUCKETS // 2
    d = np.maximum(dist, 0)
    ratio = np.maximum(d, 1).astype(np.float32) / np.float32(max_exact)
    log_ratio = np.log(ratio).astype(np.float32) / np.float32(math.log(REL_MAX_DIST / max_exact))
    large = np.minimum(max_exact + (log_ratio * np.float32(REL_BUCKETS - max_exact)).astype(np.int32),
                       REL_BUCKETS - 1)
    return np.where(d < max_exact, d, large).astype(np.int32)


def _inproj_kernel(x_ref, gain_ref, w_ref, *out_refs, widths):
    x = x_ref[...]
    ms = jnp.mean(x * x, axis=-1, keepdims=True)
    xn = (x * lax.rsqrt(ms + RMS_EPS) * gain_ref[...]).astype(BF16)
    off = 0
    for o_ref, w in zip(out_refs, widths):
        o_ref[...] = jnp.dot(xn, w_ref[:, off:off + w], preferred_element_type=F32)
        off += w


def _in_projection(x2, gain, w_cat, widths):
    n, d = x2.shape
    tm = min(ROW_TILE, n)
    total = sum(widths)
    return pl.pallas_call(
        functools.partial(_inproj_kernel, widths=widths),
        grid=(n // tm,),
        in_specs=[
            pl.BlockSpec((tm, d), lambda i: (i, 0)),
            pl.BlockSpec((1, d), lambda i: (0, 0)),
            pl.BlockSpec((d, total), lambda i: (0, 0)),
        ],
        out_specs=[pl.BlockSpec((tm, w), lambda i: (i, 0)) for w in widths],
        out_shape=[jax.ShapeDtypeStruct((n, w), F32) for w in widths],
        compiler_params=pltpu.CompilerParams(
            dimension_semantics=("arbitrary",), vmem_limit_bytes=VMEM_LIMIT_BYTES),
        name="in_projection",
    )(x2, gain, w_cat)


def _head_rms(u, gain):
    ms = jnp.mean(u * u, axis=-1, keepdims=True)
    return u * lax.rsqrt(ms + RMS_EPS) * gain


def _flash_step(s, vt_blk, m, l, acc):
    m_new = jnp.maximum(m, jnp.max(s, axis=0, keepdims=True))
    alpha = jnp.exp(m - m_new)
    p = jnp.exp(s - m_new)
    l_new = alpha * l + jnp.sum(p, axis=0, keepdims=True)
    acc_new = alpha * acc + jnp.dot(vt_blk, p.astype(BF16), preferred_element_type=F32)
    return m_new, l_new, acc_new


def _nsa_kernel(tab_ref,
                q_ref, kv_ref, gate_ref,
                qg_ref, kg_ref, pe_ref, w1_ref, b1_ref, w2_ref, b2_ref, og_ref,
                bk0_ref, bk1_ref, bkc_ref, ovl_ref,
                out_ref,
                kvpad_ref, kcmp_ref, vcmp_ref, ksel_ref, vsel_ref, kwin_ref, vwin_ref,
                btab_ref, ccan_ref, nmask_ref, ybuf_ref,
                *, seq):
    b = pl.program_id(0)
    qt = pl.program_id(1)
    n_cb = seq // CMP_STRIDE
    n_c = (seq - CMP_BLOCK) // CMP_STRIDE + 1
    n_sel = seq // SEL_BLOCK
    cols = NSA_HPG * Q_TILE

    @pl.when((b == 0) & (qt == 0))
    def _build_bias_tables():
        bk0 = bk0_ref[...]
        bk1 = bk1_ref[...]
        bkc = bkc_ref[...]
        kk = lax.broadcasted_iota(jnp.int32, (K_TILE, Q_TILE), 0)
        qq = lax.broadcasted_iota(jnp.int32, (K_TILE, Q_TILE), 1)
        causal = jnp.where(kk <= qq, 0.0, NEG_INF).astype(F32)
        anti = jnp.where(kk > qq, 0.0, NEG_INF).astype(F32)
        for h in range(NSA_HEADS):
            t0 = jnp.zeros((K_TILE, Q_TILE), F32)
            t1 = jnp.zeros((K_TILE, Q_TILE), F32)
            tc = jnp.zeros((2 * LANES, Q_TILE), F32)
            for k in range(REL_BUCKETS):
                val = tab_ref[k, h]
                t0 = jnp.where(bk0 == k, val, t0)
                t1 = jnp.where(bk1 == k, val, t1)
                tc = jnp.where(bkc == k, val, tc)
            far = tab_ref[REL_BUCKETS - 1, h]
            g, p = divmod(h, NSA_HPG)
            cs = slice(p * Q_TILE, (p + 1) * Q_TILE)
            btab_ref[g, 0, :, cs] = jnp.zeros((K_TILE, Q_TILE), F32)
            btab_ref[g, 1, :, cs] = t1 - far
            btab_ref[g, 2, :, cs] = t0 - far + causal
            btab_ref[g, 3, :, cs] = anti
            btab_ref[g, 4, :, cs] = jnp.full((K_TILE, Q_TILE), NEG_INF, F32)
            ccan_ref[h] = tc
        for g in range(NSA_KV_HEADS):
            kwin_ref[g, 0:WINDOW, :] = jnp.zeros((WINDOW, HEAD_DIM), BF16)
            vwin_ref[g, :, 0:WINDOW] = jnp.zeros((HEAD_DIM, WINDOW), BF16)

    @pl.when(qt == 0)
    def _per_batch():
        for which in range(2):
            kvpad_ref[which, seq:seq + 2 * CMP_STRIDE, :] = jnp.zeros((2 * CMP_STRIDE, KV_WIDTH), F32)
            kvpad_ref[which, 0:seq, :] = kv_ref[0, :, which * KV_WIDTH:(which + 1) * KV_WIDTH]
            cmps = []
            for g in range(NSA_KV_HEADS):
                c0 = g * HEAD_DIM
                hid = jnp.zeros((n_cb, CMP_HIDDEN), F32)
                for l in range(CMP_BLOCK):
                    rows_l = kvpad_ref[which, pl.ds(l, n_cb, stride=CMP_STRIDE), :]
                    piece = rows_l[:, c0:c0 + HEAD_DIM] + pe_ref[which, l:l + 1, :]
                    hid = hid + _dot(piece, w1_ref[which, l])
                hid = jax.nn.gelu(hid + b1_ref[which], approximate=True)
                cmps.append(_dot(hid, w2_ref[which]) + b2_ref[which])
            if which == 0:
                for g in range(NSA_KV_HEADS):
                    kcmp_ref[g] = _head_rms(cmps[g], kg_ref[0:1, :]).astype(BF16)
            else:
                vct = jnp.concatenate(cmps, axis=1).T
                for g in range(NSA_KV_HEADS):
                    vcmp_ref[g] = vct[g * HEAD_DIM:(g + 1) * HEAD_DIM].astype(BF16)
        kv = kv_ref[0]
        for g in range(NSA_KV_HEADS):
            def col(i):
                c = (2 + i) * KV_WIDTH + g * HEAD_DIM
                return kv[:, c:c + HEAD_DIM]
            ksel_ref[g] = _head_rms(col(0), kg_ref[1:2, :]).astype(BF16)
            kwin_ref[g, WINDOW:WINDOW + seq, :] = _head_rms(col(2), kg_ref[2:3, :]).astype(BF16)
        for i, vt_ref, pad in ((1, vsel_ref, 0), (3, vwin_ref, WINDOW)):
            vt = kv[:, (2 + i) * KV_WIDTH:(3 + i) * KV_WIDTH].T
            for g in range(NSA_KV_HEADS):
                vt_ref[g, :, pad:pad + seq] = vt[g * HEAD_DIM:(g + 1) * HEAD_DIM].astype(BF16)

    t0 = qt * Q_TILE
    qT = q_ref[0].T
    gates = jax.nn.sigmoid(gate_ref[0].T)
    tq = t0 + lax.broadcasted_iota(jnp.int32, (1, Q_TILE), 1)
    ssq = jnp.zeros((1, Q_TILE), F32)
    n_selp = -(-n_sel // 8) * 8

    groups = range(NSA_KV_HEADS)
    both = lambda f: [f(g) for g in groups]

    def q_group(g):
        qs = []
        for p in range(NSA_HPG):
            h = g * NSA_HPG + p
            qh = qT[h * HEAD_DIM:(h + 1) * HEAD_DIM]
            ms = jnp.mean(qh * qh, axis=0, keepdims=True)
            qs.append((qh * lax.rsqrt(ms + RMS_EPS) * qg_ref[...] * ATTN_SCALE).astype(BF16))
        return jnp.concatenate(qs, axis=1)

    q_cats = both(q_group)

    c_start = pl.multiple_of(LANES - (CMP_STRIDE // 2) * qt, 8)
    cc = lax.broadcasted_iota(jnp.int32, (n_cb, Q_TILE), 0)
    valid = (cc * CMP_STRIDE + (CMP_BLOCK - 1) <= tq) & (cc < n_c)
    valid4 = jnp.concatenate([valid] * NSA_HPG, axis=1)
    ss = both(lambda g: jnp.dot(kcmp_ref[g], q_cats[g], preferred_element_type=F32))
    ss = both(lambda g: jnp.where(valid4, ss[g] + jnp.concatenate(
        [ccan_ref[g * NSA_HPG + p, pl.ds(c_start, n_cb), :] for p in range(NSA_HPG)], axis=1), NEG_INF))
    es = both(lambda g: jnp.exp(ss[g] - jnp.max(ss[g], axis=0, keepdims=True)))
    pcs = both(lambda g: es[g] / jnp.sum(es[g], axis=0, keepdims=True) * valid4.astype(F32))
    o_cmps = both(lambda g: jnp.dot(vcmp_ref[g], pcs[g].astype(BF16), preferred_element_type=F32))
    psums = both(lambda g: functools.reduce(
        lambda u, w: u + w, [pcs[g][:, p * Q_TILE:(p + 1) * Q_TILE] for p in range(NSA_HPG)]))

    imps = both(lambda g: _dot_exact_lhs(ovl_ref[...], psums[g])[0:n_selp])
    jb = lax.broadcasted_iota(jnp.int32, (n_selp, Q_TILE), 0)
    qb = tq // SEL_BLOCK
    forced = (jb == 0) | ((jb <= qb) & (jb > qb - SEL_LOCAL))

    def masked_score(imp):
        score = jnp.where(forced, FORCE_SCORE, imp)
        score = jnp.where(jb <= qb, score, -1.0)
        return jnp.where(jb < n_sel, score, -2.0)

    scores = both(lambda g: masked_score(imps[g]))
    ranks = both(lambda g: jnp.zeros((n_selp, Q_TILE), F32))
    for i in range(n_sel):
        for g in groups:
            si = scores[g][i:i + 1, :]
            ahead = (si > scores[g]) | ((si == scores[g]) & (jb > i))
            ranks[g] = ranks[g] + ahead.astype(F32)
    for g in groups:
        sel = (ranks[g] < float(min(SEL_TOP_N, n_sel))) & (jb < n_sel)
        nmask_ref[g] = (sel.astype(F32) - 1.0) * (-NEG_INF)

    pair = 2 * K_TILE
    per_pair = pair // SEL_BLOCK

    def sel_body(kp, carry):
        k0 = pl.multiple_of(kp * pair, pair)
        rel = qt - 2 * kp
        kind_a = jnp.where(rel >= 2, 0, jnp.where(rel == 1, 1, 2))
        kind_b = jnp.where(rel >= 3, 0, jnp.where(rel == 2, 1, jnp.where(rel == 1, 2, 4)))
        def logits(g):
            s = jnp.dot(ksel_ref[g, pl.ds(k0, pair), :], q_cats[g], preferred_element_type=F32)
            bias = jnp.concatenate([btab_ref[g, kind_a], btab_ref[g, kind_b]], axis=0)
            nm = jnp.concatenate(
                [jnp.broadcast_to(nmask_ref[g, pl.ds(kp * per_pair + j, 1), :], (SEL_BLOCK, Q_TILE))
                 for j in range(per_pair)], axis=0)
            return s + bias + jnp.concatenate([nm] * NSA_HPG, axis=1)

        ss = both(logits)
        m_new = both(lambda g: jnp.maximum(carry[g][0], jnp.max(ss[g], axis=0, keepdims=True)))
        ps = both(lambda g: jnp.exp(ss[g] - m_new[g]))
        alpha = both(lambda g: jnp.exp(carry[g][0] - m_new[g]))
        l_new = both(lambda g: alpha[g] * carry[g][1] + jnp.sum(ps[g], axis=0, keepdims=True))
        pv = both(lambda g: jnp.dot(vsel_ref[g, :, pl.ds(k0, pair)], ps[g].astype(BF16),
                                    preferred_element_type=F32))
        return tuple((m_new[g], l_new[g], alpha[g] * carry[g][2] + pv[g]) for g in groups)

    init = (jnp.full((1, cols), NEG_INF, F32), jnp.zeros((1, cols), F32),
            jnp.zeros((HEAD_DIM, cols), F32))
    sel_out = lax.fori_loop(0, qt // 2 + 1, sel_body, tuple(init for _ in groups))

    n_wb = WINDOW // K_TILE
    band = WINDOW + K_TILE
    band_kinds = (3,) + (0,) * (n_wb - 2) + (1, 2)
    w0 = pl.multiple_of(qt * K_TILE, K_TILE)

    def band_logits(g, s):
        blocks = []
        for j, kind in enumerate(band_kinds):
            sj = s[j * K_TILE:(j + 1) * K_TILE]
            if kind != 0:
                sj = sj + btab_ref[g, kind]
            if j < n_wb:
                sj = sj + jnp.where(qt < n_wb - j, NEG_INF, 0.0)
            blocks.append(sj)
        return jnp.concatenate(blocks, axis=0)

    sw = both(lambda g: jnp.dot(kwin_ref[g, pl.ds(w0, band), :], q_cats[g], preferred_element_type=F32))
    sw = both(lambda g: band_logits(g, sw[g]))
    pws = both(lambda g: jnp.exp(sw[g] - jnp.max(sw[g], axis=0, keepdims=True)))
    l_ws = both(lambda g: jnp.sum(pws[g], axis=0, keepdims=True))
    o_wins = both(lambda g: jnp.dot(vwin_ref[g, :, pl.ds(w0, band)], pws[g].astype(BF16),
                                    preferred_element_type=F32) / l_ws[g])
    for g in groups:
        o_win = o_wins[g]
        _, l_s, acc_s = sel_out[g]
        o_sel = acc_s / l_s
        o_cmp = o_cmps[g]

        for p in range(NSA_HPG):
            h = g * NSA_HPG + p
            cs = slice(p * Q_TILE, (p + 1) * Q_TILE)
            o = (gates[h:h + 1] * o_cmp[:, cs]
                 + gates[NSA_HEADS + h:NSA_HEADS + h + 1] * o_sel[:, cs]
                 + gates[2 * NSA_HEADS + h:2 * NSA_HEADS + h + 1] * o_win[:, cs])
            ssq = ssq + jnp.sum(o * o, axis=0, keepdims=True)
            ybuf_ref[h * HEAD_DIM:(h + 1) * HEAD_DIM, :] = o

    yT = ybuf_ref[...] * lax.rsqrt(ssq / NSA_WIDTH + RMS_EPS) * og_ref[...]
    out_ref[0] = yT.T.astype(out_ref.dtype)


def _nsa_mixer(q, kv, gates, q_gain, k_gain, cmp_pe, cmp_w1, cmp_b1, cmp_w2, cmp_b2, out_gain, rel_table):
    bsz, seq, _ = q.shape
    assert seq % (2 * K_TILE) == 0 and seq // CMP_STRIDE == LANES and seq // SEL_BLOCK <= LANES
    n_qt = seq // Q_TILE
    n_cb = seq // CMP_STRIDE
    n_sel = seq // SEL_BLOCK
    n_selp = -(-n_sel // 8) * 8
    cols = NSA_HPG * Q_TILE

    kk = np.arange(K_TILE)[:, None]
    qq = np.arange(Q_TILE)[None, :]
    bk0 = _rel_bucket_np(qq - kk)
    bk1 = _rel_bucket_np(K_TILE + qq - kk)
    cprime = np.arange(2 * LANES)[:, None] - LANES
    bkc = _rel_bucket_np(qq - (CMP_BLOCK - 1) - CMP_STRIDE * cprime)
    c_start = np.arange(n_cb)[None, :] * CMP_STRIDE
    j_start = np.arange(LANES)[:, None] * SEL_BLOCK
    ovl = np.clip(np.minimum(c_start + CMP_BLOCK, j_start + SEL_BLOCK) - np.maximum(c_start, j_start), 0, None)
    ovl = (ovl.astype(np.float32) / CMP_STRIDE)
    ovl[n_sel:, :] = 0.0

    w1 = cmp_w1.reshape(2, CMP_BLOCK, HEAD_DIM, CMP_HIDDEN).astype(BF16)
    full = lambda shape: pl.BlockSpec(shape, lambda b, t: (0,) * len(shape))
    kernel = functools.partial(_nsa_kernel, seq=seq)
    return pl.pallas_call(
        kernel,
        grid=(bsz, n_qt),
        in_specs=[
            pl.BlockSpec(memory_space=pltpu.SMEM),
            pl.BlockSpec((1, Q_TILE, NSA_WIDTH), lambda b, t: (b, t, 0)),
            pl.BlockSpec((1, seq, 6 * KV_WIDTH), lambda b, t: (b, 0, 0)),
            pl.BlockSpec((1, Q_TILE, LANES), lambda b, t: (b, t, 0)),
            full((HEAD_DIM, Q_TILE)),
            full((3, HEAD_DIM)),
            full((2, CMP_BLOCK, HEAD_DIM)),
            full((2, CMP_BLOCK, HEAD_DIM, CMP_HIDDEN)),
            full((2, 1, CMP_HIDDEN)),
            full((2, CMP_HIDDEN, HEAD_DIM)),
            full((2, 1, HEAD_DIM)),
            full((NSA_WIDTH, Q_TILE)),
            full((K_TILE, Q_TILE)),
            full((K_TILE, Q_TILE)),
            full((2 * LANES, Q_TILE)),
            full((LANES, n_cb)),
        ],
        out_specs=pl.BlockSpec((1, Q_TILE, NSA_WIDTH), lambda b, t: (b, t, 0)),
        out_shape=jax.ShapeDtypeStruct((bsz, seq, NSA_WIDTH), BF16),
        scratch_shapes=[
            pltpu.VMEM((2, seq + 2 * CMP_STRIDE, KV_WIDTH), F32),
            pltpu.VMEM((NSA_KV_HEADS, n_cb, HEAD_DIM), BF16),
            pltpu.VMEM((NSA_KV_HEADS, HEAD_DIM, n_cb), BF16),
            pltpu.VMEM((NSA_KV_HEADS, seq, HEAD_DIM), BF16),
            pltpu.VMEM((NSA_KV_HEADS, HEAD_DIM, seq), BF16),
            pltpu.VMEM((NSA_KV_HEADS, WINDOW + seq, HEAD_DIM), BF16),
            pltpu.VMEM((NSA_KV_HEADS, HEAD_DIM, WINDOW + seq), BF16),
            pltpu.VMEM((NSA_KV_HEADS, 5, K_TILE, cols), F32),
            pltpu.VMEM((NSA_HEADS, 2 * LANES, Q_TILE), F32),
            pltpu.VMEM((NSA_KV_HEADS, n_selp, Q_TILE), F32),
            pltpu.VMEM((NSA_WIDTH, Q_TILE), F32),
        ],
        compiler_params=pltpu.CompilerParams(
            dimension_semantics=("arbitrary", "arbitrary"), vmem_limit_bytes=VMEM_LIMIT_BYTES),
        name="nsa_mixer",
    )(rel_table, q, kv, gates, jnp.broadcast_to(q_gain[:, None], (HEAD_DIM, Q_TILE)), k_gain, cmp_pe, w1,
      cmp_b1.reshape(2, 1, CMP_HIDDEN), cmp_w2.astype(BF16), cmp_b2.reshape(2, 1, HEAD_DIM),
      jnp.broadcast_to(out_gain[:, None], (NSA_WIDTH, Q_TILE)), jnp.asarray(bk0), jnp.asarray(bk1),
      jnp.asarray(bkc), jnp.asarray(ovl, BF16))


def _rwkv_kernel(zr_ref, zl_ref, zg_ref, mr_ref, ml_ref, mg_ref, w0_ref, w2_ref, a0_ref, a2_ref, g2_ref,
                 kk_ref, ka_ref, rk_ref, lng_ref, lnb_ref, tri_ref,
                 out_ref,
                 state_ref, cr_ref, cl_ref, cg_ref):
    t = pl.program_id(1)
    tt = WKV_TILE
    n = RWKV_HEAD_DIM
    nck = tt // WKV_CHUNK

    @pl.when(t == 0)
    def _reset():
        state_ref[...] = jnp.zeros_like(state_ref)
        cr_ref[...] = jnp.zeros_like(cr_ref)
        cl_ref[...] = jnp.zeros_like(cl_ref)
        cg_ref[...] = jnp.zeros_like(cg_ref)

    row = lax.broadcasted_iota(jnp.int32, (tt, 1), 0)

    def shifted(z_ref, mix_ref, carry_ref):
        z = z_ref[0]
        prev = jnp.where(row == 0, carry_ref[...], pltpu.roll(z, 1, 0))
        carry_ref[...] = z_ref[0, tt - 1:tt, :]
        return z + mix_ref[...] * (prev - z)

    zr = shifted(zr_ref, mr_ref, cr_ref)
    zl = shifted(zl_ref, ml_ref, cl_ref)
    zg = shifted(zg_ref, mg_ref, cg_ref)
    r = zr[:, 0:RWKV_WIDTH]
    k = zr[:, RWKV_WIDTH:2 * RWKV_WIDTH]
    v = zr[:, 2 * RWKV_WIDTH:3 * RWKV_WIDTH]

    xw = w0_ref[...] + _dot(jnp.tanh(zl[:, 0:DECAY_LORA]), w2_ref[...])
    neg = -xw
    softplus = jnp.maximum(neg, 0.0) + jnp.log(1.0 + jnp.exp(-jnp.abs(neg)))
    lw = -jnp.exp(-softplus - 0.5)
    a = jax.nn.sigmoid(a0_ref[...] + _dot(zl[:, DECAY_LORA:DECAY_LORA + ICLR_LORA], a2_ref[...]))
    gate = _dot(jax.nn.sigmoid(zg), g2_ref[...])

    cum = _dot_exact_lhs(tri_ref[...], lw)
    cum_end = cum.reshape(nck, WKV_CHUNK, RWKV_WIDTH)[:, WKV_CHUNK - 1:WKV_CHUNK, :]
    cum_end_b = jnp.broadcast_to(cum_end, (nck, WKV_CHUNK, RWKV_WIDTH)).reshape(tt, RWKV_WIDTH)
    e_incl = jnp.exp(cum)
    e_excl = jnp.exp(cum - lw)
    e_neg = jnp.exp(-cum)
    e_rev = jnp.exp(cum_end_b - cum)
    g_end = jnp.exp(cum_end)

    cw = WKV_CHUNK
    lane = lax.broadcasted_iota(jnp.int32, (1, 2 * n), 1)
    lo = lane < n
    ri = lax.broadcasted_iota(jnp.int32, (cw, 4 * n), 0)
    ci = lax.broadcasted_iota(jnp.int32, (cw, 4 * n), 1) % n
    strict = ri > ci
    incl = ri >= ci
    eye_row = (ri == ci).astype(F32)
    i2 = lax.broadcasted_iota(jnp.int32, (2 * n, 2 * n), 0)
    j2 = lax.broadcasted_iota(jnp.int32, (2 * n, 2 * n), 1)
    same_head = (i2 // n) == (j2 // n)
    eye_2n = (i2 == j2).astype(F32)
    zeros_slab = jnp.zeros((cw, 2 * n), BF16)

    def seg_sum(u):
        s_lo = jnp.sum(jnp.where(lo, u, 0.0), axis=-1, keepdims=True)
        s_hi = jnp.sum(jnp.where(lo, 0.0, u), axis=-1, keepdims=True)
        return jnp.where(lo, s_lo, s_hi)

    def block_diag(u):
        ua, ub = u[:, 0:2 * n], u[:, 2 * n:4 * n]
        parts = []
        for w, first in ((ua, True), (ub, False)):
            for keep_lo in (True, False):
                blk = jnp.where(lo, w, 0.0) if keep_lo else jnp.where(lo, 0.0, w)
                blk = blk.astype(BF16)
                parts.append(jnp.concatenate([blk, zeros_slab] if first else [zeros_slab, blk], axis=1))
        return jnp.concatenate(parts, axis=0)

    rows = lambda c: slice(c * cw, (c + 1) * cw)

    n_pairs = RWKV_HEADS // 2
    slab = lambda p: slice(2 * n * p, 2 * n * (p + 1))
    pre = []
    for p in range(n_pairs):
        sl = slab(p)
        rp, kp, vp, ap = r[:, sl], k[:, sl], v[:, sl], a[:, sl]
        kkh = kp * kk_ref[:, sl]
        kkn = kkh / jnp.maximum(jnp.sqrt(seg_sum(kkh * kkh)), 1e-12)
        kmod = kp * (1.0 + (ap - 1.0) * ka_ref[:, sl])
        bv = kkn * ap
        a_t = -kkn * e_excl[:, sl]
        r_t = rp * e_incl[:, sl]
        b_hat = bv * e_neg[:, sl]
        k_hat = kmod * e_neg[:, sl]
        mms = []
        for c in range(nck):
            bh, kh = b_hat[rows(c)], k_hat[rows(c)]
            rhs = jnp.concatenate([jnp.where(lo, bh, 0.0), jnp.where(lo, 0.0, bh),
                                   jnp.where(lo, kh, 0.0), jnp.where(lo, 0.0, kh)], axis=0)
            mms.append(_dot_nt(jnp.concatenate([a_t[rows(c)], r_t[rows(c)]], axis=0), rhs))
        pre.append(dict(r=rp, v=vp, kmod=kmod, a_t=a_t, r_t=r_t, b_end=bv * e_rev[:, sl],
                        k_end=kmod * e_rev[:, sl], mms=mms))

    chains = [(p, c0) for p in range(n_pairs) for c0 in range(0, nck, 2)]

    def pair_row(c0, f):
        return jnp.concatenate([f(c0), f(c0 + 1)], axis=1)

    quads = []
    for p, c0 in chains:
        mms = pre[p]["mms"]
        quads.append((
            jnp.where(strict, pair_row(c0, lambda c: mms[c][0:cw, 0:2 * n]), 0.0),
            jnp.where(strict, pair_row(c0, lambda c: mms[c][0:cw, 2 * n:4 * n]), 0.0),
            jnp.where(incl, pair_row(c0, lambda c: mms[c][cw:2 * cw, 0:2 * n]), 0.0),
            jnp.where(incl, pair_row(c0, lambda c: mms[c][cw:2 * cw, 2 * n:4 * n]), 0.0)))
    tinvs = [eye_row + q[0] for q in quads]
    xs = [_dot(q[0], block_diag(q[0])) for q in quads]
    for _ in range(int(math.log2(cw)) - 2):
        txs = [_dot(jnp.concatenate([t_, x_], axis=0), block_diag(x_)) for t_, x_ in zip(tinvs, xs)]
        tinvs = [t_ + tx[0:cw] for t_, tx in zip(tinvs, txs)]
        xs = [tx[cw:2 * cw] for tx in txs]
    tinvs = [t_ + _dot(t_, block_diag(x_)) for t_, x_ in zip(tinvs, xs)]
    lvs = [_dot(jnp.concatenate([q[1], q[3]], axis=0),
                block_diag(pair_row(c0, lambda c: pre[p]["v"][rows(c)])))
           for (p, c0), q in zip(chains, quads)]
    w_rows = [_dot(t_, block_diag(pair_row(c0, lambda c: pre[p]["a_t"][rows(c)])))
              for (p, c0), t_ in zip(chains, tinvs)]
    u_rows = [_dot(t_, block_diag(lv[0:cw])) for t_, lv in zip(tinvs, lvs)]
    qp_rows = [pair_row(c0, lambda c: pre[p]["r_t"][rows(c)]) + _dot(q[2], block_diag(w_))
               for (p, c0), q, w_ in zip(chains, quads, w_rows)]
    y0_rows = [_dot(q[2], block_diag(u_)) + lv[cw:2 * cw] for q, u_, lv in zip(quads, u_rows, lvs)]
    phs = {}
    for ci, (p, c0) in enumerate(chains):
        for j, c in enumerate((c0, c0 + 1)):
            ls = slice(2 * n * j, 2 * n * (j + 1))
            lhs = jnp.concatenate([pre[p]["b_end"][rows(c)], pre[p]["k_end"][rows(c)]], axis=0)
            rhs = jnp.concatenate(
                [jnp.concatenate([w_rows[ci][:, ls], u_rows[ci][:, ls]], axis=1),
                 jnp.concatenate([jnp.zeros((cw, 2 * n), F32), pre[p]["v"][rows(c)]], axis=1)], axis=0)
            ph = _dot_tn(lhs, rhs)
            pmat = eye_2n * g_end[c, :, slab(p)] + jnp.where(same_head, ph[:, 0:2 * n], 0.0)
            phs[(p, c)] = (jnp.concatenate([qp_rows[ci][:, ls], pmat], axis=0),
                           jnp.where(same_head, ph[:, 2 * n:4 * n], 0.0), y0_rows[ci][:, ls])
    hstates = [state_ref[p] for p in range(n_pairs)]
    ys = [[] for _ in range(n_pairs)]
    for c in range(nck):
        for p in range(n_pairs):
            qpm, hp, y0 = phs[(p, c)]
            qph = _dot(qpm, hstates[p])
            ys[p].append(qph[0:cw] + y0)
            hstates[p] = qph[cw:cw + 2 * n] + hp
    for p in range(n_pairs):
        sl = slab(p)
        state_ref[p] = hstates[p]
        y = jnp.concatenate(ys[p], axis=0)
        mu = seg_sum(y) * (1.0 / n)
        yc = y - mu
        var = seg_sum(yc * yc) * (1.0 / n)
        yn = yc * lax.rsqrt(var + GN_EPS) * lng_ref[:, sl] + lnb_ref[:, sl]
        bonus = seg_sum(pre[p]["r"] * pre[p]["kmod"] * rk_ref[:, sl]) * pre[p]["v"]
        out_ref[0, :, sl] = ((yn + bonus) * gate[:, sl]).astype(out_ref.dtype)


def _rwkv_mixer(zr, zl, zg, mix_r, mix_l, mix_g, w0, w2, a0, a2, g2p, k_k, k_a, r_k, ln_g, ln_b):
    bsz, seq, _ = zr.shape
    tt = WKV_TILE
    assert seq % tt == 0 and tt % (2 * WKV_CHUNK) == 0 and 2 * RWKV_HEAD_DIM == LANES
    idx = np.arange(tt)
    tri = ((idx[:, None] // WKV_CHUNK == idx[None, :] // WKV_CHUNK) & (idx[:, None] >= idx[None, :]))
    full = lambda shape: pl.BlockSpec(shape, lambda b, t: (0,) * len(shape))
    row = lambda a: a.reshape(1, -1)
    wl = DECAY_LORA + ICLR_LORA
    return pl.pallas_call(
        _rwkv_kernel,
        grid=(bsz, seq // tt),
        in_specs=[
            pl.BlockSpec((1, tt, 3 * RWKV_WIDTH), lambda b, t: (b, t, 0)),
            pl.BlockSpec((1, tt, wl), lambda b, t: (b, t, 0)),
            pl.BlockSpec((1, tt, GATE_LORA_PAD), lambda b, t: (b, t, 0)),
            full((1, 3 * RWKV_WIDTH)), full((1, wl)), full((1, GATE_LORA_PAD)),
            full((1, RWKV_WIDTH)), full((DECAY_LORA, RWKV_WIDTH)),
            full((1, RWKV_WIDTH)), full((ICLR_LORA, RWKV_WIDTH)),
            full((GATE_LORA_PAD, RWKV_WIDTH)),
            full((1, RWKV_WIDTH)), full((1, RWKV_WIDTH)), full((1, RWKV_WIDTH)),
            full((1, RWKV_WIDTH)), full((1, RWKV_WIDTH)),
            full((tt, tt)),
        ],
        out_specs=pl.BlockSpec((1, tt, RWKV_WIDTH), lambda b, t: (b, t, 0)),
        out_shape=jax.ShapeDtypeStruct((bsz, seq, RWKV_WIDTH), BF16),
        scratch_shapes=[
            pltpu.VMEM((RWKV_HEADS // 2, 2 * RWKV_HEAD_DIM, 2 * RWKV_HEAD_DIM), F32),
            pltpu.VMEM((1, 3 * RWKV_WIDTH), F32),
            pltpu.VMEM((1, wl), F32),
            pltpu.VMEM((1, GATE_LORA_PAD), F32),
        ],
        compiler_params=pltpu.CompilerParams(
            dimension_semantics=("arbitrary", "arbitrary"), vmem_limit_bytes=VMEM_LIMIT_BYTES),
        name="rwkv7_mixer",
    )(zr, zl, zg, row(mix_r), row(mix_l), row(mix_g), row(w0), w2.astype(BF16), row(a0), a2.astype(BF16),
      g2p.astype(BF16), row(k_k), row(k_a), row(r_k), row(ln_g), row(ln_b), jnp.asarray(tri, BF16))


def _ffn_kernel(x_ref, yn_ref, yr_ref, won_ref, wor_ref, gain_ref, wg_ref, wu_ref, wd_ref, o_ref, *, ff_chunk):
    h = x_ref[...]
    h = h + jnp.dot(yn_ref[...], won_ref[...], preferred_element_type=F32)
    h = h + jnp.dot(yr_ref[...], wor_ref[...], preferred_element_type=F32)
    ms = jnp.mean(h * h, axis=-1, keepdims=True)
    hn = (h * lax.rsqrt(ms + RMS_EPS) * gain_ref[...]).astype(BF16)
    d_ff = wg_ref.shape[1]
    acc = None
    for c in range(d_ff // ff_chunk):
        cs = slice(c * ff_chunk, (c + 1) * ff_chunk)
        gte = jnp.dot(hn, wg_ref[:, cs], preferred_element_type=F32)
        up = jnp.dot(hn, wu_ref[:, cs], preferred_element_type=F32)
        act = (gte * jax.nn.sigmoid(gte) * up).astype(BF16)
        down = jnp.dot(act, wd_ref[cs, :], preferred_element_type=F32)
        acc = down if acc is None else acc + down
    o_ref[...] = h + acc


def _out_ffn(x2, y_nsa, y_rwkv, w_out, ffn_gain, w_gate, w_up, w_down):
    n, d = x2.shape
    d_ff = w_gate.shape[1]
    tm = min(ROW_TILE // 2, n)
    ff_chunk = d_ff // 2 if (d_ff // 2) % LANES == 0 else d_ff
    const = lambda shape: pl.BlockSpec(shape, lambda i: (0, 0))
    return pl.pallas_call(
        functools.partial(_ffn_kernel, ff_chunk=ff_chunk),
        grid=(n // tm,),
        in_specs=[
            pl.BlockSpec((tm, d), lambda i: (i, 0)),
            pl.BlockSpec((tm, NSA_WIDTH), lambda i: (i, 0)),
            pl.BlockSpec((tm, RWKV_WIDTH), lambda i: (i, 0)),
            const((NSA_WIDTH, d)), const((RWKV_WIDTH, d)), const((1, d)),
            const((d, d_ff)), const((d, d_ff)), const((d_ff, d)),
        ],
        out_specs=pl.BlockSpec((tm, d), lambda i: (i, 0)),
        out_shape=jax.ShapeDtypeStruct((n, d), F32),
        compiler_params=pltpu.CompilerParams(
            dimension_semantics=("arbitrary",), vmem_limit_bytes=VMEM_LIMIT_BYTES),
        name="out_proj_ffn",
    )(x2, y_nsa, y_rwkv, w_out[:NSA_WIDTH].astype(BF16), w_out[NSA_WIDTH:].astype(BF16),
      ffn_gain.reshape(1, d), w_gate.astype(BF16), w_up.astype(BF16), w_down.astype(BF16))


def _layer(h, attn_gain, w_in, q_gain, k_gain, cmp_pe, cmp_w1, cmp_b1, cmp_w2, cmp_b2, out_gain, shift_mix,
           w0, w2, a0, a2, g2, k_k, k_a, r_k, ln_g, ln_b, w_out, ffn_gain, w_gate, w_up, w_down, rel_table):
    bsz, seq, d = h.shape
    nsa_cols = NSA_WIDTH + 6 * KV_WIDTH + 3 * NSA_HEADS
    gate0 = NSA_WIDTH + 6 * KV_WIDTH

    w_nsa, w_rw = w_in[:, :nsa_cols], w_in[:, nsa_cols:]
    gate_src = np.array([gate0 + h_ * 3 + br for br in range(3) for h_ in range(NSA_HEADS)])
    w_gates = jnp.zeros((d, LANES), F32).at[:, :3 * NSA_HEADS].set(w_nsa[:, gate_src])
    lora0 = 3 * RWKV_WIDTH
    g0 = lora0 + DECAY_LORA + ICLR_LORA
    w_g = jnp.zeros((d, GATE_LORA_PAD), F32).at[:, :GATE_LORA].set(w_rw[:, g0:g0 + GATE_LORA])
    widths = (NSA_WIDTH, 6 * KV_WIDTH, LANES, 3 * RWKV_WIDTH, DECAY_LORA + ICLR_LORA, GATE_LORA_PAD)
    w_cat = jnp.concatenate([w_nsa[:, :gate0], w_gates, w_rw[:, :g0], w_g], axis=1).astype(BF16)
    mix_g = jnp.zeros((GATE_LORA_PAD,), F32).at[:GATE_LORA].set(shift_mix[g0:g0 + GATE_LORA])
    g2p = jnp.zeros((GATE_LORA_PAD, RWKV_WIDTH), F32).at[:GATE_LORA].set(g2)

    x2 = h.reshape(bsz * seq, d)
    q, kv, gates, zr, zl, zg = _in_projection(x2, attn_gain.reshape(1, d), w_cat, widths)
    r3 = lambda u: u.reshape(bsz, seq, u.shape[-1])
    y_nsa = _nsa_mixer(r3(q), r3(kv), r3(gates), q_gain, k_gain, cmp_pe, cmp_w1, cmp_b1, cmp_w2, cmp_b2,
                       out_gain, rel_table)
    y_rwkv = _rwkv_mixer(r3(zr), r3(zl), r3(zg), shift_mix[:lora0], shift_mix[lora0:g0], mix_g,
                         w0, w2, a0, a2, g2p, k_k, k_a, r_k.reshape(-1), ln_g, ln_b)
    out = _out_ffn(x2, y_nsa.reshape(bsz * seq, NSA_WIDTH), y_rwkv.reshape(bsz * seq, RWKV_WIDTH),
                   w_out, ffn_gain, w_gate, w_up, w_down)
    return out.reshape(bsz, seq, d)


def kernel(x, attn_norm_gain, w_in, nsa_q_gain, nsa_k_gain, cmp_pe, cmp_w1, cmp_b1, cmp_w2, cmp_b2, nsa_out_gain, rwkv_shift_mix, rwkv_w0, rwkv_w2, rwkv_a0, rwkv_a2, rwkv_g2, rwkv_k_k, rwkv_k_a, rwkv_r_k, rwkv_ln_gain, rwkv_ln_bias, w_out, ffn_norm_gain, w_gate, w_up, w_down, rel_bias_table):
    h = x
    for l in range(attn_norm_gain.shape[0]):
        h = _layer(h, attn_norm_gain[l], w_in[l], nsa_q_gain[l], nsa_k_gain[l], cmp_pe[l], cmp_w1[l], cmp_b1[l],
                   cmp_w2[l], cmp_b2[l], nsa_out_gain[l], rwkv_shift_mix[l], rwkv_w0[l], rwkv_w2[l], rwkv_a0[l],
                   rwkv_a2[l], rwkv_g2[l], rwkv_k_k[l], rwkv_k_a[l], rwkv_r_k[l], rwkv_ln_gain[l],
                   rwkv_ln_bias[l], w_out[l], ffn_norm_gain[l], w_gate[l], w_up[l], w_down[l], rel_bias_table)
    return h
```

```python
import functools
import math

import numpy as np
import jax
import jax.numpy as jnp
from jax import lax
from jax.experimental import pallas as pl
from jax.experimental.pallas import tpu as pltpu

HEAD_DIM = 64
NSA_HEADS = 8
NSA_KV_HEADS = 2
NSA_HPG = NSA_HEADS // NSA_KV_HEADS
NSA_WIDTH = NSA_HEADS * HEAD_DIM
KV_WIDTH = NSA_KV_HEADS * HEAD_DIM
CMP_BLOCK = 32
CMP_STRIDE = 16
CMP_HIDDEN = 256
SEL_BLOCK = 64
SEL_TOP_N = 16
SEL_LOCAL = 2
WINDOW = 512
ATTN_SCALE = HEAD_DIM ** -0.5
LOG2E = math.log2(math.e)
NEG_INF = -1e30
FORCE_SCORE = 1e9
REL_BUCKETS = 32
REL_MAX_DIST = 128
RWKV_HEADS = 8
RWKV_HEAD_DIM = 64
RWKV_WIDTH = RWKV_HEADS * RWKV_HEAD_DIM
DECAY_LORA = 64
ICLR_LORA = 64
GATE_LORA = 160
GATE_LORA_PAD = 256
GN_EPS = 64e-5
RMS_EPS = 1e-6

LANES = 128
Q_TILE = 128
TILES_PER_STEP = 2
K_TILE = 128
WKV_TILE = 256
WKV_CHUNK = 64
ROW_TILE = 512
VMEM_LIMIT_BYTES = 56 * 1024 * 1024

F32 = jnp.float32
BF16 = jnp.bfloat16


def _dot(a, b):
    return jnp.dot(a.astype(BF16), b.astype(BF16), preferred_element_type=F32)


def _dot_nt(a, b):
    return lax.dot_general(a.astype(BF16), b.astype(BF16), (((1,), (1,)), ((), ())),
                           preferred_element_type=F32)


def _dot_tn(a, b):
    return lax.dot_general(a.astype(BF16), b.astype(BF16), (((0,), (0,)), ((), ())),
                           preferred_element_type=F32)


def _split3(x):
    hi = x.astype(BF16)
    r1 = x - hi.astype(F32)
    mid = r1.astype(BF16)
    lo = (r1 - mid.astype(F32)).astype(BF16)
    return hi, mid, lo


def _dot_exact_rhs(x, m_bf16):
    hi, mid, lo = _split3(x)
    acc = jnp.dot(lo, m_bf16, preferred_element_type=F32)
    acc = acc + jnp.dot(mid, m_bf16, preferred_element_type=F32)
    return acc + jnp.dot(hi, m_bf16, preferred_element_type=F32)


def _dot_exact_lhs(m_bf16, x):
    hi, mid, lo = _split3(x)
    acc = jnp.dot(m_bf16, lo, preferred_element_type=F32)
    acc = acc + jnp.dot(m_bf16, mid, preferred_element_type=F32)
    return acc + jnp.dot(m_bf16, hi, preferred_element_type=F32)


def _rel_bucket_np(dist):
    max_exact = REL_BUCKETS // 2
    d = np.maximum(dist, 0)
    ratio = np.maximum(d, 1).astype(np.float32) / np.float32(max_exact)
    log_ratio = np.log(ratio).astype(np.float32) / np.float32(math.log(REL_MAX_DIST / max_exact))
    large = np.minimum(max_exact + (log_ratio * np.float32(REL_BUCKETS - max_exact)).astype(np.int32),
                       REL_BUCKETS - 1)
    return np.where(d < max_exact, d, large).astype(np.int32)


def _inproj_kernel(x_ref, gain_ref, w_ref, *out_refs, widths):
    x = x_ref[...]
    ms = jnp.mean(x * x, axis=-1, keepdims=True)
    xn = (x * lax.rsqrt(ms + RMS_EPS) * gain_ref[...]).astype(BF16)
    off = 0
    for o_ref, w in zip(out_refs, widths):
        o_ref[...] = jnp.dot(xn, w_ref[:, off:off + w], preferred_element_type=F32)
        off += w


def _in_projection(x2, gain, w_cat, widths):
    n, d = x2.shape
    tm = min(ROW_TILE, n)
    total = sum(widths)
    return pl.pallas_call(
        functools.partial(_inproj_kernel, widths=widths),
        grid=(n // tm,),
        in_specs=[
            pl.BlockSpec((tm, d), lambda i: (i, 0)),
            pl.BlockSpec((1, d), lambda i: (0, 0)),
            pl.BlockSpec((d, total), lambda i: (0, 0)),
        ],
        out_specs=[pl.BlockSpec((tm, w), lambda i: (i, 0)) for w in widths],
        out_shape=[jax.ShapeDtypeStruct((n, w), F32) for w in widths],
        compiler_params=pltpu.CompilerParams(
            dimension_semantics=("arbitrary",), vmem_limit_bytes=VMEM_LIMIT_BYTES),
        name="in_projection",
    )(x2, gain, w_cat)


def _head_rms(u, gain):
    ms = jnp.mean(u * u, axis=-1, keepdims=True)
    return u * lax.rsqrt(ms + RMS_EPS) * gain


def _nsa_kernel(tab_ref,
                q_ref, kv_ref, gate_ref,
                qg_ref, kg_ref, pe_ref, w1_ref, b1_ref, w2_ref, b2_ref, og_ref,
                bk0_ref, bk1_ref, bkc_ref, ovl_ref,
                out_ref,
                kvpad_ref, kcmp_ref, vcmp_ref, ksel_ref, vsel_ref, kwin_ref, vwin_ref,
                btab_ref, ccan_ref, ybuf_ref,
                *, seq):
    b = pl.program_id(0)
    step = pl.program_id(1)
    n_cb = seq // CMP_STRIDE
    n_c = (seq - CMP_BLOCK) // CMP_STRIDE + 1
    n_sel = seq // SEL_BLOCK
    cols = NSA_HPG * Q_TILE

    @pl.when((b == 0) & (step == 0))
    def _build_bias_tables():
        bk0 = bk0_ref[...]
        bk1 = bk1_ref[...]
        bkc = bkc_ref[...]
        kk = lax.broadcasted_iota(jnp.int32, (K_TILE, Q_TILE), 0)
        qq = lax.broadcasted_iota(jnp.int32, (K_TILE, Q_TILE), 1)
        causal = jnp.where(kk <= qq, 0.0, NEG_INF).astype(F32)
        anti = jnp.where(kk > qq, 0.0, NEG_INF).astype(F32)
        for h in range(NSA_HEADS):
            t0 = jnp.zeros((K_TILE, Q_TILE), F32)
            t1 = jnp.zeros((K_TILE, Q_TILE), F32)
            tc = jnp.zeros((2 * LANES, Q_TILE), F32)
            for k in range(REL_BUCKETS):
                val = tab_ref[k, h]
                t0 = jnp.where(bk0 == k, val, t0)
                t1 = jnp.where(bk1 == k, val, t1)
                tc = jnp.where(bkc == k, val, tc)
            far = tab_ref[REL_BUCKETS - 1, h]
            g, p = divmod(h, NSA_HPG)
            cs = slice(p * Q_TILE, (p + 1) * Q_TILE)
            btab_ref[g, 0, :, cs] = jnp.zeros((K_TILE, Q_TILE), F32)
            btab_ref[g, 1, :, cs] = (t1 - far) * LOG2E
            btab_ref[g, 2, :, cs] = (t0 - far) * LOG2E + causal
            btab_ref[g, 3, :, cs] = anti
            btab_ref[g, 4, :, cs] = jnp.full((K_TILE, Q_TILE), NEG_INF, F32)
            ccan_ref[h] = tc * LOG2E
        blk = lax.broadcasted_iota(jnp.int32, (seq, HEAD_DIM), 0) // SEL_BLOCK
        col = lax.broadcasted_iota(jnp.int32, (seq, HEAD_DIM), 1)
        onehot = (blk == col).astype(BF16)
        padcol = (lax.broadcasted_iota(jnp.int32, (WINDOW, 2 * HEAD_DIM), 1) == HEAD_DIM).astype(BF16)
        for g in range(NSA_KV_HEADS):
            ksel_ref[g, :, HEAD_DIM:2 * HEAD_DIM] = onehot
            kwin_ref[g, 0:WINDOW, :] = padcol
            kwin_ref[g, WINDOW:WINDOW + seq, HEAD_DIM:2 * HEAD_DIM] = jnp.zeros((seq, HEAD_DIM), BF16)
            vwin_ref[g, :, 0:WINDOW] = jnp.zeros((HEAD_DIM, WINDOW), BF16)

    @pl.when(step == 0)
    def _per_batch():
        for which in range(2):
            kvpad_ref[which, seq:seq + 2 * CMP_STRIDE, :] = jnp.zeros((2 * CMP_STRIDE, KV_WIDTH), F32)
            kvpad_ref[which, 0:seq, :] = kv_ref[0, :, which * KV_WIDTH:(which + 1) * KV_WIDTH]
            cmps = []
            for g in range(NSA_KV_HEADS):
                c0 = g * HEAD_DIM
                hid = jnp.zeros((n_cb, CMP_HIDDEN), F32)
                for l in range(CMP_BLOCK):
                    rows_l = kvpad_ref[which, pl.ds(l, n_cb, stride=CMP_STRIDE), :]
                    piece = rows_l[:, c0:c0 + HEAD_DIM] + pe_ref[which, l:l + 1, :]
                    hid = hid + _dot(piece, w1_ref[which, l])
                hid = jax.nn.gelu(hid + b1_ref[which], approximate=True)
                cmps.append(_dot(hid, w2_ref[which]) + b2_ref[which])
            if which == 0:
                for g in range(NSA_KV_HEADS):
                    kcmp_ref[g] = _head_rms(cmps[g], kg_ref[0:1, :]).astype(BF16)
            else:
                vct = jnp.concatenate(cmps, axis=1).T
                for g in range(NSA_KV_HEADS):
                    vcmp_ref[g] = vct[g * HEAD_DIM:(g + 1) * HEAD_DIM].astype(BF16)
        kv = kv_ref[0]
        for g in range(NSA_KV_HEADS):
            def col(i):
                c = (2 + i) * KV_WIDTH + g * HEAD_DIM
                return kv[:, c:c + HEAD_DIM]
            ksel_ref[g, :, 0:HEAD_DIM] = _head_rms(col(0), kg_ref[1:2, :]).astype(BF16)
            kwin_ref[g, WINDOW:WINDOW + seq, 0:HEAD_DIM] = _head_rms(col(2), kg_ref[2:3, :]).astype(BF16)
        for i, vt_ref, pad in ((1, vsel_ref, 0), (3, vwin_ref, WINDOW)):
            vt = kv[:, (2 + i) * KV_WIDTH:(3 + i) * KV_WIDTH].T
            for g in range(NSA_KV_HEADS):
                vt_ref[g, :, pad:pad + seq] = vt[g * HEAD_DIM:(g + 1) * HEAD_DIM].astype(BF16)

    n_selp = -(-n_sel // 8) * 8
    qT_all = q_ref[0].T
    gates_all = jax.nn.sigmoid(gate_ref[0].T)
    tiles = range(TILES_PER_STEP)
    groups = range(NSA_KV_HEADS)
    units = [(j, g) for j in tiles for g in groups]
    each = lambda f: {u: f(*u) for u in units}
    qts = [TILES_PER_STEP * step + j for j in tiles]
    tqs = [qts[j] * Q_TILE + lax.broadcasted_iota(jnp.int32, (1, Q_TILE), 1) for j in tiles]
    lanes = lambda j: slice(j * Q_TILE, (j + 1) * Q_TILE)

    def q_group(j, g):
        qs = []
        for p in range(NSA_HPG):
            h = g * NSA_HPG + p
            qh = qT_all[h * HEAD_DIM:(h + 1) * HEAD_DIM, lanes(j)]
            ms = jnp.mean(qh * qh, axis=0, keepdims=True)
            qs.append((qh * lax.rsqrt(ms + RMS_EPS) * qg_ref[...] * (ATTN_SCALE * LOG2E)).astype(BF16))
        return jnp.concatenate(qs, axis=1)

    q_cats = each(q_group)
    aug_row = lax.broadcasted_iota(jnp.int32, (HEAD_DIM, cols), 0)

    cc = lax.broadcasted_iota(jnp.int32, (n_cb, Q_TILE), 0)
    valid4 = [jnp.concatenate([(cc * CMP_STRIDE + (CMP_BLOCK - 1) <= tqs[j]) & (cc < n_c)] * NSA_HPG, axis=1)
              for j in tiles]
    c_starts = [pl.multiple_of(LANES - (CMP_STRIDE // 2) * qts[j], 8) for j in tiles]
    ss = each(lambda j, g: jnp.dot(kcmp_ref[g], q_cats[j, g], preferred_element_type=F32))
    ss = each(lambda j, g: jnp.where(valid4[j], ss[j, g] + jnp.concatenate(
        [ccan_ref[g * NSA_HPG + p, pl.ds(c_starts[j], n_cb), :] for p in range(NSA_HPG)], axis=1), NEG_INF))
    es = each(lambda j, g: jnp.exp2(ss[j, g] - jnp.max(ss[j, g], axis=0, keepdims=True)))
    pcs = each(lambda j, g: es[j, g] / jnp.sum(es[j, g], axis=0, keepdims=True) * valid4[j].astype(F32))
    o_cmps = each(lambda j, g: jnp.dot(vcmp_ref[g], pcs[j, g].astype(BF16), preferred_element_type=F32))
    psums = each(lambda j, g: functools.reduce(
        lambda u, w: u + w, [pcs[j, g][:, p * Q_TILE:(p + 1) * Q_TILE] for p in range(NSA_HPG)]))

    n_wb = WINDOW // K_TILE
    band = WINDOW + K_TILE
    band_kinds = (3,) + (0,) * (n_wb - 2) + (1, 2)
    w0s = [pl.multiple_of(qts[j] * K_TILE, K_TILE) for j in tiles]
    pad_rows = jnp.where(aug_row == 0, NEG_INF, 0.0).astype(BF16)

    def band_logits(g, s):
        blocks = []
        for i, kind in enumerate(band_kinds):
            si = s[i * K_TILE:(i + 1) * K_TILE]
            blocks.append(si if kind == 0 else si + btab_ref[g, kind])
        return jnp.concatenate(blocks, axis=0)

    sw = each(lambda j, g: jnp.dot(kwin_ref[g, pl.ds(w0s[j], band), :],
                                   jnp.concatenate([q_cats[j, g], pad_rows], axis=0),
                                   preferred_element_type=F32))
    sw = each(lambda j, g: band_logits(g, sw[j, g]))
    pws = each(lambda j, g: jnp.exp2(sw[j, g] - jnp.max(sw[j, g], axis=0, keepdims=True)))
    l_ws = each(lambda j, g: jnp.sum(pws[j, g], axis=0, keepdims=True))
    o_wins = each(lambda j, g: jnp.dot(vwin_ref[g, :, pl.ds(w0s[j], band)], pws[j, g].astype(BF16),
                                       preferred_element_type=F32) / l_ws[j, g])

    imps = each(lambda j, g: _dot_exact_lhs(ovl_ref[...], psums[j, g])[0:n_selp])
    jb = lax.broadcasted_iota(jnp.int32, (n_selp, Q_TILE), 0)

    def masked_score(j, g):
        qb = tqs[j] // SEL_BLOCK
        forced = (jb == 0) | ((jb <= qb) & (jb > qb - SEL_LOCAL))
        score = jnp.where(forced, FORCE_SCORE, imps[j, g])
        score = jnp.where(jb <= qb, score, -1.0)
        return jnp.where(jb < n_sel, score, -2.0)

    scores = each(masked_score)
    ranks = each(lambda j, g: jnp.zeros((n_selp, Q_TILE), F32))
    for i in range(n_sel):
        for u in units:
            si = scores[u][i:i + 1, :]
            ahead = (si > scores[u]) | ((si == scores[u]) & (jb > i))
            ranks[u] = ranks[u] + ahead.astype(F32)

    def q_with_mask_rows(j, g):
        sel = (ranks[j, g] < float(min(SEL_TOP_N, n_sel))) & (jb < n_sel)
        nm = (sel.astype(F32) - 1.0) * (-NEG_INF)
        return jnp.concatenate([q_cats[j, g], jnp.concatenate([nm] * NSA_HPG, axis=1).astype(BF16),
                                jnp.zeros((HEAD_DIM - n_selp, cols), BF16)], axis=0)

    q_sels = each(q_with_mask_rows)

    pair = 2 * K_TILE

    def sel_step(kp, carry, near):
        k0 = pl.multiple_of(kp * pair, pair)

        def logits(j, g):
            s = jnp.dot(ksel_ref[g, pl.ds(k0, pair), :], q_sels[j, g], preferred_element_type=F32)
            if near:
                rel = qts[j] - 2 * kp
                kind_a = jnp.where(rel >= 2, 0, jnp.where(rel == 1, 1, 2))
                kind_b = jnp.where(rel >= 3, 0, jnp.where(rel == 2, 1, jnp.where(rel == 1, 2, 4)))
                s = s + jnp.concatenate([btab_ref[g, kind_a], btab_ref[g, kind_b]], axis=0)
            return s

        old = {u: carry[i] for i, u in enumerate(units)}
        ss = each(logits)
        m_new = each(lambda j, g: jnp.maximum(old[j, g][0], jnp.max(ss[j, g], axis=0, keepdims=True)))
        ps = each(lambda j, g: jnp.exp2(ss[j, g] - m_new[j, g]))
        alpha = each(lambda j, g: jnp.exp2(old[j, g][0] - m_new[j, g]))
        l_new = each(lambda j, g: alpha[j, g] * old[j, g][1] + jnp.sum(ps[j, g], axis=0, keepdims=True))
        pv = each(lambda j, g: jnp.dot(vsel_ref[g, :, pl.ds(k0, pair)], ps[j, g].astype(BF16),
                                       preferred_element_type=F32))
        return tuple((m_new[u], l_new[u], alpha[u] * old[u][2] + pv[u]) for u in units)

    init = (jnp.full((1, cols), NEG_INF, F32), jnp.zeros((1, cols), F32),
            jnp.zeros((HEAD_DIM, cols), F32))
    n_pairs = qts[0] // 2 + 1
    n_far = jnp.maximum((qts[0] - 1) // 2, 0)
    sel_mid = lax.fori_loop(0, n_far, functools.partial(sel_step, near=False), tuple(init for _ in units))
    sel_out = lax.fori_loop(n_far, n_pairs, functools.partial(sel_step, near=True), sel_mid)

    ssqs = []
    for j in tiles:
        gates = gates_all[:, lanes(j)]
        ssq = jnp.zeros((1, Q_TILE), F32)
        for g in groups:
            _, l_s, acc_s = sel_out[units.index((j, g))]
            o_sel = acc_s / l_s
            for p in range(NSA_HPG):
                h = g * NSA_HPG + p
                cs = slice(p * Q_TILE, (p + 1) * Q_TILE)
                o = (gates[h:h + 1] * o_cmps[j, g][:, cs]
                     + gates[NSA_HEADS + h:NSA_HEADS + h + 1] * o_sel[:, cs]
                     + gates[2 * NSA_HEADS + h:2 * NSA_HEADS + h + 1] * o_wins[j, g][:, cs])
                ssq = ssq + jnp.sum(o * o, axis=0, keepdims=True)
                ybuf_ref[h * HEAD_DIM:(h + 1) * HEAD_DIM, lanes(j)] = o
        ssqs.append(ssq)
    ssq_all = jnp.concatenate(ssqs, axis=1)
    yT = ybuf_ref[...] * lax.rsqrt(ssq_all / NSA_WIDTH + RMS_EPS) * og_ref[...]
    out_ref[0] = yT.T.astype(out_ref.dtype)


def _nsa_mixer(q, kv, gates, q_gain, k_gain, cmp_pe, cmp_w1, cmp_b1, cmp_w2, cmp_b2, out_gain, rel_table):
    bsz, seq, _ = q.shape
    assert TILES_PER_STEP == 2 and seq % (TILES_PER_STEP * Q_TILE) == 0
    assert seq // CMP_STRIDE == LANES and seq // SEL_BLOCK <= HEAD_DIM
    n_steps = seq // (TILES_PER_STEP * Q_TILE)
    q_rows = TILES_PER_STEP * Q_TILE
    n_cb = seq // CMP_STRIDE
    n_sel = seq // SEL_BLOCK
    n_selp = -(-n_sel // 8) * 8
    cols = NSA_HPG * Q_TILE

    kk = np.arange(K_TILE)[:, None]
    qq = np.arange(Q_TILE)[None, :]
    bk0 = _rel_bucket_np(qq - kk)
    bk1 = _rel_bucket_np(K_TILE + qq - kk)
    cprime = np.arange(2 * LANES)[:, None] - LANES
    bkc = _rel_bucket_np(qq - (CMP_BLOCK - 1) - CMP_STRIDE * cprime)
    c_start = np.arange(n_cb)[None, :] * CMP_STRIDE
    j_start = np.arange(LANES)[:, None] * SEL_BLOCK
    ovl = np.clip(np.minimum(c_start + CMP_BLOCK, j_start + SEL_BLOCK) - np.maximum(c_start, j_start), 0, None)
    ovl = (ovl.astype(np.float32) / CMP_STRIDE)
    ovl[n_sel:, :] = 0.0

    w1 = cmp_w1.reshape(2, CMP_BLOCK, HEAD_DIM, CMP_HIDDEN).astype(BF16)
    full = lambda shape: pl.BlockSpec(shape, lambda b, t: (0,) * len(shape))
    kernel = functools.partial(_nsa_kernel, seq=seq)
    return pl.pallas_call(
        kernel,
        grid=(bsz, n_steps),
        in_specs=[
            pl.BlockSpec(memory_space=pltpu.SMEM),
            pl.BlockSpec((1, q_rows, NSA_WIDTH), lambda b, t: (b, t, 0)),
            pl.BlockSpec((1, seq, 6 * KV_WIDTH), lambda b, t: (b, 0, 0)),
            pl.BlockSpec((1, q_rows, LANES), lambda b, t: (b, t, 0)),
            full((HEAD_DIM, Q_TILE)),
            full((3, HEAD_DIM)),
            full((2, CMP_BLOCK, HEAD_DIM)),
            full((2, CMP_BLOCK, HEAD_DIM, CMP_HIDDEN)),
            full((2, 1, CMP_HIDDEN)),
            full((2, CMP_HIDDEN, HEAD_DIM)),
            full((2, 1, HEAD_DIM)),
            full((NSA_WIDTH, q_rows)),
            full((K_TILE, Q_TILE)),
            full((K_TILE, Q_TILE)),
            full((2 * LANES, Q_TILE)),
            full((LANES, n_cb)),
        ],
        out_specs=pl.BlockSpec((1, q_rows, NSA_WIDTH), lambda b, t: (b, t, 0)),
        out_shape=jax.ShapeDtypeStruct((bsz, seq, NSA_WIDTH), BF16),
        scratch_shapes=[
            pltpu.VMEM((2, seq + 2 * CMP_STRIDE, KV_WIDTH), F32),
            pltpu.VMEM((NSA_KV_HEADS, n_cb, HEAD_DIM), BF16),
            pltpu.VMEM((NSA_KV_HEADS, HEAD_DIM, n_cb), BF16),
            pltpu.VMEM((NSA_KV_HEADS, seq, 2 * HEAD_DIM), BF16),
            pltpu.VMEM((NSA_KV_HEADS, HEAD_DIM, seq), BF16),
            pltpu.VMEM((NSA_KV_HEADS, WINDOW + seq, 2 * HEAD_DIM), BF16),
            pltpu.VMEM((NSA_KV_HEADS, HEAD_DIM, WINDOW + seq), BF16),
            pltpu.VMEM((NSA_KV_HEADS, 5, K_TILE, cols), F32),
            pltpu.VMEM((NSA_HEADS, 2 * LANES, Q_TILE), F32),
            pltpu.VMEM((NSA_WIDTH, q_rows), F32),
        ],
        compiler_params=pltpu.CompilerParams(
            dimension_semantics=("arbitrary", "arbitrary"), vmem_limit_bytes=VMEM_LIMIT_BYTES),
        name="nsa_mixer",
    )(rel_table, q, kv, gates, jnp.broadcast_to(q_gain[:, None], (HEAD_DIM, Q_TILE)), k_gain, cmp_pe, w1,
      cmp_b1.reshape(2, 1, CMP_HIDDEN), cmp_w2.astype(BF16), cmp_b2.reshape(2, 1, HEAD_DIM),
      jnp.broadcast_to(out_gain[:, None], (NSA_WIDTH, q_rows)), jnp.asarray(bk0), jnp.asarray(bk1),
      jnp.asarray(bkc), jnp.asarray(ovl, BF16))


def _rwkv_kernel(zr_ref, zl_ref, zg_ref, mr_ref, ml_ref, mg_ref, w0_ref, w2_ref, a0_ref, a2_ref, g2_ref,
                 kk_ref, ka_ref, rk_ref, lng_ref, lnb_ref, tri_ref,
                 out_ref,
                 state_ref, cr_ref, cl_ref, cg_ref):
    t = pl.program_id(1)
    tt = WKV_TILE
    n = RWKV_HEAD_DIM
    nck = tt // WKV_CHUNK

    @pl.when(t == 0)
    def _reset():
        state_ref[...] = jnp.zeros_like(state_ref)
        cr_ref[...] = jnp.zeros_like(cr_ref)
        cl_ref[...] = jnp.zeros_like(cl_ref)
        cg_ref[...] = jnp.zeros_like(cg_ref)

    row = lax.broadcasted_iota(jnp.int32, (tt, 1), 0)

    def shifted(z_ref, mix_ref, carry_ref):
        z = z_ref[0]
        prev = jnp.where(row == 0, carry_ref[...], pltpu.roll(z, 1, 0))
        carry_ref[...] = z_ref[0, tt - 1:tt, :]
        return z + mix_ref[...] * (prev - z)

    zr = shifted(zr_ref, mr_ref, cr_ref)
    zl = shifted(zl_ref, ml_ref, cl_ref)
    zg = shifted(zg_ref, mg_ref, cg_ref)
    r = zr[:, 0:RWKV_WIDTH]
    k = zr[:, RWKV_WIDTH:2 * RWKV_WIDTH]
    v = zr[:, 2 * RWKV_WIDTH:3 * RWKV_WIDTH]

    xw = w0_ref[...] + _dot(jnp.tanh(zl[:, 0:DECAY_LORA]), w2_ref[...])
    neg = -xw
    softplus = jnp.maximum(neg, 0.0) + jnp.log(1.0 + jnp.exp(-jnp.abs(neg)))
    lw = -jnp.exp(-softplus - 0.5)
    a = jax.nn.sigmoid(a0_ref[...] + _dot(zl[:, DECAY_LORA:DECAY_LORA + ICLR_LORA], a2_ref[...]))
    gate = _dot(jax.nn.sigmoid(zg), g2_ref[...])

    cum = _dot_exact_lhs(tri_ref[...], lw)
    cum_end = cum.reshape(nck, WKV_CHUNK, RWKV_WIDTH)[:, WKV_CHUNK - 1:WKV_CHUNK, :]
    cum_end_b = jnp.broadcast_to(cum_end, (nck, WKV_CHUNK, RWKV_WIDTH)).reshape(tt, RWKV_WIDTH)
    e_incl = jnp.exp(cum)
    e_excl = jnp.exp(cum - lw)
    e_neg = jnp.exp(-cum)
    e_rev = jnp.exp(cum_end_b - cum)
    g_end = jnp.exp(cum_end)

    cw = WKV_CHUNK
    lane = lax.broadcasted_iota(jnp.int32, (1, 2 * n), 1)
    lo = lane < n
    ri = lax.broadcasted_iota(jnp.int32, (cw, 4 * n), 0)
    ci = lax.broadcasted_iota(jnp.int32, (cw, 4 * n), 1) % n
    strict = ri > ci
    incl = ri >= ci
    eye_row = (ri == ci).astype(F32)
    i2 = lax.broadcasted_iota(jnp.int32, (2 * n, 2 * n), 0)
    j2 = lax.broadcasted_iota(jnp.int32, (2 * n, 2 * n), 1)
    same_head = (i2 // n) == (j2 // n)
    eye_2n = (i2 == j2).astype(F32)
    zeros_slab = jnp.zeros((cw, 2 * n), BF16)

    def seg_sum(u):
        s_lo = jnp.sum(jnp.where(lo, u, 0.0), axis=-1, keepdims=True)
        s_hi = jnp.sum(jnp.where(lo, 0.0, u), axis=-1, keepdims=True)
        return jnp.where(lo, s_lo, s_hi)

    def block_diag(u):
        ua, ub = u[:, 0:2 * n], u[:, 2 * n:4 * n]
        parts = []
        for w, first in ((ua, True), (ub, False)):
            for keep_lo in (True, False):
                blk = jnp.where(lo, w, 0.0) if keep_lo else jnp.where(lo, 0.0, w)
                blk = blk.astype(BF16)
                parts.append(jnp.concatenate([blk, zeros_slab] if first else [zeros_slab, blk], axis=1))
        return jnp.concatenate(parts, axis=0)

    rows = lambda c: slice(c * cw, (c + 1) * cw)

    n_pairs = RWKV_HEADS // 2
    slab = lambda p: slice(2 * n * p, 2 * n * (p + 1))
    pre = []
    for p in range(n_pairs):
        sl = slab(p)
        rp, kp, vp, ap = r[:, sl], k[:, sl], v[:, sl], a[:, sl]
        kkh = kp * kk_ref[:, sl]
        kkn = kkh / jnp.maximum(jnp.sqrt(seg_sum(kkh * kkh)), 1e-12)
        kmod = kp * (1.0 + (ap - 1.0) * ka_ref[:, sl])
        bv = kkn * ap
        a_t = -kkn * e_excl[:, sl]
        r_t = rp * e_incl[:, sl]
        b_hat = bv * e_neg[:, sl]
        k_hat = kmod * e_neg[:, sl]
        mms = []
        for c in range(nck):
            bh, kh = b_hat[rows(c)], k_hat[rows(c)]
            rhs = jnp.concatenate([jnp.where(lo, bh, 0.0), jnp.where(lo, 0.0, bh),
                                   jnp.where(lo, kh, 0.0), jnp.where(lo, 0.0, kh)], axis=0)
            mms.append(_dot_nt(jnp.concatenate([a_t[rows(c)], r_t[rows(c)]], axis=0), rhs))
        pre.append(dict(r=rp, v=vp, kmod=kmod, a_t=a_t, r_t=r_t, b_end=bv * e_rev[:, sl],
                        k_end=kmod * e_rev[:, sl], mms=mms))

    chains = [(p, c0) for p in range(n_pairs) for c0 in range(0, nck, 2)]

    def pair_row(c0, f):
        return jnp.concatenate([f(c0), f(c0 + 1)], axis=1)

    quads = []
    for p, c0 in chains:
        mms = pre[p]["mms"]
        quads.append((
            jnp.where(strict, pair_row(c0, lambda c: mms[c][0:cw, 0:2 * n]), 0.0),
            jnp.where(strict, pair_row(c0, lambda c: mms[c][0:cw, 2 * n:4 * n]), 0.0),
            jnp.where(incl, pair_row(c0, lambda c: mms[c][cw:2 * cw, 0:2 * n]), 0.0),
            jnp.where(incl, pair_row(c0, lambda c: mms[c][cw:2 * cw, 2 * n:4 * n]), 0.0)))
    tinvs = [eye_row + q[0] for q in quads]
    xs = [_dot(q[0], block_diag(q[0])) for q in quads]
    for _ in range(int(math.log2(cw)) - 2):
        txs = [_dot(jnp.concatenate([t_, x_], axis=0), block_diag(x_)) for t_, x_ in zip(tinvs, xs)]
        tinvs = [t_ + tx[0:cw] for t_, tx in zip(tinvs, txs)]
        xs = [tx[cw:2 * cw] for tx in txs]
    tinvs = [t_ + _dot(t_, block_diag(x_)) for t_, x_ in zip(tinvs, xs)]
    lvs = [_dot(jnp.concatenate([q[1], q[3]], axis=0),
                block_diag(pair_row(c0, lambda c: pre[p]["v"][rows(c)])))
           for (p, c0), q in zip(chains, quads)]
    w_rows = [_dot(t_, block_diag(pair_row(c0, lambda c: pre[p]["a_t"][rows(c)])))
              for (p, c0), t_ in zip(chains, tinvs)]
    u_rows = [_dot(t_, block_diag(lv[0:cw])) for t_, lv in zip(tinvs, lvs)]
    qp_rows = [pair_row(c0, lambda c: pre[p]["r_t"][rows(c)]) + _dot(q[2], block_diag(w_))
               for (p, c0), q, w_ in zip(chains, quads, w_rows)]
    y0_rows = [_dot(q[2], block_diag(u_)) + lv[cw:2 * cw] for q, u_, lv in zip(quads, u_rows, lvs)]
    phs = {}
    for ci, (p, c0) in enumerate(chains):
        for j, c in enumerate((c0, c0 + 1)):
            ls = slice(2 * n * j, 2 * n * (j + 1))
            lhs = jnp.concatenate([pre[p]["b_end"][rows(c)], pre[p]["k_end"][rows(c)]], axis=0)
            rhs = jnp.concatenate(
                [jnp.concatenate([w_rows[ci][:, ls], u_rows[ci][:, ls]], axis=1),
                 jnp.concatenate([jnp.zeros((cw, 2 * n), F32), pre[p]["v"][rows(c)]], axis=1)], axis=0)
            ph = _dot_tn(lhs, rhs)
            pmat = eye_2n * g_end[c, :, slab(p)] + jnp.where(same_head, ph[:, 0:2 * n], 0.0)
            phs[(p, c)] = (jnp.concatenate([qp_rows[ci][:, ls], pmat], axis=0),
                           jnp.where(same_head, ph[:, 2 * n:4 * n], 0.0), y0_rows[ci][:, ls])
    hstates = [state_ref[p] for p in range(n_pairs)]
    ys = [[] for _ in range(n_pairs)]
    for c in range(nck):
        for p in range(n_pairs):
            qpm, hp, y0 = phs[(p, c)]
            qph = _dot(qpm, hstates[p])
            ys[p].append(qph[0:cw] + y0)
            hstates[p] = qph[cw:cw + 2 * n] + hp
    for p in range(n_pairs):
        sl = slab(p)
        state_ref[p] = hstates[p]
        y = jnp.concatenate(ys[p], axis=0)
        mu = seg_sum(y) * (1.0 / n)
        yc = y - mu
        var = seg_sum(yc * yc) * (1.0 / n)
        yn = yc * lax.rsqrt(var + GN_EPS) * lng_ref[:, sl] + lnb_ref[:, sl]
        bonus = seg_sum(pre[p]["r"] * pre[p]["kmod"] * rk_ref[:, sl]) * pre[p]["v"]
        out_ref[0, :, sl] = ((yn + bonus) * gate[:, sl]).astype(out_ref.dtype)


def _rwkv_mixer(zr, zl, zg, mix_r, mix_l, mix_g, w0, w2, a0, a2, g2p, k_k, k_a, r_k, ln_g, ln_b):
    bsz, seq, _ = zr.shape
    tt = WKV_TILE
    assert seq % tt == 0 and tt % (2 * WKV_CHUNK) == 0 and 2 * RWKV_HEAD_DIM == LANES
    idx = np.arange(tt)
    tri = ((idx[:, None] // WKV_CHUNK == idx[None, :] // WKV_CHUNK) & (idx[:, None] >= idx[None, :]))
    full = lambda shape: pl.BlockSpec(shape, lambda b, t: (0,) * len(shape))
    row = lambda a: a.reshape(1, -1)
    wl = DECAY_LORA + ICLR_LORA
    return pl.pallas_call(
        _rwkv_kernel,
        grid=(bsz, seq // tt),
        in_specs=[
            pl.BlockSpec((1, tt, 3 * RWKV_WIDTH), lambda b, t: (b, t, 0)),
            pl.BlockSpec((1, tt, wl), lambda b, t: (b, t, 0)),
            pl.BlockSpec((1, tt, GATE_LORA_PAD), lambda b, t: (b, t, 0)),
            full((1, 3 * RWKV_WIDTH)), full((1, wl)), full((1, GATE_LORA_PAD)),
            full((1, RWKV_WIDTH)), full((DECAY_LORA, RWKV_WIDTH)),
            full((1, RWKV_WIDTH)), full((ICLR_LORA, RWKV_WIDTH)),
            full((GATE_LORA_PAD, RWKV_WIDTH)),
            full((1, RWKV_WIDTH)), full((1, RWKV_WIDTH)), full((1, RWKV_WIDTH)),
            full((1, RWKV_WIDTH)), full((1, RWKV_WIDTH)),
            full((tt, tt)),
        ],
        out_specs=pl.BlockSpec((1, tt, RWKV_WIDTH), lambda b, t: (b, t, 0)),
        out_shape=jax.ShapeDtypeStruct((bsz, seq, RWKV_WIDTH), BF16),
        scratch_shapes=[
            pltpu.VMEM((RWKV_HEADS // 2, 2 * RWKV_HEAD_DIM, 2 * RWKV_HEAD_DIM), F32),
            pltpu.VMEM((1, 3 * RWKV_WIDTH), F32),
            pltpu.VMEM((1, wl), F32),
            pltpu.VMEM((1, GATE_LORA_PAD), F32),
        ],
        compiler_params=pltpu.CompilerParams(
            dimension_semantics=("arbitrary", "arbitrary"), vmem_limit_bytes=VMEM_LIMIT_BYTES),
        name="rwkv7_mixer",
    )(zr, zl, zg, row(mix_r), row(mix_l), row(mix_g), row(w0), w2.astype(BF16), row(a0), a2.astype(BF16),
      g2p.astype(BF16), row(k_k), row(k_a), row(r_k), row(ln_g), row(ln_b), jnp.asarray(tri, BF16))


def _ffn_kernel(x_ref, yn_ref, yr_ref, won_ref, wor_ref, gain_ref, wg_ref, wu_ref, wd_ref, o_ref, *, ff_chunk):
    h = x_ref[...]
    h = h + jnp.dot(yn_ref[...], won_ref[...], preferred_element_type=F32)
    h = h + jnp.dot(yr_ref[...], wor_ref[...], preferred_element_type=F32)
    ms = jnp.mean(h * h, axis=-1, keepdims=True)
    hn = (h * lax.rsqrt(ms + RMS_EPS) * gain_ref[...]).astype(BF16)
    d_ff = wg_ref.shape[1]
    acc = None
    for c in range(d_ff // ff_chunk):
        cs = slice(c * ff_chunk, (c + 1) * ff_chunk)
        gte = jnp.dot(hn, wg_ref[:, cs], preferred_element_type=F32)
        up = jnp.dot(hn, wu_ref[:, cs], preferred_element_type=F32)
        act = (gte * jax.nn.sigmoid(gte) * up).astype(BF16)
        down = jnp.dot(act, wd_ref[cs, :], preferred_element_type=F32)
        acc = down if acc is None else acc + down
    o_ref[...] = h + acc


def _out_ffn(x2, y_nsa, y_rwkv, w_out, ffn_gain, w_gate, w_up, w_down):
    n, d = x2.shape
    d_ff = w_gate.shape[1]
    tm = min(ROW_TILE // 2, n)
    ff_chunk = d_ff // 2 if (d_ff // 2) % LANES == 0 else d_ff
    const = lambda shape: pl.BlockSpec(shape, lambda i: (0, 0))
    return pl.pallas_call(
        functools.partial(_ffn_kernel, ff_chunk=ff_chunk),
        grid=(n // tm,),
        in_specs=[
            pl.BlockSpec((tm, d), lambda i: (i, 0)),
            pl.BlockSpec((tm, NSA_WIDTH), lambda i: (i, 0)),
            pl.BlockSpec((tm, RWKV_WIDTH), lambda i: (i, 0)),
            const((NSA_WIDTH, d)), const((RWKV_WIDTH, d)), const((1, d)),
            const((d, d_ff)), const((d, d_ff)), const((d_ff, d)),
        ],
        out_specs=pl.BlockSpec((tm, d), lambda i: (i, 0)),
        out_shape=jax.ShapeDtypeStruct((n, d), F32),
        compiler_params=pltpu.CompilerParams(
            dimension_semantics=("arbitrary",), vmem_limit_bytes=VMEM_LIMIT_BYTES),
        name="out_proj_ffn",
    )(x2, y_nsa, y_rwkv, w_out[:NSA_WIDTH].astype(BF16), w_out[NSA_WIDTH:].astype(BF16),
      ffn_gain.reshape(1, d), w_gate.astype(BF16), w_up.astype(BF16), w_down.astype(BF16))


def _layer(h, attn_gain, w_in, q_gain, k_gain, cmp_pe, cmp_w1, cmp_b1, cmp_w2, cmp_b2, out_gain, shift_mix,
           w0, w2, a0, a2, g2, k_k, k_a, r_k, ln_g, ln_b, w_out, ffn_gain, w_gate, w_up, w_down, rel_table):
    bsz, seq, d = h.shape
    nsa_cols = NSA_WIDTH + 6 * KV_WIDTH + 3 * NSA_HEADS
    gate0 = NSA_WIDTH + 6 * KV_WIDTH

    w_nsa, w_rw = w_in[:, :nsa_cols], w_in[:, nsa_cols:]
    gate_src = np.array([gate0 + h_ * 3 + br for br in range(3) for h_ in range(NSA_HEADS)])
    w_gates = jnp.zeros((d, LANES), F32).at[:, :3 * NSA_HEADS].set(w_nsa[:, gate_src])
    lora0 = 3 * RWKV_WIDTH
    g0 = lora0 + DECAY_LORA + ICLR_LORA
    w_g = jnp.zeros((d, GATE_LORA_PAD), F32).at[:, :GATE_LORA].set(w_rw[:, g0:g0 + GATE_LORA])
    widths = (NSA_WIDTH, 6 * KV_WIDTH, LANES, 3 * RWKV_WIDTH, DECAY_LORA + ICLR_LORA, GATE_LORA_PAD)
    w_cat = jnp.concatenate([w_nsa[:, :gate0], w_gates, w_rw[:, :g0], w_g], axis=1).astype(BF16)
    mix_g = jnp.zeros((GATE_LORA_PAD,), F32).at[:GATE_LORA].set(shift_mix[g0:g0 + GATE_LORA])
    g2p = jnp.zeros((GATE_LORA_PAD, RWKV_WIDTH), F32).at[:GATE_LORA].set(g2)

    x2 = h.reshape(bsz * seq, d)
    q, kv, gates, zr, zl, zg = _in_projection(x2, attn_gain.reshape(1, d), w_cat, widths)
    r3 = lambda u: u.reshape(bsz, seq, u.shape[-1])
    y_nsa = _nsa_mixer(r3(q), r3(kv), r3(gates), q_gain, k_gain, cmp_pe, cmp_w1, cmp_b1, cmp_w2, cmp_b2,
                       out_gain, rel_table)
    y_rwkv = _rwkv_mixer(r3(zr), r3(zl), r3(zg), shift_mix[:lora0], shift_mix[lora0:g0], mix_g,
                         w0, w2, a0, a2, g2p, k_k, k_a, r_k.reshape(-1), ln_g, ln_b)
    out = _out_ffn(x2, y_nsa.reshape(bsz * seq, NSA_WIDTH), y_rwkv.reshape(bsz * seq, RWKV_WIDTH),
                   w_out, ffn_gain, w_gate, w_up, w_down)
    return out.reshape(bsz, seq, d)


def kernel(x, attn_norm_gain, w_in, nsa_q_gain, nsa_k_gain, cmp_pe, cmp_w1, cmp_b1, cmp_w2, cmp_b2, nsa_out_gain, rwkv_shift_mix, rwkv_w0, rwkv_w2, rwkv_a0, rwkv_a2, rwkv_g2, rwkv_k_k, rwkv_k_a, rwkv_r_k, rwkv_ln_gain, rwkv_ln_bias, w_out, ffn_norm_gain, w_gate, w_up, w_down, rel_bias_table):
    h = x
    for l in range(attn_norm_gain.shape[0]):
        h = _layer(h, attn_norm_gain[l], w_in[l], nsa_q_gain[l], nsa_k_gain[l], cmp_pe[l], cmp_w1[l], cmp_b1[l],
                   cmp_w2[l], cmp_b2[l], nsa_out_gain[l], rwkv_shift_mix[l], rwkv_w0[l], rwkv_w2[l], rwkv_a0[l],
                   rwkv_a2[l], rwkv_g2[l], rwkv_k_k[l], rwkv_k_a[l], rwkv_r_k[l], rwkv_ln_gain[l],
                   rwkv_ln_bias[l], w_out[l], ffn_norm_gain[l], w_gate[l], w_up[l], w_down[l], rel_bias_table)
    return h
```

```python
import functools
import math

import numpy as np
import jax
import jax.numpy as jnp
from jax import lax
from jax.experimental import pallas as pl
from jax.experimental.pallas import tpu as pltpu

HEAD_DIM = 64
NSA_HEADS = 8
NSA_KV_HEADS = 2
NSA_HPG = NSA_HEADS // NSA_KV_HEADS
NSA_WIDTH = NSA_HEADS * HEAD_DIM
KV_WIDTH = NSA_KV_HEADS * HEAD_DIM
CMP_BLOCK = 32
CMP_STRIDE = 16
CMP_HIDDEN = 256
SEL_BLOCK = 64
SEL_TOP_N = 16
SEL_LOCAL = 2
WINDOW = 512
ATTN_SCALE = HEAD_DIM ** -0.5
LOG2E = math.log2(math.e)
NEG_INF = -1e30
FORCE_SCORE = 1e9
REL_BUCKETS = 32
REL_MAX_DIST = 128
RWKV_HEADS = 8
RWKV_HEAD_DIM = 64
RWKV_WIDTH = RWKV_HEADS * RWKV_HEAD_DIM
DECAY_LORA = 64
ICLR_LORA = 64
GATE_LORA = 160
GATE_LORA_PAD = 256
GN_EPS = 64e-5
RMS_EPS = 1e-6

LANES = 128
Q_TILE = 128
TILES_PER_STEP = 2
V_ROWS = HEAD_DIM + 16
K_TILE = 128
WKV_TILE = 256
WKV_CHUNK = 64
ROW_TILE = 512
FF_CHUNK = 256
VMEM_LIMIT_BYTES = 56 * 1024 * 1024

F32 = jnp.float32
BF16 = jnp.bfloat16


def _dot(a, b):
    return jnp.dot(a.astype(BF16), b.astype(BF16), preferred_element_type=F32)


def _dot_nt(a, b):
    return lax.dot_general(a.astype(BF16), b.astype(BF16), (((1,), (1,)), ((), ())),
                           preferred_element_type=F32)


def _dot_tn(a, b):
    return lax.dot_general(a.astype(BF16), b.astype(BF16), (((0,), (0,)), ((), ())),
                           preferred_element_type=F32)


def _split3(x):
    hi = x.astype(BF16)
    r1 = x - hi.astype(F32)
    mid = r1.astype(BF16)
    lo = (r1 - mid.astype(F32)).astype(BF16)
    return hi, mid, lo


def _dot_exact_rhs(x, m_bf16):
    hi, mid, lo = _split3(x)
    acc = jnp.dot(lo, m_bf16, preferred_element_type=F32)
    acc = acc + jnp.dot(mid, m_bf16, preferred_element_type=F32)
    return acc + jnp.dot(hi, m_bf16, preferred_element_type=F32)


def _dot_exact_lhs(m_bf16, x):
    hi, mid, lo = _split3(x)
    acc = jnp.dot(m_bf16, lo, preferred_element_type=F32)
    acc = acc + jnp.dot(m_bf16, mid, preferred_element_type=F32)
    return acc + jnp.dot(m_bf16, hi, preferred_element_type=F32)


def _rel_bucket_np(dist):
    max_exact = REL_BUCKETS // 2
    d = np.maximum(dist, 0)
    ratio = np.maximum(d, 1).astype(np.float32) / np.float32(max_exact)
    log_ratio = np.log(ratio).astype(np.float32) / np.float32(math.log(REL_MAX_DIST / max_exact))
    large = np.minimum(max_exact + (log_ratio * np.float32(REL_BUCKETS - max_exact)).astype(np.int32),
                       REL_BUCKETS - 1)
    return np.where(d < max_exact, d, large).astype(np.int32)


def _inproj_kernel(x_ref, gain_ref, w_ref, *out_refs, widths):
    x = x_ref[...]
    ms = jnp.mean(x * x, axis=-1, keepdims=True)
    xn = (x * lax.rsqrt(ms + RMS_EPS) * gain_ref[...]).astype(BF16)
    off = 0
    for o_ref, w in zip(out_refs, widths):
        o_ref[...] = jnp.dot(xn, w_ref[:, off:off + w], preferred_element_type=F32)
        off += w


def _in_projection(x2, gain, w_cat, widths):
    n, d = x2.shape
    tm = min(ROW_TILE, n)
    total = sum(widths)
    return pl.pallas_call(
        functools.partial(_inproj_kernel, widths=widths),
        grid=(n // tm,),
        in_specs=[
            pl.BlockSpec((tm, d), lambda i: (i, 0)),
            pl.BlockSpec((1, d), lambda i: (0, 0)),
            pl.BlockSpec((d, total), lambda i: (0, 0)),
        ],
        out_specs=[pl.BlockSpec((tm, w), lambda i: (i, 0)) for w in widths],
        out_shape=[jax.ShapeDtypeStruct((n, w), F32) for w in widths],
        compiler_params=pltpu.CompilerParams(
            dimension_semantics=("arbitrary",), vmem_limit_bytes=VMEM_LIMIT_BYTES),
        name="in_projection",
    )(x2, gain, w_cat)


def _head_rms(u, gain):
    ms = jnp.mean(u * u, axis=-1, keepdims=True)
    return u * lax.rsqrt(ms + RMS_EPS) * gain


def _nsa_kernel(tab_ref,
                q_ref, kv_ref, gate_ref,
                qg_ref, kg_ref, pe_ref, w1_ref, b1_ref, w2_ref, b2_ref, og_ref,
                bk0_ref, bk1_ref, bkc_ref, ovl_ref,
                out_ref,
                kvpad_ref, kcmp_ref, vcmp_ref, ksel_ref, vsel_ref, kwin_ref, vwin_ref,
                btab_ref, ccan_ref, ybuf_ref,
                *, seq):
    b = pl.program_id(0)
    step = pl.program_id(1)
    n_cb = seq // CMP_STRIDE
    n_c = (seq - CMP_BLOCK) // CMP_STRIDE + 1
    n_sel = seq // SEL_BLOCK
    cols = NSA_HPG * Q_TILE

    @pl.when((b == 0) & (step == 0))
    def _build_bias_tables():
        bk0 = bk0_ref[...]
        bk1 = bk1_ref[...]
        bkc = bkc_ref[...]
        kk = lax.broadcasted_iota(jnp.int32, (K_TILE, Q_TILE), 0)
        qq = lax.broadcasted_iota(jnp.int32, (K_TILE, Q_TILE), 1)
        causal = jnp.where(kk <= qq, 0.0, NEG_INF).astype(F32)
        anti = jnp.where(kk > qq, 0.0, NEG_INF).astype(F32)
        for h in range(NSA_HEADS):
            t0 = jnp.zeros((K_TILE, Q_TILE), F32)
            t1 = jnp.zeros((K_TILE, Q_TILE), F32)
            tc = jnp.zeros((2 * LANES, Q_TILE), F32)
            for k in range(REL_BUCKETS):
                val = tab_ref[k, h]
                t0 = jnp.where(bk0 == k, val, t0)
                t1 = jnp.where(bk1 == k, val, t1)
                tc = jnp.where(bkc == k, val, tc)
            far = tab_ref[REL_BUCKETS - 1, h]
            g, p = divmod(h, NSA_HPG)
            cs = slice(p * Q_TILE, (p + 1) * Q_TILE)
            btab_ref[g, 0, :, cs] = jnp.zeros((K_TILE, Q_TILE), F32)
            btab_ref[g, 1, :, cs] = (t1 - far) * LOG2E
            btab_ref[g, 2, :, cs] = (t0 - far) * LOG2E + causal
            btab_ref[g, 3, :, cs] = anti
            btab_ref[g, 4, :, cs] = jnp.full((K_TILE, Q_TILE), NEG_INF, F32)
            ccan_ref[h] = tc * LOG2E
        blk = lax.broadcasted_iota(jnp.int32, (seq, HEAD_DIM), 0) // SEL_BLOCK
        col = lax.broadcasted_iota(jnp.int32, (seq, HEAD_DIM), 1)
        onehot = (blk == col).astype(BF16)
        padcol = (lax.broadcasted_iota(jnp.int32, (WINDOW, 2 * HEAD_DIM), 1) == HEAD_DIM).astype(BF16)
        for g in range(NSA_KV_HEADS):
            ksel_ref[g, :, HEAD_DIM:2 * HEAD_DIM] = onehot
            kwin_ref[g, 0:WINDOW, :] = padcol
            kwin_ref[g, WINDOW:WINDOW + seq, HEAD_DIM:2 * HEAD_DIM] = jnp.zeros((seq, HEAD_DIM), BF16)
            vwin_ref[g, 0:HEAD_DIM, 0:WINDOW] = jnp.zeros((HEAD_DIM, WINDOW), BF16)
            for vt_ref, width in ((vsel_ref, seq), (vwin_ref, WINDOW + seq)):
                ones_row = lax.broadcasted_iota(jnp.int32, (V_ROWS - HEAD_DIM, width), 0) == 0
                vt_ref[g, HEAD_DIM:V_ROWS, :] = ones_row.astype(BF16)

    @pl.when(step == 0)
    def _per_batch():
        for which in range(2):
            kvpad_ref[which, seq:seq + 2 * CMP_STRIDE, :] = jnp.zeros((2 * CMP_STRIDE, KV_WIDTH), F32)
            kvpad_ref[which, 0:seq, :] = kv_ref[0, :, which * KV_WIDTH:(which + 1) * KV_WIDTH]
            first = lax.broadcasted_iota(jnp.int32, (1, KV_WIDTH), 1) < HEAD_DIM
            slabs = [[], []]
            for l in range(0, CMP_BLOCK, 2):
                ra = kvpad_ref[which, pl.ds(l, n_cb, stride=CMP_STRIDE), :]
                rb = kvpad_ref[which, pl.ds(l + 1, n_cb, stride=CMP_STRIDE), :]
                slabs[0].append(jnp.where(first, ra, pltpu.roll(rb, HEAD_DIM, 1)))
                slabs[1].append(jnp.where(first, pltpu.roll(ra, HEAD_DIM, 1), rb))
            cmps = []
            for g in range(NSA_KV_HEADS):
                blk = jnp.concatenate(slabs[g], axis=1) + pe_ref[which]
                hid = jax.nn.gelu(_dot(blk, w1_ref[which]) + b1_ref[which], approximate=True)
                cmps.append(_dot(hid, w2_ref[which]) + b2_ref[which])
            if which == 0:
                for g in range(NSA_KV_HEADS):
                    kcmp_ref[g] = _head_rms(cmps[g], kg_ref[0:1, :]).astype(BF16)
            else:
                vct = jnp.concatenate(cmps, axis=1).T
                for g in range(NSA_KV_HEADS):
                    vcmp_ref[g] = vct[g * HEAD_DIM:(g + 1) * HEAD_DIM].astype(BF16)
        kv = kv_ref[0]
        for g in range(NSA_KV_HEADS):
            def col(i):
                c = (2 + i) * KV_WIDTH + g * HEAD_DIM
                return kv[:, c:c + HEAD_DIM]
            ksel_ref[g, :, 0:HEAD_DIM] = _head_rms(col(0), kg_ref[1:2, :]).astype(BF16)
            kwin_ref[g, WINDOW:WINDOW + seq, 0:HEAD_DIM] = _head_rms(col(2), kg_ref[2:3, :]).astype(BF16)
        for i, vt_ref, pad in ((1, vsel_ref, 0), (3, vwin_ref, WINDOW)):
            vt = kv[:, (2 + i) * KV_WIDTH:(3 + i) * KV_WIDTH].T
            for g in range(NSA_KV_HEADS):
                vt_ref[g, 0:HEAD_DIM, pad:pad + seq] = vt[g * HEAD_DIM:(g + 1) * HEAD_DIM].astype(BF16)

    n_selp = -(-n_sel // 8) * 8
    qT_all = q_ref[0].T
    gates_all = jax.nn.sigmoid(gate_ref[0].T)
    tiles = range(TILES_PER_STEP)
    groups = range(NSA_KV_HEADS)
    units = [(j, g) for j in tiles for g in groups]
    each = lambda f: {u: f(*u) for u in units}
    qts = [TILES_PER_STEP * step + j for j in tiles]
    tqs = [qts[j] * Q_TILE + lax.broadcasted_iota(jnp.int32, (1, Q_TILE), 1) for j in tiles]
    lanes = lambda j: slice(j * Q_TILE, (j + 1) * Q_TILE)

    def q_group(j, g):
        qs = []
        for p in range(NSA_HPG):
            h = g * NSA_HPG + p
            qh = qT_all[h * HEAD_DIM:(h + 1) * HEAD_DIM, lanes(j)]
            ms = jnp.mean(qh * qh, axis=0, keepdims=True)
            qs.append((qh * lax.rsqrt(ms + RMS_EPS) * qg_ref[...] * (ATTN_SCALE * LOG2E)).astype(BF16))
        return jnp.concatenate(qs, axis=1)

    q_cats = each(q_group)
    aug_row = lax.broadcasted_iota(jnp.int32, (HEAD_DIM, cols), 0)

    cc = lax.broadcasted_iota(jnp.int32, (n_cb, Q_TILE), 0)
    valid4 = [jnp.concatenate([(cc * CMP_STRIDE + (CMP_BLOCK - 1) <= tqs[j]) & (cc < n_c)] * NSA_HPG, axis=1)
              for j in tiles]
    c_starts = [pl.multiple_of(LANES - (CMP_STRIDE // 2) * qts[j], 8) for j in tiles]
    ss = each(lambda j, g: jnp.dot(kcmp_ref[g], q_cats[j, g], preferred_element_type=F32))
    ss = each(lambda j, g: jnp.where(valid4[j], ss[j, g] + jnp.concatenate(
        [ccan_ref[g * NSA_HPG + p, pl.ds(c_starts[j], n_cb), :] for p in range(NSA_HPG)], axis=1), NEG_INF))
    es = each(lambda j, g: jnp.exp2(ss[j, g] - jnp.max(ss[j, g], axis=0, keepdims=True)))
    pcs = each(lambda j, g: es[j, g] * (1.0 / jnp.sum(es[j, g], axis=0, keepdims=True))
               * valid4[j].astype(F32))
    o_cmps = each(lambda j, g: jnp.dot(vcmp_ref[g], pcs[j, g].astype(BF16), preferred_element_type=F32))
    psums = each(lambda j, g: functools.reduce(
        lambda u, w: u + w, [pcs[j, g][:, p * Q_TILE:(p + 1) * Q_TILE] for p in range(NSA_HPG)]))

    n_wb = WINDOW // K_TILE
    band = WINDOW + K_TILE
    band_kinds = (3,) + (0,) * (n_wb - 2) + (1, 2)
    w0s = [pl.multiple_of(qts[j] * K_TILE, K_TILE) for j in tiles]
    pad_rows = jnp.where(aug_row == 0, NEG_INF, 0.0).astype(BF16)

    def band_logits(g, s):
        blocks = []
        for i, kind in enumerate(band_kinds):
            si = s[i * K_TILE:(i + 1) * K_TILE]
            blocks.append(si if kind == 0 else si + btab_ref[g, kind])
        return jnp.concatenate(blocks, axis=0)

    sw = each(lambda j, g: jnp.dot(kwin_ref[g, pl.ds(w0s[j], band), :],
                                   jnp.concatenate([q_cats[j, g], pad_rows], axis=0),
                                   preferred_element_type=F32))
    sw = each(lambda j, g: band_logits(g, sw[j, g]))
    pws = each(lambda j, g: jnp.exp2(sw[j, g] - jnp.max(sw[j, g], axis=0, keepdims=True)))
    o_wins = each(lambda j, g: jnp.dot(vwin_ref[g, :, pl.ds(w0s[j], band)], pws[j, g].astype(BF16),
                                       preferred_element_type=F32))
    o_wins = each(lambda j, g: o_wins[j, g][0:HEAD_DIM] * (1.0 / o_wins[j, g][HEAD_DIM:HEAD_DIM + 1]))

    imps = each(lambda j, g: _dot_exact_lhs(ovl_ref[...], psums[j, g])[0:n_selp])
    jb = lax.broadcasted_iota(jnp.int32, (n_selp, Q_TILE), 0)

    def masked_score(j, g):
        qb = tqs[j] // SEL_BLOCK
        forced = (jb == 0) | ((jb <= qb) & (jb > qb - SEL_LOCAL))
        score = jnp.where(forced, FORCE_SCORE, imps[j, g])
        score = jnp.where(jb <= qb, score, -1.0)
        return jnp.where(jb < n_sel, score, -2.0)

    scores = each(masked_score)
    ranks = each(lambda j, g: jnp.zeros((n_selp, Q_TILE), F32))
    for i in range(n_sel):
        for u in units:
            si = scores[u][i:i + 1, :]
            ahead = (si > scores[u]) | ((si == scores[u]) & (jb > i))
            ranks[u] = ranks[u] + ahead.astype(F32)

    def q_with_mask_rows(j, g):
        sel = (ranks[j, g] < float(min(SEL_TOP_N, n_sel))) & (jb < n_sel)
        nm = (sel.astype(F32) - 1.0) * (-NEG_INF)
        return jnp.concatenate([q_cats[j, g], jnp.concatenate([nm] * NSA_HPG, axis=1).astype(BF16),
                                jnp.zeros((HEAD_DIM - n_selp, cols), BF16)], axis=0)

    q_sels = each(q_with_mask_rows)

    pair = 2 * K_TILE

    def sel_step(kp, carry, near):
        k0 = pl.multiple_of(kp * pair, pair)

        def logits(j, g):
            s = jnp.dot(ksel_ref[g, pl.ds(k0, pair), :], q_sels[j, g], preferred_element_type=F32)
            if near:
                rel = qts[j] - 2 * kp
                kind_a = jnp.where(rel >= 2, 0, jnp.where(rel == 1, 1, 2))
                kind_b = jnp.where(rel >= 3, 0, jnp.where(rel == 2, 1, jnp.where(rel == 1, 2, 4)))
                s = s + jnp.concatenate([btab_ref[g, kind_a], btab_ref[g, kind_b]], axis=0)
            return s

        old = {u: carry[i] for i, u in enumerate(units)}
        ss = each(logits)
        m_new = each(lambda j, g: jnp.maximum(old[j, g][0], jnp.max(ss[j, g], axis=0, keepdims=True)))
        ps = each(lambda j, g: jnp.exp2(ss[j, g] - m_new[j, g]))
        alpha = each(lambda j, g: jnp.exp2(old[j, g][0] - m_new[j, g]))
        pv = each(lambda j, g: jnp.dot(vsel_ref[g, :, pl.ds(k0, pair)], ps[j, g].astype(BF16),
                                       preferred_element_type=F32))
        return tuple((m_new[u], alpha[u] * old[u][1] + pv[u]) for u in units)

    init = (jnp.full((1, cols), NEG_INF, F32), jnp.zeros((V_ROWS, cols), F32))
    n_pairs = qts[0] // 2 + 1
    n_far = jnp.maximum((qts[0] - 1) // 2, 0)
    sel_mid = lax.fori_loop(0, n_far, functools.partial(sel_step, near=False), tuple(init for _ in units))
    sel_out = lax.fori_loop(n_far, n_pairs, functools.partial(sel_step, near=True), sel_mid)

    ssqs = []
    for j in tiles:
        gates = gates_all[:, lanes(j)]
        ssq = jnp.zeros((1, Q_TILE), F32)
        for g in groups:
            _, acc_s = sel_out[units.index((j, g))]
            o_sel = acc_s[0:HEAD_DIM] * (1.0 / acc_s[HEAD_DIM:HEAD_DIM + 1])
            for p in range(NSA_HPG):
                h = g * NSA_HPG + p
                cs = slice(p * Q_TILE, (p + 1) * Q_TILE)
                o = (gates[h:h + 1] * o_cmps[j, g][:, cs]
                     + gates[NSA_HEADS + h:NSA_HEADS + h + 1] * o_sel[:, cs]
                     + gates[2 * NSA_HEADS + h:2 * NSA_HEADS + h + 1] * o_wins[j, g][:, cs])
                ssq = ssq + jnp.sum(o * o, axis=0, keepdims=True)
                ybuf_ref[h * HEAD_DIM:(h + 1) * HEAD_DIM, lanes(j)] = o
        ssqs.append(ssq)
    ssq_all = jnp.concatenate(ssqs, axis=1)
    yT = ybuf_ref[...] * lax.rsqrt(ssq_all / NSA_WIDTH + RMS_EPS) * og_ref[...]
    out_ref[0] = yT.T.astype(out_ref.dtype)


def _nsa_mixer(q, kv, gates, q_gain, k_gain, cmp_pe, cmp_w1, cmp_b1, cmp_w2, cmp_b2, out_gain, rel_table):
    bsz, seq, _ = q.shape
    assert TILES_PER_STEP == 2 and seq % (TILES_PER_STEP * Q_TILE) == 0
    assert seq // CMP_STRIDE == LANES and seq // SEL_BLOCK <= HEAD_DIM
    n_steps = seq // (TILES_PER_STEP * Q_TILE)
    q_rows = TILES_PER_STEP * Q_TILE
    n_cb = seq // CMP_STRIDE
    n_sel = seq // SEL_BLOCK
    n_selp = -(-n_sel // 8) * 8
    cols = NSA_HPG * Q_TILE

    kk = np.arange(K_TILE)[:, None]
    qq = np.arange(Q_TILE)[None, :]
    bk0 = _rel_bucket_np(qq - kk)
    bk1 = _rel_bucket_np(K_TILE + qq - kk)
    cprime = np.arange(2 * LANES)[:, None] - LANES
    bkc = _rel_bucket_np(qq - (CMP_BLOCK - 1) - CMP_STRIDE * cprime)
    c_start = np.arange(n_cb)[None, :] * CMP_STRIDE
    j_start = np.arange(LANES)[:, None] * SEL_BLOCK
    ovl = np.clip(np.minimum(c_start + CMP_BLOCK, j_start + SEL_BLOCK) - np.maximum(c_start, j_start), 0, None)
    ovl = (ovl.astype(np.float32) / CMP_STRIDE)
    ovl[n_sel:, :] = 0.0

    assert NSA_KV_HEADS == 2 and KV_WIDTH == LANES
    w1 = cmp_w1.astype(BF16)
    full = lambda shape: pl.BlockSpec(shape, lambda b, t: (0,) * len(shape))
    kernel = functools.partial(_nsa_kernel, seq=seq)
    return pl.pallas_call(
        kernel,
        grid=(bsz, n_steps),
        in_specs=[
            pl.BlockSpec(memory_space=pltpu.SMEM),
            pl.BlockSpec((1, q_rows, NSA_WIDTH), lambda b, t: (b, t, 0)),
            pl.BlockSpec((1, seq, 6 * KV_WIDTH), lambda b, t: (b, 0, 0)),
            pl.BlockSpec((1, q_rows, LANES), lambda b, t: (b, t, 0)),
            full((HEAD_DIM, Q_TILE)),
            full((3, HEAD_DIM)),
            full((2, 1, CMP_BLOCK * HEAD_DIM)),
            full((2, CMP_BLOCK * HEAD_DIM, CMP_HIDDEN)),
            full((2, 1, CMP_HIDDEN)),
            full((2, CMP_HIDDEN, HEAD_DIM)),
            full((2, 1, HEAD_DIM)),
            full((NSA_WIDTH, q_rows)),
            full((K_TILE, Q_TILE)),
            full((K_TILE, Q_TILE)),
            full((2 * LANES, Q_TILE)),
            full((LANES, n_cb)),
        ],
        out_specs=pl.BlockSpec((1, q_rows, NSA_WIDTH), lambda b, t: (b, t, 0)),
        out_shape=jax.ShapeDtypeStruct((bsz, seq, NSA_WIDTH), BF16),
        scratch_shapes=[
            pltpu.VMEM((2, seq + 2 * CMP_STRIDE, KV_WIDTH), F32),
            pltpu.VMEM((NSA_KV_HEADS, n_cb, HEAD_DIM), BF16),
            pltpu.VMEM((NSA_KV_HEADS, HEAD_DIM, n_cb), BF16),
            pltpu.VMEM((NSA_KV_HEADS, seq, 2 * HEAD_DIM), BF16),
            pltpu.VMEM((NSA_KV_HEADS, V_ROWS, seq), BF16),
            pltpu.VMEM((NSA_KV_HEADS, WINDOW + seq, 2 * HEAD_DIM), BF16),
            pltpu.VMEM((NSA_KV_HEADS, V_ROWS, WINDOW + seq), BF16),
            pltpu.VMEM((NSA_KV_HEADS, 5, K_TILE, cols), F32),
            pltpu.VMEM((NSA_HEADS, 2 * LANES, Q_TILE), F32),
            pltpu.VMEM((NSA_WIDTH, q_rows), F32),
        ],
        compiler_params=pltpu.CompilerParams(
            dimension_semantics=("arbitrary", "arbitrary"), vmem_limit_bytes=VMEM_LIMIT_BYTES),
        name="nsa_mixer",
    )(rel_table, q, kv, gates, jnp.broadcast_to(q_gain[:, None], (HEAD_DIM, Q_TILE)), k_gain,
      cmp_pe.reshape(2, 1, CMP_BLOCK * HEAD_DIM), w1,
      cmp_b1.reshape(2, 1, CMP_HIDDEN), cmp_w2.astype(BF16), cmp_b2.reshape(2, 1, HEAD_DIM),
      jnp.broadcast_to(out_gain[:, None], (NSA_WIDTH, q_rows)), jnp.asarray(bk0), jnp.asarray(bk1),
      jnp.asarray(bkc), jnp.asarray(ovl, BF16))


def _rwkv_kernel(zr_ref, zl_ref, zg_ref, mr_ref, ml_ref, mg_ref, w0_ref, w2_ref, a0_ref, a2_ref, g2_ref,
                 kk_ref, ka_ref, rk_ref, lng_ref, lnb_ref, tri_ref,
                 out_ref,
                 state_ref, cr_ref, cl_ref, cg_ref):
    t = pl.program_id(1)
    tt = WKV_TILE
    n = RWKV_HEAD_DIM
    nck = tt // WKV_CHUNK

    @pl.when(t == 0)
    def _reset():
        state_ref[...] = jnp.zeros_like(state_ref)
        cr_ref[...] = jnp.zeros_like(cr_ref)
        cl_ref[...] = jnp.zeros_like(cl_ref)
        cg_ref[...] = jnp.zeros_like(cg_ref)

    row = lax.broadcasted_iota(jnp.int32, (tt, 1), 0)

    def shifted(z_ref, mix_ref, carry_ref):
        z = z_ref[0]
        prev = jnp.where(row == 0, carry_ref[...], pltpu.roll(z, 1, 0))
        carry_ref[...] = z_ref[0, tt - 1:tt, :]
        return z + mix_ref[...] * (prev - z)

    zr = shifted(zr_ref, mr_ref, cr_ref)
    zl = shifted(zl_ref, ml_ref, cl_ref)
    zg = shifted(zg_ref, mg_ref, cg_ref)
    r = zr[:, 0:RWKV_WIDTH]
    k = zr[:, RWKV_WIDTH:2 * RWKV_WIDTH]
    v = zr[:, 2 * RWKV_WIDTH:3 * RWKV_WIDTH]

    xw = w0_ref[...] + _dot(jnp.tanh(zl[:, 0:DECAY_LORA]), w2_ref[...])
    neg = -xw
    softplus = jnp.maximum(neg, 0.0) + jnp.log(1.0 + jnp.exp(-jnp.abs(neg)))
    lw = -jnp.exp(-softplus - 0.5)
    a = jax.nn.sigmoid(a0_ref[...] + _dot(zl[:, DECAY_LORA:DECAY_LORA + ICLR_LORA], a2_ref[...]))
    gate = _dot(jax.nn.sigmoid(zg), g2_ref[...])

    cum = _dot_exact_lhs(tri_ref[...], lw)
    cum_end = cum.reshape(nck, WKV_CHUNK, RWKV_WIDTH)[:, WKV_CHUNK - 1:WKV_CHUNK, :]
    cum_end_b = jnp.broadcast_to(cum_end, (nck, WKV_CHUNK, RWKV_WIDTH)).reshape(tt, RWKV_WIDTH)
    e_incl = jnp.exp(cum)
    e_excl = jnp.exp(cum - lw)
    e_neg = jnp.exp(-cum)
    e_rev = jnp.exp(cum_end_b - cum)
    g_end = jnp.exp(cum_end)

    cw = WKV_CHUNK
    lane = lax.broadcasted_iota(jnp.int32, (1, 2 * n), 1)
    lo = lane < n
    ri = lax.broadcasted_iota(jnp.int32, (cw, 4 * n), 0)
    ci = lax.broadcasted_iota(jnp.int32, (cw, 4 * n), 1) % n
    strict = ri > ci
    incl = ri >= ci
    eye_row = (ri == ci).astype(F32)
    i2 = lax.broadcasted_iota(jnp.int32, (2 * n, 2 * n), 0)
    j2 = lax.broadcasted_iota(jnp.int32, (2 * n, 2 * n), 1)
    same_head = (i2 // n) == (j2 // n)
    eye_2n = (i2 == j2).astype(F32)
    zeros_slab = jnp.zeros((cw, 2 * n), BF16)

    def seg_sum(u, f=lambda s: s):
        s_lo = jnp.sum(jnp.where(lo, u, 0.0), axis=-1, keepdims=True)
        s_hi = jnp.sum(jnp.where(lo, 0.0, u), axis=-1, keepdims=True)
        return jnp.where(lo, f(s_lo), f(s_hi))

    def block_diag(u):
        ua, ub = u[:, 0:2 * n], u[:, 2 * n:4 * n]
        parts = []
        for w, first in ((ua, True), (ub, False)):
            for keep_lo in (True, False):
                blk = jnp.where(lo, w, 0.0) if keep_lo else jnp.where(lo, 0.0, w)
                blk = blk.astype(BF16)
                parts.append(jnp.concatenate([blk, zeros_slab] if first else [zeros_slab, blk], axis=1))
        return jnp.concatenate(parts, axis=0)

    rows = lambda c: slice(c * cw, (c + 1) * cw)

    n_pairs = RWKV_HEADS // 2
    slab = lambda p: slice(2 * n * p, 2 * n * (p + 1))
    pre = []
    for p in range(n_pairs):
        sl = slab(p)
        rp, kp, vp, ap = r[:, sl], k[:, sl], v[:, sl], a[:, sl]
        kkh = kp * kk_ref[:, sl]
        kkn = kkh * seg_sum(kkh * kkh, lambda s: lax.rsqrt(jnp.maximum(s, 1e-24)))
        kmod = kp * (1.0 + (ap - 1.0) * ka_ref[:, sl])
        bv = kkn * ap
        a_t = -kkn * e_excl[:, sl]
        r_t = rp * e_incl[:, sl]
        b_hat = bv * e_neg[:, sl]
        k_hat = kmod * e_neg[:, sl]
        mms = []
        for c in range(nck):
            bh, kh = b_hat[rows(c)], k_hat[rows(c)]
            rhs = jnp.concatenate([jnp.where(lo, bh, 0.0), jnp.where(lo, 0.0, bh),
                                   jnp.where(lo, kh, 0.0), jnp.where(lo, 0.0, kh)], axis=0)
            mms.append(_dot_nt(jnp.concatenate([a_t[rows(c)], r_t[rows(c)]], axis=0), rhs))
        pre.append(dict(r=rp, v=vp, kmod=kmod, a_t=a_t, r_t=r_t, b_end=bv * e_rev[:, sl],
                        k_end=kmod * e_rev[:, sl], mms=mms))

    chains = [(p, c0) for p in range(n_pairs) for c0 in range(0, nck, 2)]

    def pair_row(c0, f):
        return jnp.concatenate([f(c0), f(c0 + 1)], axis=1)

    quads = []
    for p, c0 in chains:
        mms = pre[p]["mms"]
        quads.append((
            jnp.where(strict, pair_row(c0, lambda c: mms[c][0:cw, 0:2 * n]), 0.0),
            jnp.where(strict, pair_row(c0, lambda c: mms[c][0:cw, 2 * n:4 * n]), 0.0),
            jnp.where(incl, pair_row(c0, lambda c: mms[c][cw:2 * cw, 0:2 * n]), 0.0),
            jnp.where(incl, pair_row(c0, lambda c: mms[c][cw:2 * cw, 2 * n:4 * n]), 0.0)))
    tinvs = [eye_row + q[0] for q in quads]
    xs = [_dot(q[0], block_diag(q[0])) for q in quads]
    for _ in range(int(math.log2(cw)) - 2):
        txs = [_dot(jnp.concatenate([t_, x_], axis=0), block_diag(x_)) for t_, x_ in zip(tinvs, xs)]
        tinvs = [t_ + tx[0:cw] for t_, tx in zip(tinvs, txs)]
        xs = [tx[cw:2 * cw] for tx in txs]
    tinvs = [t_ + _dot(t_, block_diag(x_)) for t_, x_ in zip(tinvs, xs)]
    lvs = [_dot(jnp.concatenate([q[1], q[3]], axis=0),
                block_diag(pair_row(c0, lambda c: pre[p]["v"][rows(c)])))
           for (p, c0), q in zip(chains, quads)]
    w_rows = [_dot(t_, block_diag(pair_row(c0, lambda c: pre[p]["a_t"][rows(c)])))
              for (p, c0), t_ in zip(chains, tinvs)]
    u_rows = [_dot(t_, block_diag(lv[0:cw])) for t_, lv in zip(tinvs, lvs)]
    qp_rows = [pair_row(c0, lambda c: pre[p]["r_t"][rows(c)]) + _dot(q[2], block_diag(w_))
               for (p, c0), q, w_ in zip(chains, quads, w_rows)]
    y0_rows = [_dot(q[2], block_diag(u_)) + lv[cw:2 * cw] for q, u_, lv in zip(quads, u_rows, lvs)]
    phs = {}
    for ci, (p, c0) in enumerate(chains):
        for j, c in enumerate((c0, c0 + 1)):
            ls = slice(2 * n * j, 2 * n * (j + 1))
            lhs = jnp.concatenate([pre[p]["b_end"][rows(c)], pre[p]["k_end"][rows(c)]], axis=0)
            rhs = jnp.concatenate(
                [jnp.concatenate([w_rows[ci][:, ls], u_rows[ci][:, ls]], axis=1),
                 jnp.concatenate([jnp.zeros((cw, 2 * n), F32), pre[p]["v"][rows(c)]], axis=1)], axis=0)
            ph = _dot_tn(lhs, rhs)
            pmat = eye_2n * g_end[c, :, slab(p)] + jnp.where(same_head, ph[:, 0:2 * n], 0.0)
            phs[(p, c)] = (jnp.concatenate([qp_rows[ci][:, ls], pmat], axis=0),
                           jnp.where(same_head, ph[:, 2 * n:4 * n], 0.0), y0_rows[ci][:, ls])
    hstates = [state_ref[p] for p in range(n_pairs)]
    ys = [[] for _ in range(n_pairs)]
    for c in range(nck):
        for p in range(n_pairs):
            qpm, hp, y0 = phs[(p, c)]
            qph = _dot(qpm, hstates[p])
            ys[p].append(qph[0:cw] + y0)
            hstates[p] = qph[cw:cw + 2 * n] + hp
    for p in range(n_pairs):
        sl = slab(p)
        state_ref[p] = hstates[p]
        y = jnp.concatenate(ys[p], axis=0)
        yc = y - seg_sum(y, lambda s: s * (1.0 / n))
        inv_std = seg_sum(yc * yc, lambda s: lax.rsqrt(s * (1.0 / n) + GN_EPS))
        yn = yc * inv_std * lng_ref[:, sl] + lnb_ref[:, sl]
        bonus = seg_sum(pre[p]["r"] * pre[p]["kmod"] * rk_ref[:, sl]) * pre[p]["v"]
        out_ref[0, :, sl] = ((yn + bonus) * gate[:, sl]).astype(out_ref.dtype)


def _rwkv_mixer(zr, zl, zg, mix_r, mix_l, mix_g, w0, w2, a0, a2, g2p, k_k, k_a, r_k, ln_g, ln_b):
    bsz, seq, _ = zr.shape
    tt = WKV_TILE
    assert seq % tt == 0 and tt % (2 * WKV_CHUNK) == 0 and 2 * RWKV_HEAD_DIM == LANES
    idx = np.arange(tt)
    tri = ((idx[:, None] // WKV_CHUNK == idx[None, :] // WKV_CHUNK) & (idx[:, None] >= idx[None, :]))
    full = lambda shape: pl.BlockSpec(shape, lambda b, t: (0,) * len(shape))
    row = lambda a: a.reshape(1, -1)
    wl = DECAY_LORA + ICLR_LORA
    return pl.pallas_call(
        _rwkv_kernel,
        grid=(bsz, seq // tt),
        in_specs=[
            pl.BlockSpec((1, tt, 3 * RWKV_WIDTH), lambda b, t: (b, t, 0)),
            pl.BlockSpec((1, tt, wl), lambda b, t: (b, t, 0)),
            pl.BlockSpec((1, tt, GATE_LORA_PAD), lambda b, t: (b, t, 0)),
            full((1, 3 * RWKV_WIDTH)), full((1, wl)), full((1, GATE_LORA_PAD)),
            full((1, RWKV_WIDTH)), full((DECAY_LORA, RWKV_WIDTH)),
            full((1, RWKV_WIDTH)), full((ICLR_LORA, RWKV_WIDTH)),
            full((GATE_LORA_PAD, RWKV_WIDTH)),
            full((1, RWKV_WIDTH)), full((1, RWKV_WIDTH)), full((1, RWKV_WIDTH)),
            full((1, RWKV_WIDTH)), full((1, RWKV_WIDTH)),
            full((tt, tt)),
        ],
        out_specs=pl.BlockSpec((1, tt, RWKV_WIDTH), lambda b, t: (b, t, 0)),
        out_shape=jax.ShapeDtypeStruct((bsz, seq, RWKV_WIDTH), BF16),
        scratch_shapes=[
            pltpu.VMEM((RWKV_HEADS // 2, 2 * RWKV_HEAD_DIM, 2 * RWKV_HEAD_DIM), F32),
            pltpu.VMEM((1, 3 * RWKV_WIDTH), F32),
            pltpu.VMEM((1, wl), F32),
            pltpu.VMEM((1, GATE_LORA_PAD), F32),
        ],
        compiler_params=pltpu.CompilerParams(
            dimension_semantics=("arbitrary", "arbitrary"), vmem_limit_bytes=VMEM_LIMIT_BYTES),
        name="rwkv7_mixer",
    )(zr, zl, zg, row(mix_r), row(mix_l), row(mix_g), row(w0), w2.astype(BF16), row(a0), a2.astype(BF16),
      g2p.astype(BF16), row(k_k), row(k_a), row(r_k), row(ln_g), row(ln_b), jnp.asarray(tri, BF16))


def _ffn_kernel(x_ref, yn_ref, yr_ref, won_ref, wor_ref, gain_ref, wg_ref, wu_ref, wd_ref, o_ref, *, ff_chunk):
    h = x_ref[...]
    h = h + jnp.dot(yn_ref[...], won_ref[...], preferred_element_type=F32)
    h = h + jnp.dot(yr_ref[...], wor_ref[...], preferred_element_type=F32)
    ms = jnp.mean(h * h, axis=-1, keepdims=True)
    hn = (h * lax.rsqrt(ms + RMS_EPS) * gain_ref[...]).astype(BF16)
    d_ff = wg_ref.shape[1]
    acc = None
    for c in range(d_ff // ff_chunk):
        cs = slice(c * ff_chunk, (c + 1) * ff_chunk)
        gte = jnp.dot(hn, wg_ref[:, cs], preferred_element_type=F32)
        up = jnp.dot(hn, wu_ref[:, cs], preferred_element_type=F32)
        act = (gte * jax.nn.sigmoid(gte) * up).astype(BF16)
        down = jnp.dot(act, wd_ref[cs, :], preferred_element_type=F32)
        acc = down if acc is None else acc + down
    o_ref[...] = h + acc


def _out_ffn(x2, y_nsa, y_rwkv, w_out, ffn_gain, w_gate, w_up, w_down):
    n, d = x2.shape
    d_ff = w_gate.shape[1]
    tm = min(ROW_TILE, n)
    ff_chunk = FF_CHUNK if d_ff % FF_CHUNK == 0 else d_ff
    const = lambda shape: pl.BlockSpec(shape, lambda i: (0, 0), pipeline_mode=pl.Buffered(1))
    return pl.pallas_call(
        functools.partial(_ffn_kernel, ff_chunk=ff_chunk),
        grid=(n // tm,),
        in_specs=[
            pl.BlockSpec((tm, d), lambda i: (i, 0)),
            pl.BlockSpec((tm, NSA_WIDTH), lambda i: (i, 0)),
            pl.BlockSpec((tm, RWKV_WIDTH), lambda i: (i, 0)),
            const((NSA_WIDTH, d)), const((RWKV_WIDTH, d)), const((1, d)),
            const((d, d_ff)), const((d, d_ff)), const((d_ff, d)),
        ],
        out_specs=pl.BlockSpec((tm, d), lambda i: (i, 0)),
        out_shape=jax.ShapeDtypeStruct((n, d), F32),
        compiler_params=pltpu.CompilerParams(
            dimension_semantics=("arbitrary",), vmem_limit_bytes=VMEM_LIMIT_BYTES),
        name="out_proj_ffn",
    )(x2, y_nsa, y_rwkv, w_out[:NSA_WIDTH].astype(BF16), w_out[NSA_WIDTH:].astype(BF16),
      ffn_gain.reshape(1, d), w_gate.astype(BF16), w_up.astype(BF16), w_down.astype(BF16))


def _layer(h, attn_gain, w_in, q_gain, k_gain, cmp_pe, cmp_w1, cmp_b1, cmp_w2, cmp_b2, out_gain, shift_mix,
           w0, w2, a0, a2, g2, k_k, k_a, r_k, ln_g, ln_b, w_out, ffn_gain, w_gate, w_up, w_down, rel_table):
    bsz, seq, d = h.shape
    nsa_cols = NSA_WIDTH + 6 * KV_WIDTH + 3 * NSA_HEADS
    gate0 = NSA_WIDTH + 6 * KV_WIDTH

    w_nsa, w_rw = w_in[:, :nsa_cols], w_in[:, nsa_cols:]
    gate_src = np.array([gate0 + h_ * 3 + br for br in range(3) for h_ in range(NSA_HEADS)])
    w_gates = jnp.zeros((d, LANES), F32).at[:, :3 * NSA_HEADS].set(w_nsa[:, gate_src])
    lora0 = 3 * RWKV_WIDTH
    g0 = lora0 + DECAY_LORA + ICLR_LORA
    w_g = jnp.zeros((d, GATE_LORA_PAD), F32).at[:, :GATE_LORA].set(w_rw[:, g0:g0 + GATE_LORA])
    widths = (NSA_WIDTH, 6 * KV_WIDTH, LANES, 3 * RWKV_WIDTH, DECAY_LORA + ICLR_LORA, GATE_LORA_PAD)
    w_cat = jnp.concatenate([w_nsa[:, :gate0], w_gates, w_rw[:, :g0], w_g], axis=1).astype(BF16)
    mix_g = jnp.zeros((GATE_LORA_PAD,), F32).at[:GATE_LORA].set(shift_mix[g0:g0 + GATE_LORA])
    g2p = jnp.zeros((GATE_LORA_PAD, RWKV_WIDTH), F32).at[:GATE_LORA].set(g2)

    x2 = h.reshape(bsz * seq, d)
    q, kv, gates, zr, zl, zg = _in_projection(x2, attn_gain.reshape(1, d), w_cat, widths)
    r3 = lambda u: u.reshape(bsz, seq, u.shape[-1])
    y_nsa = _nsa_mixer(r3(q), r3(kv), r3(gates), q_gain, k_gain, cmp_pe, cmp_w1, cmp_b1, cmp_w2, cmp_b2,
                       out_gain, rel_table)
    y_rwkv = _rwkv_mixer(r3(zr), r3(zl), r3(zg), shift_mix[:lora0], shift_mix[lora0:g0], mix_g,
                         w0, w2, a0, a2, g2p, k_k, k_a, r_k.reshape(-1), ln_g, ln_b)
    out = _out_ffn(x2, y_nsa.reshape(bsz * seq, NSA_WIDTH), y_rwkv.reshape(bsz * seq, RWKV_WIDTH),
                   w_out, ffn_gain, w_gate, w_up, w_down)
    return out.reshape(bsz, seq, d)


def kernel(x, attn_norm_gain, w_in, nsa_q_gain, nsa_k_gain, cmp_pe, cmp_w1, cmp_b1, cmp_w2, cmp_b2, nsa_out_gain, rwkv_shift_mix, rwkv_w0, rwkv_w2, rwkv_a0, rwkv_a2, rwkv_g2, rwkv_k_k, rwkv_k_a, rwkv_r_k, rwkv_ln_gain, rwkv_ln_bias, w_out, ffn_norm_gain, w_gate, w_up, w_down, rel_bias_table):
    h = x
    for l in range(attn_norm_gain.shape[0]):
        h = _layer(h, attn_norm_gain[l], w_in[l], nsa_q_gain[l], nsa_k_gain[l], cmp_pe[l], cmp_w1[l], cmp_b1[l],
                   cmp_w2[l], cmp_b2[l], nsa_out_gain[l], rwkv_shift_mix[l], rwkv_w0[l], rwkv_w2[l], rwkv_a0[l],
                   rwkv_a2[l], rwkv_g2[l], rwkv_k_k[l], rwkv_k_a[l], rwkv_r_k[l], rwkv_ln_gain[l],
                   rwkv_ln_bias[l], w_out[l], ffn_norm_gain[l], w_gate[l], w_up[l], w_down[l], rel_bias_table)
    return h
```

```python
import functools
import math

import numpy as np
import jax
import jax.numpy as jnp
from jax import lax
from jax.experimental import pallas as pl
from jax.experimental.pallas import tpu as pltpu

HEAD_DIM = 64
NSA_HEADS = 8
NSA_KV_HEADS = 2
NSA_HPG = NSA_HEADS // NSA_KV_HEADS
NSA_WIDTH = NSA_HEADS * HEAD_DIM
KV_WIDTH = NSA_KV_HEADS * HEAD_DIM
CMP_BLOCK = 32
CMP_STRIDE = 16
CMP_HIDDEN = 256
SEL_BLOCK = 64
SEL_TOP_N = 16
SEL_LOCAL = 2
WINDOW = 512
ATTN_SCALE = HEAD_DIM ** -0.5
LOG2E = math.log2(math.e)
NEG_INF = -1e30
FORCE_SCORE = 1e9
REL_BUCKETS = 32
REL_MAX_DIST = 128
RWKV_HEADS = 8
RWKV_HEAD_DIM = 64
RWKV_WIDTH = RWKV_HEADS * RWKV_HEAD_DIM
DECAY_LORA = 64
ICLR_LORA = 64
GATE_LORA = 160
GATE_LORA_PAD = 256
GN_EPS = 64e-5
RMS_EPS = 1e-6

LANES = 128
Q_TILE = 128
TILES_PER_STEP = 2
V_ROWS = HEAD_DIM + 16
K_TILE = 128
WKV_TILE = 256
WKV_CHUNK = 64
WKV_BATCH = 2
ROW_TILE = 512
FF_CHUNK = 256
VMEM_LIMIT_BYTES = 56 * 1024 * 1024

F32 = jnp.float32
BF16 = jnp.bfloat16


def _dot(a, b):
    return jnp.dot(a.astype(BF16), b.astype(BF16), preferred_element_type=F32)


def _dot_nt(a, b):
    return lax.dot_general(a.astype(BF16), b.astype(BF16), (((1,), (1,)), ((), ())),
                           preferred_element_type=F32)


def _dot_tn(a, b):
    return lax.dot_general(a.astype(BF16), b.astype(BF16), (((0,), (0,)), ((), ())),
                           preferred_element_type=F32)


def _split3(x):
    hi = x.astype(BF16)
    r1 = x - hi.astype(F32)
    mid = r1.astype(BF16)
    lo = (r1 - mid.astype(F32)).astype(BF16)
    return hi, mid, lo


def _dot_exact_rhs(x, m_bf16):
    hi, mid, lo = _split3(x)
    acc = jnp.dot(lo, m_bf16, preferred_element_type=F32)
    acc = acc + jnp.dot(mid, m_bf16, preferred_element_type=F32)
    return acc + jnp.dot(hi, m_bf16, preferred_element_type=F32)


def _dot_exact_lhs(m_bf16, x):
    hi, mid, lo = _split3(x)
    acc = jnp.dot(m_bf16, lo, preferred_element_type=F32)
    acc = acc + jnp.dot(m_bf16, mid, preferred_element_type=F32)
    return acc + jnp.dot(m_bf16, hi, preferred_element_type=F32)


def _rel_bucket_np(dist):
    max_exact = REL_BUCKETS // 2
    d = np.maximum(dist, 0)
    ratio = np.maximum(d, 1).astype(np.float32) / np.float32(max_exact)
    log_ratio = np.log(ratio).astype(np.float32) / np.float32(math.log(REL_MAX_DIST / max_exact))
    large = np.minimum(max_exact + (log_ratio * np.float32(REL_BUCKETS - max_exact)).astype(np.int32),
                       REL_BUCKETS - 1)
    return np.where(d < max_exact, d, large).astype(np.int32)


def _inproj_kernel(x_ref, gain_ref, w_ref, *out_refs, widths):
    x = x_ref[...]
    ms = jnp.mean(x * x, axis=-1, keepdims=True)
    xn = (x * lax.rsqrt(ms + RMS_EPS) * gain_ref[...]).astype(BF16)
    off = 0
    for o_ref, w in zip(out_refs, widths):
        o_ref[...] = jnp.dot(xn, w_ref[:, off:off + w], preferred_element_type=F32)
        off += w


def _in_projection(x2, gain, w_cat, widths):
    n, d = x2.shape
    tm = min(ROW_TILE, n)
    total = sum(widths)
    return pl.pallas_call(
        functools.partial(_inproj_kernel, widths=widths),
        grid=(n // tm,),
        in_specs=[
            pl.BlockSpec((tm, d), lambda i: (i, 0)),
            pl.BlockSpec((1, d), lambda i: (0, 0)),
            pl.BlockSpec((d, total), lambda i: (0, 0)),
        ],
        out_specs=[pl.BlockSpec((tm, w), lambda i: (i, 0)) for w in widths],
        out_shape=[jax.ShapeDtypeStruct((n, w), F32) for w in widths],
        compiler_params=pltpu.CompilerParams(
            dimension_semantics=("arbitrary",), vmem_limit_bytes=VMEM_LIMIT_BYTES),
        name="in_projection",
    )(x2, gain, w_cat)


def _head_rms(u, gain):
    ms = jnp.mean(u * u, axis=-1, keepdims=True)
    return u * lax.rsqrt(ms + RMS_EPS) * gain


def _nsa_kernel(tab_ref,
                q_ref, kv_ref, gate_ref,
                qg_ref, kg_ref, pe_ref, w1_ref, b1_ref, w2_ref, b2_ref, og_ref,
                bk0_ref, bk1_ref, bkc_ref, ovl_ref,
                out_ref,
                kvpad_ref, kcmp_ref, vcmp_ref, ksel_ref, vsel_ref, kwin_ref, vwin_ref,
                btab_ref, ccan_ref, ybuf_ref,
                *, seq):
    b = pl.program_id(0)
    step = pl.program_id(1)
    n_cb = seq // CMP_STRIDE
    n_c = (seq - CMP_BLOCK) // CMP_STRIDE + 1
    n_sel = seq // SEL_BLOCK
    cols = NSA_HPG * Q_TILE

    @pl.when((b == 0) & (step == 0))
    def _build_bias_tables():
        bk0 = bk0_ref[...]
        bk1 = bk1_ref[...]
        bkc = bkc_ref[...]
        kk = lax.broadcasted_iota(jnp.int32, (K_TILE, Q_TILE), 0)
        qq = lax.broadcasted_iota(jnp.int32, (K_TILE, Q_TILE), 1)
        causal = jnp.where(kk <= qq, 0.0, NEG_INF).astype(F32)
        anti = jnp.where(kk > qq, 0.0, NEG_INF).astype(F32)
        for h in range(NSA_HEADS):
            t0 = jnp.zeros((K_TILE, Q_TILE), F32)
            t1 = jnp.zeros((K_TILE, Q_TILE), F32)
            tc = jnp.zeros((2 * LANES, Q_TILE), F32)
            for k in range(REL_BUCKETS):
                val = tab_ref[k, h]
                t0 = jnp.where(bk0 == k, val, t0)
                t1 = jnp.where(bk1 == k, val, t1)
                tc = jnp.where(bkc == k, val, tc)
            far = tab_ref[REL_BUCKETS - 1, h]
            g, p = divmod(h, NSA_HPG)
            cs = slice(p * Q_TILE, (p + 1) * Q_TILE)
            btab_ref[g, 0, :, cs] = jnp.zeros((K_TILE, Q_TILE), F32)
            btab_ref[g, 1, :, cs] = (t1 - far) * LOG2E
            btab_ref[g, 2, :, cs] = (t0 - far) * LOG2E + causal
            btab_ref[g, 3, :, cs] = anti
            btab_ref[g, 4, :, cs] = jnp.full((K_TILE, Q_TILE), NEG_INF, F32)
            ccan_ref[h] = tc * LOG2E
        blk = lax.broadcasted_iota(jnp.int32, (seq, HEAD_DIM), 0) // SEL_BLOCK
        col = lax.broadcasted_iota(jnp.int32, (seq, HEAD_DIM), 1)
        onehot = (blk == col).astype(BF16)
        padcol = (lax.broadcasted_iota(jnp.int32, (WINDOW, 2 * HEAD_DIM), 1) == HEAD_DIM).astype(BF16)
        for g in range(NSA_KV_HEADS):
            ksel_ref[g, :, HEAD_DIM:2 * HEAD_DIM] = onehot
            kwin_ref[g, 0:WINDOW, :] = padcol
            kwin_ref[g, WINDOW:WINDOW + seq, HEAD_DIM:2 * HEAD_DIM] = jnp.zeros((seq, HEAD_DIM), BF16)
            vwin_ref[g, 0:HEAD_DIM, 0:WINDOW] = jnp.zeros((HEAD_DIM, WINDOW), BF16)
            for vt_ref, width in ((vsel_ref, seq), (vwin_ref, WINDOW + seq)):
                ones_row = lax.broadcasted_iota(jnp.int32, (V_ROWS - HEAD_DIM, width), 0) == 0
                vt_ref[g, HEAD_DIM:V_ROWS, :] = ones_row.astype(BF16)

    @pl.when(step == 0)
    def _per_batch():
        for which in range(2):
            kvpad_ref[which, seq:seq + 2 * CMP_STRIDE, :] = jnp.zeros((2 * CMP_STRIDE, KV_WIDTH), F32)
            kvpad_ref[which, 0:seq, :] = kv_ref[0, :, which * KV_WIDTH:(which + 1) * KV_WIDTH]
            first = lax.broadcasted_iota(jnp.int32, (1, KV_WIDTH), 1) < HEAD_DIM
            slabs = [[], []]
            for l in range(0, CMP_BLOCK, 2):
                ra = kvpad_ref[which, pl.ds(l, n_cb, stride=CMP_STRIDE), :]
                rb = kvpad_ref[which, pl.ds(l + 1, n_cb, stride=CMP_STRIDE), :]
                slabs[0].append(jnp.where(first, ra, pltpu.roll(rb, HEAD_DIM, 1)))
                slabs[1].append(jnp.where(first, pltpu.roll(ra, HEAD_DIM, 1), rb))
            cmps = []
            for g in range(NSA_KV_HEADS):
                blk = jnp.concatenate(slabs[g], axis=1) + pe_ref[which]
                hid = jax.nn.gelu(_dot(blk, w1_ref[which]) + b1_ref[which], approximate=True)
                cmps.append(_dot(hid, w2_ref[which]) + b2_ref[which])
            if which == 0:
                for g in range(NSA_KV_HEADS):
                    kcmp_ref[g] = _head_rms(cmps[g], kg_ref[0:1, :]).astype(BF16)
            else:
                vct = jnp.concatenate(cmps, axis=1).T
                for g in range(NSA_KV_HEADS):
                    vcmp_ref[g] = vct[g * HEAD_DIM:(g + 1) * HEAD_DIM].astype(BF16)
        kv = kv_ref[0]
        for g in range(NSA_KV_HEADS):
            def col(i):
                c = (2 + i) * KV_WIDTH + g * HEAD_DIM
                return kv[:, c:c + HEAD_DIM]
            ksel_ref[g, :, 0:HEAD_DIM] = _head_rms(col(0), kg_ref[1:2, :]).astype(BF16)
            kwin_ref[g, WINDOW:WINDOW + seq, 0:HEAD_DIM] = _head_rms(col(2), kg_ref[2:3, :]).astype(BF16)
        for i, vt_ref, pad in ((1, vsel_ref, 0), (3, vwin_ref, WINDOW)):
            vt = kv[:, (2 + i) * KV_WIDTH:(3 + i) * KV_WIDTH].T
            for g in range(NSA_KV_HEADS):
                vt_ref[g, 0:HEAD_DIM, pad:pad + seq] = vt[g * HEAD_DIM:(g + 1) * HEAD_DIM].astype(BF16)

    n_selp = -(-n_sel // 8) * 8
    qT_all = q_ref[0].T
    gates_all = jax.nn.sigmoid(gate_ref[0].T)
    tiles = range(TILES_PER_STEP)
    groups = range(NSA_KV_HEADS)
    units = [(j, g) for j in tiles for g in groups]
    each = lambda f: {u: f(*u) for u in units}
    qts = [TILES_PER_STEP * step + j for j in tiles]
    tqs = [qts[j] * Q_TILE + lax.broadcasted_iota(jnp.int32, (1, Q_TILE), 1) for j in tiles]
    lanes = lambda j: slice(j * Q_TILE, (j + 1) * Q_TILE)

    def q_group(j, g):
        qs = []
        for p in range(NSA_HPG):
            h = g * NSA_HPG + p
            qh = qT_all[h * HEAD_DIM:(h + 1) * HEAD_DIM, lanes(j)]
            ms = jnp.mean(qh * qh, axis=0, keepdims=True)
            qs.append((qh * lax.rsqrt(ms + RMS_EPS) * qg_ref[...] * (ATTN_SCALE * LOG2E)).astype(BF16))
        return jnp.concatenate(qs, axis=1)

    q_cats = each(q_group)
    aug_row = lax.broadcasted_iota(jnp.int32, (HEAD_DIM, cols), 0)

    cc = lax.broadcasted_iota(jnp.int32, (n_cb, Q_TILE), 0)
    valid4 = [jnp.concatenate([(cc * CMP_STRIDE + (CMP_BLOCK - 1) <= tqs[j]) & (cc < n_c)] * NSA_HPG, axis=1)
              for j in tiles]
    c_starts = [pl.multiple_of(LANES - (CMP_STRIDE // 2) * qts[j], 8) for j in tiles]
    ss = each(lambda j, g: jnp.dot(kcmp_ref[g], q_cats[j, g], preferred_element_type=F32))
    ss = each(lambda j, g: jnp.where(valid4[j], ss[j, g] + jnp.concatenate(
        [ccan_ref[g * NSA_HPG + p, pl.ds(c_starts[j], n_cb), :] for p in range(NSA_HPG)], axis=1), NEG_INF))
    es = each(lambda j, g: jnp.exp2(ss[j, g] - jnp.max(ss[j, g], axis=0, keepdims=True)))
    pcs = each(lambda j, g: es[j, g] * (1.0 / jnp.sum(es[j, g], axis=0, keepdims=True))
               * valid4[j].astype(F32))
    o_cmps = each(lambda j, g: jnp.dot(vcmp_ref[g], pcs[j, g].astype(BF16), preferred_element_type=F32))
    psums = each(lambda j, g: functools.reduce(
        lambda u, w: u + w, [pcs[j, g][:, p * Q_TILE:(p + 1) * Q_TILE] for p in range(NSA_HPG)]))

    n_wb = WINDOW // K_TILE
    band = WINDOW + K_TILE
    band_kinds = (3,) + (0,) * (n_wb - 2) + (1, 2)
    w0s = [pl.multiple_of(qts[j] * K_TILE, K_TILE) for j in tiles]
    pad_rows = jnp.where(aug_row == 0, NEG_INF, 0.0).astype(BF16)

    def band_logits(g, s):
        blocks = []
        for i, kind in enumerate(band_kinds):
            si = s[i * K_TILE:(i + 1) * K_TILE]
            blocks.append(si if kind == 0 else si + btab_ref[g, kind])
        return jnp.concatenate(blocks, axis=0)

    sw = each(lambda j, g: jnp.dot(kwin_ref[g, pl.ds(w0s[j], band), :],
                                   jnp.concatenate([q_cats[j, g], pad_rows], axis=0),
                                   preferred_element_type=F32))
    sw = each(lambda j, g: band_logits(g, sw[j, g]))
    pws = each(lambda j, g: jnp.exp2(sw[j, g] - jnp.max(sw[j, g], axis=0, keepdims=True)))
    o_wins = each(lambda j, g: jnp.dot(vwin_ref[g, :, pl.ds(w0s[j], band)], pws[j, g].astype(BF16),
                                       preferred_element_type=F32))
    o_wins = each(lambda j, g: o_wins[j, g][0:HEAD_DIM] * (1.0 / o_wins[j, g][HEAD_DIM:HEAD_DIM + 1]))

    imps = each(lambda j, g: _dot_exact_lhs(ovl_ref[...], psums[j, g])[0:n_selp])
    jb = lax.broadcasted_iota(jnp.int32, (n_selp, Q_TILE), 0)

    def masked_score(j, g):
        qb = tqs[j] // SEL_BLOCK
        forced = (jb == 0) | ((jb <= qb) & (jb > qb - SEL_LOCAL))
        score = jnp.where(forced, FORCE_SCORE, imps[j, g])
        score = jnp.where(jb <= qb, score, -1.0)
        return jnp.where(jb < n_sel, score, -2.0)

    scores = each(masked_score)
    sub = 8
    sub_row = lax.broadcasted_iota(jnp.int32, (sub, Q_TILE), 0)
    score_rows = {u: [scores[u][r0:r0 + sub] for r0 in range(0, n_selp, sub)] for u in units}
    rank_rows = {u: [jnp.zeros((sub, Q_TILE), F32) for _ in range(0, n_selp, sub)] for u in units}
    for i in range(n_sel):
        for u in units:
            si = scores[u][i:i + 1, :]
            for v, s_v in enumerate(score_rows[u]):
                r0 = v * sub
                if r0 > i:
                    ahead = si >= s_v
                elif r0 + sub - 1 < i:
                    ahead = si > s_v
                else:
                    ahead = (si > s_v) | ((si == s_v) & (sub_row + r0 > i))
                rank_rows[u][v] = rank_rows[u][v] + ahead.astype(F32)
    ranks = {u: jnp.concatenate(rank_rows[u], axis=0) for u in units}

    def q_with_mask_rows(j, g):
        sel = (ranks[j, g] < float(min(SEL_TOP_N, n_sel))) & (jb < n_sel)
        nm = (sel.astype(F32) - 1.0) * (-NEG_INF)
        return jnp.concatenate([q_cats[j, g], jnp.concatenate([nm] * NSA_HPG, axis=1).astype(BF16),
                                jnp.zeros((HEAD_DIM - n_selp, cols), BF16)], axis=0)

    q_sels = each(q_with_mask_rows)

    pair = 2 * K_TILE

    n_pairs = qts[-1] // 2 + 1
    n_far = jnp.maximum((qts[0] - 1) // 2, 0)

    def sel_step(kp, carry, near):
        k0 = pl.multiple_of(kp * pair, pair)
        old = {u: carry[i] for i, u in enumerate(units)}

        def logits(j, g):
            s = jnp.dot(ksel_ref[g, pl.ds(k0, pair), :], q_sels[j, g], preferred_element_type=F32)
            if near:
                rel = qts[j] - 2 * kp
                kind_of = lambda d: jnp.where(d >= 2, 0, jnp.where(d == 1, 1, jnp.where(d == 0, 2, 4)))
                kind_a, kind_b = kind_of(rel), kind_of(rel - 1)
                s = s + jnp.concatenate([btab_ref[g, kind_a], btab_ref[g, kind_b]], axis=0)
            return s

        ss = each(logits)
        m_new = each(lambda j, g: jnp.maximum(old[j, g][0], jnp.max(ss[j, g], axis=0, keepdims=True)))
        ps = each(lambda j, g: jnp.exp2(ss[j, g] - m_new[j, g]))
        alpha = each(lambda j, g: jnp.exp2(old[j, g][0] - m_new[j, g]))
        pv = each(lambda j, g: jnp.dot(vsel_ref[g, :, pl.ds(k0, pair)], ps[j, g].astype(BF16),
                                       preferred_element_type=F32))
        return tuple((m_new[u], alpha[u] * old[u][1] + pv[u]) for u in units)

    init = (jnp.full((1, cols), NEG_INF, F32), jnp.zeros((V_ROWS, cols), F32))
    sel_mid = lax.fori_loop(0, n_far, functools.partial(sel_step, near=False), tuple(init for _ in units))
    sel_out = lax.fori_loop(n_far, n_pairs, functools.partial(sel_step, near=True), sel_mid)

    ssqs = []
    for j in tiles:
        gates = gates_all[:, lanes(j)]
        ssq = jnp.zeros((1, Q_TILE), F32)
        for g in groups:
            _, acc_s = sel_out[units.index((j, g))]
            o_sel = acc_s[0:HEAD_DIM] * (1.0 / acc_s[HEAD_DIM:HEAD_DIM + 1])
            for p in range(NSA_HPG):
                h = g * NSA_HPG + p
                cs = slice(p * Q_TILE, (p + 1) * Q_TILE)
                o = (gates[h:h + 1] * o_cmps[j, g][:, cs]
                     + gates[NSA_HEADS + h:NSA_HEADS + h + 1] * o_sel[:, cs]
                     + gates[2 * NSA_HEADS + h:2 * NSA_HEADS + h + 1] * o_wins[j, g][:, cs])
                ssq = ssq + jnp.sum(o * o, axis=0, keepdims=True)
                ybuf_ref[h * HEAD_DIM:(h + 1) * HEAD_DIM, lanes(j)] = o
        ssqs.append(ssq)
    ssq_all = jnp.concatenate(ssqs, axis=1)
    yT = ybuf_ref[...] * lax.rsqrt(ssq_all / NSA_WIDTH + RMS_EPS) * og_ref[...]
    out_ref[0] = yT.T.astype(out_ref.dtype)


def _nsa_mixer(q, kv, gates, q_gain, k_gain, cmp_pe, cmp_w1, cmp_b1, cmp_w2, cmp_b2, out_gain, rel_table):
    bsz, seq, _ = q.shape
    assert TILES_PER_STEP % 2 == 0 and seq % (TILES_PER_STEP * Q_TILE) == 0
    assert seq // CMP_STRIDE == LANES and seq // SEL_BLOCK <= HEAD_DIM
    n_steps = seq // (TILES_PER_STEP * Q_TILE)
    q_rows = TILES_PER_STEP * Q_TILE
    n_cb = seq // CMP_STRIDE
    n_sel = seq // SEL_BLOCK
    n_selp = -(-n_sel // 8) * 8
    cols = NSA_HPG * Q_TILE

    kk = np.arange(K_TILE)[:, None]
    qq = np.arange(Q_TILE)[None, :]
    bk0 = _rel_bucket_np(qq - kk)
    bk1 = _rel_bucket_np(K_TILE + qq - kk)
    cprime = np.arange(2 * LANES)[:, None] - LANES
    bkc = _rel_bucket_np(qq - (CMP_BLOCK - 1) - CMP_STRIDE * cprime)
    c_start = np.arange(n_cb)[None, :] * CMP_STRIDE
    j_start = np.arange(LANES)[:, None] * SEL_BLOCK
    ovl = np.clip(np.minimum(c_start + CMP_BLOCK, j_start + SEL_BLOCK) - np.maximum(c_start, j_start), 0, None)
    ovl = (ovl.astype(np.float32) / CMP_STRIDE)
    ovl[n_sel:, :] = 0.0

    assert NSA_KV_HEADS == 2 and KV_WIDTH == LANES
    w1 = cmp_w1.astype(BF16)
    full = lambda shape: pl.BlockSpec(shape, lambda b, t: (0,) * len(shape))
    kernel = functools.partial(_nsa_kernel, seq=seq)
    return pl.pallas_call(
        kernel,
        grid=(bsz, n_steps),
        in_specs=[
            pl.BlockSpec(memory_space=pltpu.SMEM),
            pl.BlockSpec((1, q_rows, NSA_WIDTH), lambda b, t: (b, t, 0)),
            pl.BlockSpec((1, seq, 6 * KV_WIDTH), lambda b, t: (b, 0, 0)),
            pl.BlockSpec((1, q_rows, LANES), lambda b, t: (b, t, 0)),
            full((HEAD_DIM, Q_TILE)),
            full((3, HEAD_DIM)),
            full((2, 1, CMP_BLOCK * HEAD_DIM)),
            full((2, CMP_BLOCK * HEAD_DIM, CMP_HIDDEN)),
            full((2, 1, CMP_HIDDEN)),
            full((2, CMP_HIDDEN, HEAD_DIM)),
            full((2, 1, HEAD_DIM)),
            full((NSA_WIDTH, q_rows)),
            full((K_TILE, Q_TILE)),
            full((K_TILE, Q_TILE)),
            full((2 * LANES, Q_TILE)),
            full((LANES, n_cb)),
        ],
        out_specs=pl.BlockSpec((1, q_rows, NSA_WIDTH), lambda b, t: (b, t, 0)),
        out_shape=jax.ShapeDtypeStruct((bsz, seq, NSA_WIDTH), BF16),
        scratch_shapes=[
            pltpu.VMEM((2, seq + 2 * CMP_STRIDE, KV_WIDTH), F32),
            pltpu.VMEM((NSA_KV_HEADS, n_cb, HEAD_DIM), BF16),
            pltpu.VMEM((NSA_KV_HEADS, HEAD_DIM, n_cb), BF16),
            pltpu.VMEM((NSA_KV_HEADS, seq, 2 * HEAD_DIM), BF16),
            pltpu.VMEM((NSA_KV_HEADS, V_ROWS, seq), BF16),
            pltpu.VMEM((NSA_KV_HEADS, WINDOW + seq, 2 * HEAD_DIM), BF16),
            pltpu.VMEM((NSA_KV_HEADS, V_ROWS, WINDOW + seq), BF16),
            pltpu.VMEM((NSA_KV_HEADS, 5, K_TILE, cols), F32),
            pltpu.VMEM((NSA_HEADS, 2 * LANES, Q_TILE), F32),
            pltpu.VMEM((NSA_WIDTH, q_rows), F32),
        ],
        compiler_params=pltpu.CompilerParams(
            dimension_semantics=("arbitrary", "arbitrary"), vmem_limit_bytes=VMEM_LIMIT_BYTES),
        name="nsa_mixer",
    )(rel_table, q, kv, gates, jnp.broadcast_to(q_gain[:, None], (HEAD_DIM, Q_TILE)), k_gain,
      cmp_pe.reshape(2, 1, CMP_BLOCK * HEAD_DIM), w1,
      cmp_b1.reshape(2, 1, CMP_HIDDEN), cmp_w2.astype(BF16), cmp_b2.reshape(2, 1, HEAD_DIM),
      jnp.broadcast_to(out_gain[:, None], (NSA_WIDTH, q_rows)), jnp.asarray(bk0), jnp.asarray(bk1),
      jnp.asarray(bkc), jnp.asarray(ovl, BF16))


def _rwkv_kernel(zr_ref, zl_ref, zg_ref, mr_ref, ml_ref, mg_ref, w0_ref, w2_ref, a0_ref, a2_ref, g2_ref,
                 kk_ref, ka_ref, rk_ref, lng_ref, lnb_ref, tri_ref,
                 out_ref,
                 state_ref, cr_ref, cl_ref, cg_ref):
    t = pl.program_id(1)
    tt = WKV_TILE
    n = RWKV_HEAD_DIM
    nb = zr_ref.shape[0]
    nck_seq = tt // WKV_CHUNK
    nck = nb * nck_seq

    @pl.when(t == 0)
    def _reset():
        state_ref[...] = jnp.zeros_like(state_ref)
        cr_ref[...] = jnp.zeros_like(cr_ref)
        cl_ref[...] = jnp.zeros_like(cl_ref)
        cg_ref[...] = jnp.zeros_like(cg_ref)

    row = lax.broadcasted_iota(jnp.int32, (tt, 1), 0)

    def shifted(z_ref, mix_ref, carry_ref):
        parts = []
        for i in range(nb):
            z = z_ref[i]
            prev = jnp.where(row == 0, carry_ref[i], pltpu.roll(z, 1, 0))
            carry_ref[i] = z_ref[i, tt - 1:tt, :]
            parts.append(z + mix_ref[...] * (prev - z))
        return jnp.concatenate(parts, axis=0)

    zr = shifted(zr_ref, mr_ref, cr_ref)
    zl = shifted(zl_ref, ml_ref, cl_ref)
    zg = shifted(zg_ref, mg_ref, cg_ref)
    r = zr[:, 0:RWKV_WIDTH]
    k = zr[:, RWKV_WIDTH:2 * RWKV_WIDTH]
    v = zr[:, 2 * RWKV_WIDTH:3 * RWKV_WIDTH]

    xw = w0_ref[...] + _dot(jnp.tanh(zl[:, 0:DECAY_LORA]), w2_ref[...])
    lw = -math.exp(-0.5) * jax.nn.sigmoid(xw)
    a = jax.nn.sigmoid(a0_ref[...] + _dot(zl[:, DECAY_LORA:DECAY_LORA + ICLR_LORA], a2_ref[...]))
    gate = _dot(jax.nn.sigmoid(zg), g2_ref[...])

    cum = jnp.concatenate([_dot_exact_lhs(tri_ref[...], lw[i * tt:(i + 1) * tt]) for i in range(nb)],
                          axis=0)
    cum_end = cum.reshape(nck, WKV_CHUNK, RWKV_WIDTH)[:, WKV_CHUNK - 1:WKV_CHUNK, :]
    cum_end_b = jnp.broadcast_to(cum_end, (nck, WKV_CHUNK, RWKV_WIDTH)).reshape(nb * tt, RWKV_WIDTH)
    e_incl = jnp.exp(cum)
    e_excl = jnp.exp(cum - lw)
    e_neg = jnp.exp(-cum)
    e_rev = jnp.exp(cum_end_b - cum)
    g_end = jnp.exp(cum_end)

    cw = WKV_CHUNK
    lane = lax.broadcasted_iota(jnp.int32, (1, 2 * n), 1)
    lo = lane < n
    ri = lax.broadcasted_iota(jnp.int32, (cw, 4 * n), 0)
    ci = lax.broadcasted_iota(jnp.int32, (cw, 4 * n), 1) % n
    strict = ri > ci
    incl = ri >= ci
    eye_row = (ri == ci).astype(F32)
    i2 = lax.broadcasted_iota(jnp.int32, (2 * n, 2 * n), 0)
    j2 = lax.broadcasted_iota(jnp.int32, (2 * n, 2 * n), 1)
    same_head = (i2 // n) == (j2 // n)
    eye_2n = (i2 == j2).astype(F32)
    zeros_slab = jnp.zeros((cw, 2 * n), BF16)

    def seg_sum(u, f=lambda s: s):
        s_lo = jnp.sum(jnp.where(lo, u, 0.0), axis=-1, keepdims=True)
        s_hi = jnp.sum(jnp.where(lo, 0.0, u), axis=-1, keepdims=True)
        return jnp.where(lo, f(s_lo), f(s_hi))

    def block_diag(u):
        ua, ub = u[:, 0:2 * n], u[:, 2 * n:4 * n]
        parts = []
        for w, first in ((ua, True), (ub, False)):
            for keep_lo in (True, False):
                blk = jnp.where(lo, w, 0.0) if keep_lo else jnp.where(lo, 0.0, w)
                blk = blk.astype(BF16)
                parts.append(jnp.concatenate([blk, zeros_slab] if first else [zeros_slab, blk], axis=1))
        return jnp.concatenate(parts, axis=0)

    rows = lambda c: slice(c * cw, (c + 1) * cw)

    n_pairs = RWKV_HEADS // 2
    slab = lambda p: slice(2 * n * p, 2 * n * (p + 1))
    pre = []
    for p in range(n_pairs):
        sl = slab(p)
        rp, kp, vp, ap = r[:, sl], k[:, sl], v[:, sl], a[:, sl]
        kkh = kp * kk_ref[:, sl]
        kkn = kkh * seg_sum(kkh * kkh, lambda s: lax.rsqrt(jnp.maximum(s, 1e-24)))
        kmod = kp * (1.0 + (ap - 1.0) * ka_ref[:, sl])
        bv = kkn * ap
        a_t = -kkn * e_excl[:, sl]
        r_t = rp * e_incl[:, sl]
        b_hat = bv * e_neg[:, sl]
        k_hat = kmod * e_neg[:, sl]
        mms = []
        for c in range(nck):
            bh, kh = b_hat[rows(c)], k_hat[rows(c)]
            rhs = jnp.concatenate([jnp.where(lo, bh, 0.0), jnp.where(lo, 0.0, bh),
                                   jnp.where(lo, kh, 0.0), jnp.where(lo, 0.0, kh)], axis=0)
            mms.append(_dot_nt(jnp.concatenate([a_t[rows(c)], r_t[rows(c)]], axis=0), rhs))
        pre.append(dict(r=rp, v=vp, kmod=kmod, a_t=a_t, r_t=r_t, b_end=bv * e_rev[:, sl],
                        k_end=kmod * e_rev[:, sl], mms=mms))

    chains = [(p, c0) for p in range(n_pairs) for c0 in range(0, nck, 2)]

    def pair_row(c0, f):
        return jnp.concatenate([f(c0), f(c0 + 1)], axis=1)

    quads = []
    for p, c0 in chains:
        mms = pre[p]["mms"]
        quads.append((
            jnp.where(strict, pair_row(c0, lambda c: mms[c][0:cw, 0:2 * n]), 0.0),
            jnp.where(strict, pair_row(c0, lambda c: mms[c][0:cw, 2 * n:4 * n]), 0.0),
            jnp.where(incl, pair_row(c0, lambda c: mms[c][cw:2 * cw, 0:2 * n]), 0.0),
            jnp.where(incl, pair_row(c0, lambda c: mms[c][cw:2 * cw, 2 * n:4 * n]), 0.0)))
    tinvs = [eye_row + q[0] for q in quads]
    xs = [_dot(q[0], block_diag(q[0])) for q in quads]
    for _ in range(int(math.log2(cw)) - 2):
        txs = [_dot(jnp.concatenate([t_, x_], axis=0), block_diag(x_)) for t_, x_ in zip(tinvs, xs)]
        tinvs = [t_ + tx[0:cw] for t_, tx in zip(tinvs, txs)]
        xs = [tx[cw:2 * cw] for tx in txs]
    tinvs = [t_ + _dot(t_, block_diag(x_)) for t_, x_ in zip(tinvs, xs)]
    lvs = [_dot(jnp.concatenate([q[1], q[3]], axis=0),
                block_diag(pair_row(c0, lambda c: pre[p]["v"][rows(c)])))
           for (p, c0), q in zip(chains, quads)]
    w_rows = [_dot(t_, block_diag(pair_row(c0, lambda c: pre[p]["a_t"][rows(c)])))
              for (p, c0), t_ in zip(chains, tinvs)]
    u_rows = [_dot(t_, block_diag(lv[0:cw])) for t_, lv in zip(tinvs, lvs)]
    qp_rows = [pair_row(c0, lambda c: pre[p]["r_t"][rows(c)]) + _dot(q[2], block_diag(w_))
               for (p, c0), q, w_ in zip(chains, quads, w_rows)]
    y0_rows = [_dot(q[2], block_diag(u_)) + lv[cw:2 * cw] for q, u_, lv in zip(quads, u_rows, lvs)]
    phs = {}
    for ci, (p, c0) in enumerate(chains):
        for j, c in enumerate((c0, c0 + 1)):
            ls = slice(2 * n * j, 2 * n * (j + 1))
            lhs = jnp.concatenate([pre[p]["b_end"][rows(c)], pre[p]["k_end"][rows(c)]], axis=0)
            rhs = jnp.concatenate(
                [jnp.concatenate([w_rows[ci][:, ls], u_rows[ci][:, ls]], axis=1),
                 jnp.concatenate([jnp.zeros((cw, 2 * n), F32), pre[p]["v"][rows(c)]], axis=1)], axis=0)
            ph = _dot_tn(lhs, rhs)
            pmat = eye_2n * g_end[c, :, slab(p)] + jnp.where(same_head, ph[:, 0:2 * n], 0.0)
            phs[(p, c)] = (jnp.concatenate([qp_rows[ci][:, ls], pmat], axis=0),
                           jnp.where(same_head, ph[:, 2 * n:4 * n], 0.0), y0_rows[ci][:, ls])
    seqs = [(i, p) for i in range(nb) for p in range(n_pairs)]
    hstates = {ip: state_ref[ip[0], ip[1]] for ip in seqs}
    ys = {ip: [] for ip in seqs}
    for c in range(nck_seq):
        for i, p in seqs:
            qpm, hp, y0 = phs[(p, i * nck_seq + c)]
            qph = _dot(qpm, hstates[i, p])
            ys[i, p].append(qph[0:cw] + y0)
            hstates[i, p] = qph[cw:cw + 2 * n] + hp
    for i, p in seqs:
        state_ref[i, p] = hstates[i, p]
    for p in range(n_pairs):
        sl = slab(p)
        y = jnp.concatenate([u for i in range(nb) for u in ys[i, p]], axis=0)
        yc = y - seg_sum(y, lambda s: s * (1.0 / n))
        inv_std = seg_sum(yc * yc, lambda s: lax.rsqrt(s * (1.0 / n) + GN_EPS))
        yn = yc * inv_std * lng_ref[:, sl] + lnb_ref[:, sl]
        bonus = seg_sum(pre[p]["r"] * pre[p]["kmod"] * rk_ref[:, sl]) * pre[p]["v"]
        res = ((yn + bonus) * gate[:, sl]).astype(out_ref.dtype)
        for i in range(nb):
            out_ref[i, :, sl] = res[i * tt:(i + 1) * tt]


def _rwkv_mixer(zr, zl, zg, mix_r, mix_l, mix_g, w0, w2, a0, a2, g2p, k_k, k_a, r_k, ln_g, ln_b):
    bsz, seq, _ = zr.shape
    tt = WKV_TILE
    assert seq % tt == 0 and tt % (2 * WKV_CHUNK) == 0 and 2 * RWKV_HEAD_DIM == LANES
    nb = WKV_BATCH if bsz % WKV_BATCH == 0 else 1
    idx = np.arange(tt)
    tri = ((idx[:, None] // WKV_CHUNK == idx[None, :] // WKV_CHUNK) & (idx[:, None] >= idx[None, :]))
    full = lambda shape: pl.BlockSpec(shape, lambda b, t: (0,) * len(shape))
    row = lambda a: a.reshape(1, -1)
    wl = DECAY_LORA + ICLR_LORA
    return pl.pallas_call(
        _rwkv_kernel,
        grid=(bsz // nb, seq // tt),
        in_specs=[
            pl.BlockSpec((nb, tt, 3 * RWKV_WIDTH), lambda b, t: (b, t, 0)),
            pl.BlockSpec((nb, tt, wl), lambda b, t: (b, t, 0)),
            pl.BlockSpec((nb, tt, GATE_LORA_PAD), lambda b, t: (b, t, 0)),
            full((1, 3 * RWKV_WIDTH)), full((1, wl)), full((1, GATE_LORA_PAD)),
            full((1, RWKV_WIDTH)), full((DECAY_LORA, RWKV_WIDTH)),
            full((1, RWKV_WIDTH)), full((ICLR_LORA, RWKV_WIDTH)),
            full((GATE_LORA_PAD, RWKV_WIDTH)),
            full((1, RWKV_WIDTH)), full((1, RWKV_WIDTH)), full((1, RWKV_WIDTH)),
            full((1, RWKV_WIDTH)), full((1, RWKV_WIDTH)),
            full((tt, tt)),
        ],
        out_specs=pl.BlockSpec((nb, tt, RWKV_WIDTH), lambda b, t: (b, t, 0)),
        out_shape=jax.ShapeDtypeStruct((bsz, seq, RWKV_WIDTH), BF16),
        scratch_shapes=[
            pltpu.VMEM((nb, RWKV_HEADS // 2, 2 * RWKV_HEAD_DIM, 2 * RWKV_HEAD_DIM), F32),
            pltpu.VMEM((nb, 1, 3 * RWKV_WIDTH), F32),
            pltpu.VMEM((nb, 1, wl), F32),
            pltpu.VMEM((nb, 1, GATE_LORA_PAD), F32),
        ],
        compiler_params=pltpu.CompilerParams(
            dimension_semantics=("arbitrary", "arbitrary"), vmem_limit_bytes=VMEM_LIMIT_BYTES),
        name="rwkv7_mixer",
    )(zr, zl, zg, row(mix_r), row(mix_l), row(mix_g), row(w0), w2.astype(BF16), row(a0), a2.astype(BF16),
      g2p.astype(BF16), row(k_k), row(k_a), row(r_k), row(ln_g), row(ln_b), jnp.asarray(tri, BF16))


def _ffn_kernel(x_ref, yn_ref, yr_ref, won_ref, wor_ref, gain_ref, wg_ref, wu_ref, wd_ref, o_ref, *, ff_chunk):
    h = x_ref[...]
    h = h + jnp.dot(yn_ref[...], won_ref[...], preferred_element_type=F32)
    h = h + jnp.dot(yr_ref[...], wor_ref[...], preferred_element_type=F32)
    ms = jnp.mean(h * h, axis=-1, keepdims=True)
    hn = (h * lax.rsqrt(ms + RMS_EPS) * gain_ref[...]).astype(BF16)
    d_ff = wg_ref.shape[1]
    acc = None
    for c in range(d_ff // ff_chunk):
        cs = slice(c * ff_chunk, (c + 1) * ff_chunk)
        gte = jnp.dot(hn, wg_ref[:, cs], preferred_element_type=F32)
        up = jnp.dot(hn, wu_ref[:, cs], preferred_element_type=F32)
        act = (gte * jax.nn.sigmoid(gte) * up).astype(BF16)
        down = jnp.dot(act, wd_ref[cs, :], preferred_element_type=F32)
        acc = down if acc is None else acc + down
    o_ref[...] = h + acc


def _out_ffn(x2, y_nsa, y_rwkv, w_out, ffn_gain, w_gate, w_up, w_down):
    n, d = x2.shape
    d_ff = w_gate.shape[1]
    tm = min(ROW_TILE, n)
    ff_chunk = FF_CHUNK if d_ff % FF_CHUNK == 0 else d_ff
    const = lambda shape: pl.BlockSpec(shape, lambda i: (0, 0), pipeline_mode=pl.Buffered(1))
    return pl.pallas_call(
        functools.partial(_ffn_kernel, ff_chunk=ff_chunk),
        grid=(n // tm,),
        in_specs=[
            pl.BlockSpec((tm, d), lambda i: (i, 0)),
            pl.BlockSpec((tm, NSA_WIDTH), lambda i: (i, 0)),
            pl.BlockSpec((tm, RWKV_WIDTH), lambda i: (i, 0)),
            const((NSA_WIDTH, d)), const((RWKV_WIDTH, d)), const((1, d)),
            const((d, d_ff)), const((d, d_ff)), const((d_ff, d)),
        ],
        out_specs=pl.BlockSpec((tm, d), lambda i: (i, 0)),
        out_shape=jax.ShapeDtypeStruct((n, d), F32),
        compiler_params=pltpu.CompilerParams(
            dimension_semantics=("arbitrary",), vmem_limit_bytes=VMEM_LIMIT_BYTES),
        name="out_proj_ffn",
    )(x2, y_nsa, y_rwkv, w_out[:NSA_WIDTH].astype(BF16), w_out[NSA_WIDTH:].astype(BF16),
      ffn_gain.reshape(1, d), w_gate.astype(BF16), w_up.astype(BF16), w_down.astype(BF16))


def _layer(h, attn_gain, w_in, q_gain, k_gain, cmp_pe, cmp_w1, cmp_b1, cmp_w2, cmp_b2, out_gain, shift_mix,
           w0, w2, a0, a2, g2, k_k, k_a, r_k, ln_g, ln_b, w_out, ffn_gain, w_gate, w_up, w_down, rel_table):
    bsz, seq, d = h.shape
    nsa_cols = NSA_WIDTH + 6 * KV_WIDTH + 3 * NSA_HEADS
    gate0 = NSA_WIDTH + 6 * KV_WIDTH

    w_nsa, w_rw = w_in[:, :nsa_cols], w_in[:, nsa_cols:]
    gate_src = np.array([gate0 + h_ * 3 + br for br in range(3) for h_ in range(NSA_HEADS)])
    w_gates = jnp.zeros((d, LANES), F32).at[:, :3 * NSA_HEADS].set(w_nsa[:, gate_src])
    lora0 = 3 * RWKV_WIDTH
    g0 = lora0 + DECAY_LORA + ICLR_LORA
    w_g = jnp.zeros((d, GATE_LORA_PAD), F32).at[:, :GATE_LORA].set(w_rw[:, g0:g0 + GATE_LORA])
    widths = (NSA_WIDTH, 6 * KV_WIDTH, LANES, 3 * RWKV_WIDTH, DECAY_LORA + ICLR_LORA, GATE_LORA_PAD)
    w_cat = jnp.concatenate([w_nsa[:, :gate0], w_gates, w_rw[:, :g0], w_g], axis=1).astype(BF16)
    mix_g = jnp.zeros((GATE_LORA_PAD,), F32).at[:GATE_LORA].set(shift_mix[g0:g0 + GATE_LORA])
    g2p = jnp.zeros((GATE_LORA_PAD, RWKV_WIDTH), F32).at[:GATE_LORA].set(g2)

    x2 = h.reshape(bsz * seq, d)
    q, kv, gates, zr, zl, zg = _in_projection(x2, attn_gain.reshape(1, d), w_cat, widths)
    r3 = lambda u: u.reshape(bsz, seq, u.shape[-1])
    y_nsa = _nsa_mixer(r3(q), r3(kv), r3(gates), q_gain, k_gain, cmp_pe, cmp_w1, cmp_b1, cmp_w2, cmp_b2,
                       out_gain, rel_table)
    y_rwkv = _rwkv_mixer(r3(zr), r3(zl), r3(zg), shift_mix[:lora0], shift_mix[lora0:g0], mix_g,
                         w0, w2, a0, a2, g2p, k_k, k_a, r_k.reshape(-1), ln_g, ln_b)
    out = _out_ffn(x2, y_nsa.reshape(bsz * seq, NSA_WIDTH), y_rwkv.reshape(bsz * seq, RWKV_WIDTH),
                   w_out, ffn_gain, w_gate, w_up, w_down)
    return out.reshape(bsz, seq, d)


def kernel(x, attn_norm_gain, w_in, nsa_q_gain, nsa_k_gain, cmp_pe, cmp_w1, cmp_b1, cmp_w2, cmp_b2, nsa_out_gain, rwkv_shift_mix, rwkv_w0, rwkv_w2, rwkv_a0, rwkv_a2, rwkv_g2, rwkv_k_k, rwkv_k_a, rwkv_r_k, rwkv_ln_gain, rwkv_ln_bias, w_out, ffn_norm_gain, w_gate, w_up, w_down, rel_bias_table):
    h = x
    for l in range(attn_norm_gain.shape[0]):
        h = _layer(h, attn_norm_gain[l], w_in[l], nsa_q_gain[l], nsa_k_gain[l], cmp_pe[l], cmp_w1[l], cmp_b1[l],
                   cmp_w2[l], cmp_b2[l], nsa_out_gain[l], rwkv_shift_mix[l], rwkv_w0[l], rwkv_w2[l], rwkv_a0[l],
                   rwkv_a2[l], rwkv_g2[l], rwkv_k_k[l], rwkv_k_a[l], rwkv_r_k[l], rwkv_ln_gain[l],
                   rwkv_ln_bias[l], w_out[l], ffn_norm_gain[l], w_gate[l], w_up[l], w_down[l], rel_bias_table)
    return h
```

```python
import functools
import math

import numpy as np
import jax
import jax.numpy as jnp
from jax import lax
from jax.experimental import pallas as pl
from jax.experimental.pallas import tpu as pltpu

HEAD_DIM = 64
NSA_HEADS = 8
NSA_KV_HEADS = 2
NSA_HPG = NSA_HEADS // NSA_KV_HEADS
NSA_WIDTH = NSA_HEADS * HEAD_DIM
KV_WIDTH = NSA_KV_HEADS * HEAD_DIM
CMP_BLOCK = 32
CMP_STRIDE = 16
CMP_HIDDEN = 256
SEL_BLOCK = 64
SEL_TOP_N = 16
SEL_LOCAL = 2
WINDOW = 512
ATTN_SCALE = HEAD_DIM ** -0.5
LOG2E = math.log2(math.e)
NEG_INF = -1e30
FORCE_SCORE = 1e9
REL_BUCKETS = 32
REL_MAX_DIST = 128
RWKV_HEADS = 8
RWKV_HEAD_DIM = 64
RWKV_WIDTH = RWKV_HEADS * RWKV_HEAD_DIM
DECAY_LORA = 64
ICLR_LORA = 64
GATE_LORA = 160
GATE_LORA_PAD = 256
GN_EPS = 64e-5
RMS_EPS = 1e-6

LANES = 128
Q_TILE = 128
TILES_PER_STEP = 2
V_ROWS = HEAD_DIM + 16
K_TILE = 128
WKV_TILE = 256
WKV_CHUNK = 64
WKV_BATCH = 2
ROW_TILE = 512
FF_CHUNK = 256
VMEM_LIMIT_BYTES = 56 * 1024 * 1024

F32 = jnp.float32
BF16 = jnp.bfloat16


def _dot(a, b):
    return jnp.dot(a.astype(BF16), b.astype(BF16), preferred_element_type=F32)


def _dot_nt(a, b):
    return lax.dot_general(a.astype(BF16), b.astype(BF16), (((1,), (1,)), ((), ())),
                           preferred_element_type=F32)


def _dot_tn(a, b):
    return lax.dot_general(a.astype(BF16), b.astype(BF16), (((0,), (0,)), ((), ())),
                           preferred_element_type=F32)


def _split3(x):
    hi = x.astype(BF16)
    r1 = x - hi.astype(F32)
    mid = r1.astype(BF16)
    lo = (r1 - mid.astype(F32)).astype(BF16)
    return hi, mid, lo


def _dot_exact_rhs(x, m_bf16):
    hi, mid, lo = _split3(x)
    acc = jnp.dot(lo, m_bf16, preferred_element_type=F32)
    acc = acc + jnp.dot(mid, m_bf16, preferred_element_type=F32)
    return acc + jnp.dot(hi, m_bf16, preferred_element_type=F32)


def _dot_exact_lhs(m_bf16, x):
    hi, mid, lo = _split3(x)
    acc = jnp.dot(m_bf16, lo, preferred_element_type=F32)
    acc = acc + jnp.dot(m_bf16, mid, preferred_element_type=F32)
    return acc + jnp.dot(m_bf16, hi, preferred_element_type=F32)


def _rel_bucket_np(dist):
    max_exact = REL_BUCKETS // 2
    d = np.maximum(dist, 0)
    ratio = np.maximum(d, 1).astype(np.float32) / np.float32(max_exact)
    log_ratio = np.log(ratio).astype(np.float32) / np.float32(math.log(REL_MAX_DIST / max_exact))
    large = np.minimum(max_exact + (log_ratio * np.float32(REL_BUCKETS - max_exact)).astype(np.int32),
                       REL_BUCKETS - 1)
    return np.where(d < max_exact, d, large).astype(np.int32)


def _inproj_kernel(x_ref, gain_ref, w_ref, qg_ref, kg_ref, mr_ref, ml_ref, mg_ref,
                   q_ref, kv_ref, gt_ref, zr_ref, zl_ref, zg_ref,
                   cr_ref, cl_ref, cg_ref, *, widths, tiles_per_seq):
    i = pl.program_id(0)
    tm = x_ref.shape[0]
    x = x_ref[...]
    ms = jnp.mean(x * x, axis=-1, keepdims=True)
    xn = (x * lax.rsqrt(ms + RMS_EPS) * gain_ref[...]).astype(BF16)
    offs = np.concatenate([[0], np.cumsum(widths)])
    seg = lambda k: jnp.dot(xn, w_ref[:, int(offs[k]):int(offs[k + 1])], preferred_element_type=F32)
    lo = lax.broadcasted_iota(jnp.int32, (1, LANES), 1) < HEAD_DIM

    def slab_rms(u, gain):
        s_lo = jnp.sum(jnp.where(lo, u * u, 0.0), axis=-1, keepdims=True)
        s_hi = jnp.sum(jnp.where(lo, 0.0, u * u), axis=-1, keepdims=True)
        inv = lambda s: lax.rsqrt(s * (1.0 / HEAD_DIM) + RMS_EPS)
        return u * jnp.where(lo, inv(s_lo), inv(s_hi)) * gain

    q = seg(0)
    for p in range(NSA_WIDTH // LANES):
        sl = slice(p * LANES, (p + 1) * LANES)
        q_ref[:, sl] = slab_rms(q[:, sl], qg_ref[...]) * (ATTN_SCALE * LOG2E)
    kv = seg(1)
    for b in range(6):
        sl = slice(b * KV_WIDTH, (b + 1) * KV_WIDTH)
        kv_ref[:, sl] = slab_rms(kv[:, sl], kg_ref[b // 2 - 1:b // 2, :]) if b in (2, 4) else kv[:, sl]
    gt_ref[...] = seg(2)

    @pl.when(i == 0)
    def _init_carries():
        for carry_ref in (cr_ref, cl_ref, cg_ref):
            carry_ref[...] = jnp.zeros_like(carry_ref)

    first = (i % tiles_per_seq) == 0
    row = lax.broadcasted_iota(jnp.int32, (tm, 1), 0)
    for k, z_ref, mix_ref, carry_ref in ((3, zr_ref, mr_ref, cr_ref), (4, zl_ref, ml_ref, cl_ref),
                                         (5, zg_ref, mg_ref, cg_ref)):
        z = seg(k)
        carry = jnp.where(first, 0.0, carry_ref[...])
        prev = jnp.where(row == 0, carry, pltpu.roll(z, 1, 0))
        carry_ref[...] = z[tm - 1:tm, :]
        z_ref[...] = z + mix_ref[...] * (prev - z)


def _in_projection(x2, gain, w_cat, widths, seq, q_gain, k_gain, mix_r, mix_l, mix_g):
    n, d = x2.shape
    tm = min(ROW_TILE, n)
    assert seq % tm == 0 and KV_WIDTH == LANES
    total = sum(widths)
    full = lambda shape: pl.BlockSpec(shape, lambda i: (0, 0))
    row = lambda a: a.reshape(1, -1)
    return pl.pallas_call(
        functools.partial(_inproj_kernel, widths=widths, tiles_per_seq=seq // tm),
        grid=(n // tm,),
        in_specs=[
            pl.BlockSpec((tm, d), lambda i: (i, 0)),
            full((1, d)), full((d, total)),
            full((1, LANES)), full((2, LANES)),
            full((1, widths[3])), full((1, widths[4])), full((1, widths[5])),
        ],
        out_specs=[pl.BlockSpec((tm, w), lambda i: (i, 0)) for w in widths],
        out_shape=[jax.ShapeDtypeStruct((n, w), F32) for w in widths],
        scratch_shapes=[pltpu.VMEM((1, w), F32) for w in widths[3:6]],
        compiler_params=pltpu.CompilerParams(
            dimension_semantics=("arbitrary",), vmem_limit_bytes=VMEM_LIMIT_BYTES),
        name="in_projection",
    )(x2, gain, w_cat, jnp.tile(q_gain, 2).reshape(1, LANES), jnp.tile(k_gain[1:3], (1, 2)),
      row(mix_r), row(mix_l), row(mix_g))


def _head_rms(u, gain):
    ms = jnp.mean(u * u, axis=-1, keepdims=True)
    return u * lax.rsqrt(ms + RMS_EPS) * gain


def _nsa_kernel(tab_ref,
                q_ref, kv_ref, gate_ref,
                kg_ref, pe_ref, w1_ref, b1_ref, w2_ref, b2_ref, og_ref,
                bk0_ref, bk1_ref, bkc_ref, ovl_ref,
                out_ref,
                kvpad_ref, kcmp_ref, vcmp_ref, ksel_ref, vsel_ref, kwin_ref, vwin_ref,
                btab_ref, ccan_ref, ybuf_ref,
                *, seq):
    b = pl.program_id(0)
    step = pl.program_id(1)
    n_cb = seq // CMP_STRIDE
    n_c = (seq - CMP_BLOCK) // CMP_STRIDE + 1
    n_sel = seq // SEL_BLOCK
    cols = NSA_HPG * Q_TILE

    @pl.when((b == 0) & (step == 0))
    def _build_bias_tables():
        bk0 = bk0_ref[...]
        bk1 = bk1_ref[...]
        bkc = bkc_ref[...]
        kk = lax.broadcasted_iota(jnp.int32, (K_TILE, Q_TILE), 0)
        qq = lax.broadcasted_iota(jnp.int32, (K_TILE, Q_TILE), 1)
        causal = jnp.where(kk <= qq, 0.0, NEG_INF).astype(F32)
        anti = jnp.where(kk > qq, 0.0, NEG_INF).astype(F32)
        for h in range(NSA_HEADS):
            t0 = jnp.zeros((K_TILE, Q_TILE), F32)
            t1 = jnp.zeros((K_TILE, Q_TILE), F32)
            tc = jnp.zeros((2 * LANES, Q_TILE), F32)
            for k in range(REL_BUCKETS):
                val = tab_ref[k, h]
                t0 = jnp.where(bk0 == k, val, t0)
                t1 = jnp.where(bk1 == k, val, t1)
                tc = jnp.where(bkc == k, val, tc)
            far = tab_ref[REL_BUCKETS - 1, h]
            g, p = divmod(h, NSA_HPG)
            cs = slice(p * Q_TILE, (p + 1) * Q_TILE)
            btab_ref[g, 0, :, cs] = jnp.zeros((K_TILE, Q_TILE), F32)
            btab_ref[g, 1, :, cs] = (t1 - far) * LOG2E
            btab_ref[g, 2, :, cs] = (t0 - far) * LOG2E + causal
            btab_ref[g, 3, :, cs] = anti
            btab_ref[g, 4, :, cs] = jnp.full((K_TILE, Q_TILE), NEG_INF, F32)
            ccan_ref[h] = tc * LOG2E
        blk = lax.broadcasted_iota(jnp.int32, (seq, HEAD_DIM), 0) // SEL_BLOCK
        col = lax.broadcasted_iota(jnp.int32, (seq, HEAD_DIM), 1)
        onehot = (blk == col).astype(BF16)
        padcol = (lax.broadcasted_iota(jnp.int32, (WINDOW, 2 * HEAD_DIM), 1) == HEAD_DIM).astype(BF16)
        for g in range(NSA_KV_HEADS):
            ksel_ref[g, :, HEAD_DIM:2 * HEAD_DIM] = onehot
            kwin_ref[g, 0:WINDOW, :] = padcol
            kwin_ref[g, WINDOW:WINDOW + seq, HEAD_DIM:2 * HEAD_DIM] = jnp.zeros((seq, HEAD_DIM), BF16)
            vwin_ref[g, 0:HEAD_DIM, 0:WINDOW] = jnp.zeros((HEAD_DIM, WINDOW), BF16)
            for vt_ref, width in ((vsel_ref, seq), (vwin_ref, WINDOW + seq)):
                ones_row = lax.broadcasted_iota(jnp.int32, (V_ROWS - HEAD_DIM, width), 0) == 0
                vt_ref[g, HEAD_DIM:V_ROWS, :] = ones_row.astype(BF16)

    @pl.when(step == 0)
    def _per_batch():
        for which in range(2):
            kvpad_ref[which, seq:seq + 2 * CMP_STRIDE, :] = jnp.zeros((2 * CMP_STRIDE, KV_WIDTH), F32)
            kvpad_ref[which, 0:seq, :] = kv_ref[0, :, which * KV_WIDTH:(which + 1) * KV_WIDTH]
            first = lax.broadcasted_iota(jnp.int32, (1, KV_WIDTH), 1) < HEAD_DIM
            slabs = [[], []]
            for l in range(0, CMP_BLOCK, 2):
                ra = kvpad_ref[which, pl.ds(l, n_cb, stride=CMP_STRIDE), :]
                rb = kvpad_ref[which, pl.ds(l + 1, n_cb, stride=CMP_STRIDE), :]
                slabs[0].append(jnp.where(first, ra, pltpu.roll(rb, HEAD_DIM, 1)))
                slabs[1].append(jnp.where(first, pltpu.roll(ra, HEAD_DIM, 1), rb))
            cmps = []
            for g in range(NSA_KV_HEADS):
                blk = jnp.concatenate(slabs[g], axis=1) + pe_ref[which]
                hid = jax.nn.gelu(_dot(blk, w1_ref[which]) + b1_ref[which], approximate=True)
                cmps.append(_dot(hid, w2_ref[which]) + b2_ref[which])
            if which == 0:
                for g in range(NSA_KV_HEADS):
                    kcmp_ref[g] = _head_rms(cmps[g], kg_ref[0:1, :]).astype(BF16)
            else:
                vct = jnp.concatenate(cmps, axis=1).T
                for g in range(NSA_KV_HEADS):
                    vcmp_ref[g] = vct[g * HEAD_DIM:(g + 1) * HEAD_DIM].astype(BF16)
        kv = kv_ref[0]
        for g in range(NSA_KV_HEADS):
            def col(i):
                c = (2 + i) * KV_WIDTH + g * HEAD_DIM
                return kv[:, c:c + HEAD_DIM]
            ksel_ref[g, :, 0:HEAD_DIM] = col(0).astype(BF16)
            kwin_ref[g, WINDOW:WINDOW + seq, 0:HEAD_DIM] = col(2).astype(BF16)
        for i, vt_ref, pad in ((1, vsel_ref, 0), (3, vwin_ref, WINDOW)):
            vt = kv[:, (2 + i) * KV_WIDTH:(3 + i) * KV_WIDTH].T
            for g in range(NSA_KV_HEADS):
                vt_ref[g, 0:HEAD_DIM, pad:pad + seq] = vt[g * HEAD_DIM:(g + 1) * HEAD_DIM].astype(BF16)

    n_selp = -(-n_sel // 8) * 8
    qT_all = q_ref[0].T
    gates_all = jax.nn.sigmoid(gate_ref[0].T)
    tiles = range(TILES_PER_STEP)
    groups = range(NSA_KV_HEADS)
    units = [(j, g) for j in tiles for g in groups]
    each = lambda f: {u: f(*u) for u in units}
    qts = [TILES_PER_STEP * step + j for j in tiles]
    tqs = [qts[j] * Q_TILE + lax.broadcasted_iota(jnp.int32, (1, Q_TILE), 1) for j in tiles]
    lanes = lambda j: slice(j * Q_TILE, (j + 1) * Q_TILE)

    def q_group(j, g):
        qs = [qT_all[h * HEAD_DIM:(h + 1) * HEAD_DIM, lanes(j)].astype(BF16)
              for h in range(g * NSA_HPG, (g + 1) * NSA_HPG)]
        return jnp.concatenate(qs, axis=1)

    q_cats = each(q_group)
    aug_row = lax.broadcasted_iota(jnp.int32, (HEAD_DIM, cols), 0)

    cc = lax.broadcasted_iota(jnp.int32, (n_cb, Q_TILE), 0)
    valid4 = [jnp.concatenate([(cc * CMP_STRIDE + (CMP_BLOCK - 1) <= tqs[j]) & (cc < n_c)] * NSA_HPG, axis=1)
              for j in tiles]
    c_starts = [pl.multiple_of(LANES - (CMP_STRIDE // 2) * qts[j], 8) for j in tiles]
    ss = each(lambda j, g: jnp.dot(kcmp_ref[g], q_cats[j, g], preferred_element_type=F32))
    ss = each(lambda j, g: jnp.where(valid4[j], ss[j, g] + jnp.concatenate(
        [ccan_ref[g * NSA_HPG + p, pl.ds(c_starts[j], n_cb), :] for p in range(NSA_HPG)], axis=1), NEG_INF))
    es = each(lambda j, g: jnp.exp2(ss[j, g] - jnp.max(ss[j, g], axis=0, keepdims=True)))
    pcs = each(lambda j, g: es[j, g] * (1.0 / jnp.sum(es[j, g], axis=0, keepdims=True))
               * valid4[j].astype(F32))
    o_cmps = each(lambda j, g: jnp.dot(vcmp_ref[g], pcs[j, g].astype(BF16), preferred_element_type=F32))
    psums = each(lambda j, g: functools.reduce(
        lambda u, w: u + w, [pcs[j, g][:, p * Q_TILE:(p + 1) * Q_TILE] for p in range(NSA_HPG)]))

    n_wb = WINDOW // K_TILE
    band = WINDOW + K_TILE
    band_kinds = (3,) + (0,) * (n_wb - 2) + (1, 2)
    w0s = [pl.multiple_of(qts[j] * K_TILE, K_TILE) for j in tiles]
    pad_rows = jnp.where(aug_row == 0, NEG_INF, 0.0).astype(BF16)

    def band_logits(g, s):
        blocks = []
        for i, kind in enumerate(band_kinds):
            si = s[i * K_TILE:(i + 1) * K_TILE]
            blocks.append(si if kind == 0 else si + btab_ref[g, kind])
        return jnp.concatenate(blocks, axis=0)

    sw = each(lambda j, g: jnp.dot(kwin_ref[g, pl.ds(w0s[j], band), :],
                                   jnp.concatenate([q_cats[j, g], pad_rows], axis=0),
                                   preferred_element_type=F32))
    sw = each(lambda j, g: band_logits(g, sw[j, g]))
    pws = each(lambda j, g: jnp.exp2(sw[j, g] - jnp.max(sw[j, g], axis=0, keepdims=True)))
    o_wins = each(lambda j, g: jnp.dot(vwin_ref[g, :, pl.ds(w0s[j], band)], pws[j, g].astype(BF16),
                                       preferred_element_type=F32))
    o_wins = each(lambda j, g: o_wins[j, g][0:HEAD_DIM] * (1.0 / o_wins[j, g][HEAD_DIM:HEAD_DIM + 1]))

    imps = each(lambda j, g: _dot_exact_lhs(ovl_ref[...], psums[j, g])[0:n_selp])
    jb = lax.broadcasted_iota(jnp.int32, (n_selp, Q_TILE), 0)

    def masked_score(j, g):
        qb = tqs[j] // SEL_BLOCK
        forced = (jb == 0) | ((jb <= qb) & (jb > qb - SEL_LOCAL))
        score = jnp.where(forced, FORCE_SCORE, imps[j, g])
        score = jnp.where(jb <= qb, score, -1.0)
        return jnp.where(jb < n_sel, score, -2.0)

    scores = each(masked_score)
    sub = 8
    sub_row = lax.broadcasted_iota(jnp.int32, (sub, Q_TILE), 0)
    score_rows = {u: [scores[u][r0:r0 + sub] for r0 in range(0, n_selp, sub)] for u in units}
    rank_rows = {u: [jnp.zeros((sub, Q_TILE), F32) for _ in range(0, n_selp, sub)] for u in units}
    for i in range(n_sel):
        for u in units:
            si = scores[u][i:i + 1, :]
            for v, s_v in enumerate(score_rows[u]):
                r0 = v * sub
                if r0 > i:
                    ahead = si >= s_v
                elif r0 + sub - 1 < i:
                    ahead = si > s_v
                else:
                    ahead = (si > s_v) | ((si == s_v) & (sub_row + r0 > i))
                rank_rows[u][v] = rank_rows[u][v] + ahead.astype(F32)
    ranks = {u: jnp.concatenate(rank_rows[u], axis=0) for u in units}

    def q_with_mask_rows(j, g):
        sel = (ranks[j, g] < float(min(SEL_TOP_N, n_sel))) & (jb < n_sel)
        nm = (sel.astype(F32) - 1.0) * (-NEG_INF)
        return jnp.concatenate([q_cats[j, g], jnp.concatenate([nm] * NSA_HPG, axis=1).astype(BF16),
                                jnp.zeros((HEAD_DIM - n_selp, cols), BF16)], axis=0)

    q_sels = each(q_with_mask_rows)

    pair = 2 * K_TILE

    n_pairs = qts[-1] // 2 + 1
    n_far = jnp.maximum((qts[0] - 1) // 2, 0)

    def sel_step(kp, carry, near):
        k0 = pl.multiple_of(kp * pair, pair)
        old = {u: carry[i] for i, u in enumerate(units)}

        def logits(j, g):
            s = jnp.dot(ksel_ref[g, pl.ds(k0, pair), :], q_sels[j, g], preferred_element_type=F32)
            if near:
                rel = qts[j] - 2 * kp
                kind_of = lambda d: jnp.where(d >= 2, 0, jnp.where(d == 1, 1, jnp.where(d == 0, 2, 4)))
                kind_a, kind_b = kind_of(rel), kind_of(rel - 1)
                s = s + jnp.concatenate([btab_ref[g, kind_a], btab_ref[g, kind_b]], axis=0)
            return s

        ss = each(logits)
        m_new = each(lambda j, g: jnp.maximum(old[j, g][0], jnp.max(ss[j, g], axis=0, keepdims=True)))
        ps = each(lambda j, g: jnp.exp2(ss[j, g] - m_new[j, g]))
        alpha = each(lambda j, g: jnp.exp2(old[j, g][0] - m_new[j, g]))
        pv = each(lambda j, g: jnp.dot(vsel_ref[g, :, pl.ds(k0, pair)], ps[j, g].astype(BF16),
                                       preferred_element_type=F32))
        return tuple((m_new[u], alpha[u] * old[u][1] + pv[u]) for u in units)

    init = (jnp.full((1, cols), NEG_INF, F32), jnp.zeros((V_ROWS, cols), F32))
    sel_mid = lax.fori_loop(0, n_far, functools.partial(sel_step, near=False), tuple(init for _ in units))
    sel_out = lax.fori_loop(n_far, n_pairs, functools.partial(sel_step, near=True), sel_mid)

    ssqs = []
    for j in tiles:
        gates = gates_all[:, lanes(j)]
        ssq = jnp.zeros((1, Q_TILE), F32)
        for g in groups:
            _, acc_s = sel_out[units.index((j, g))]
            o_sel = acc_s[0:HEAD_DIM] * (1.0 / acc_s[HEAD_DIM:HEAD_DIM + 1])
            for p in range(NSA_HPG):
                h = g * NSA_HPG + p
                cs = slice(p * Q_TILE, (p + 1) * Q_TILE)
                o = (gates[h:h + 1] * o_cmps[j, g][:, cs]
                     + gates[NSA_HEADS + h:NSA_HEADS + h + 1] * o_sel[:, cs]
                     + gates[2 * NSA_HEADS + h:2 * NSA_HEADS + h + 1] * o_wins[j, g][:, cs])
                ssq = ssq + jnp.sum(o * o, axis=0, keepdims=True)
                ybuf_ref[h * HEAD_DIM:(h + 1) * HEAD_DIM, lanes(j)] = o
        ssqs.append(ssq)
    ssq_all = jnp.concatenate(ssqs, axis=1)
    yT = ybuf_ref[...] * lax.rsqrt(ssq_all / NSA_WIDTH + RMS_EPS) * og_ref[...]
    out_ref[0] = yT.T.astype(out_ref.dtype)


def _nsa_mixer(q, kv, gates, cmp_k_gain, cmp_pe, cmp_w1, cmp_b1, cmp_w2, cmp_b2, out_gain, rel_table):
    bsz, seq, _ = q.shape
    assert TILES_PER_STEP % 2 == 0 and seq % (TILES_PER_STEP * Q_TILE) == 0
    assert seq // CMP_STRIDE == LANES and seq // SEL_BLOCK <= HEAD_DIM
    n_steps = seq // (TILES_PER_STEP * Q_TILE)
    q_rows = TILES_PER_STEP * Q_TILE
    n_cb = seq // CMP_STRIDE
    n_sel = seq // SEL_BLOCK
    n_selp = -(-n_sel // 8) * 8
    cols = NSA_HPG * Q_TILE

    kk = np.arange(K_TILE)[:, None]
    qq = np.arange(Q_TILE)[None, :]
    bk0 = _rel_bucket_np(qq - kk)
    bk1 = _rel_bucket_np(K_TILE + qq - kk)
    cprime = np.arange(2 * LANES)[:, None] - LANES
    bkc = _rel_bucket_np(qq - (CMP_BLOCK - 1) - CMP_STRIDE * cprime)
    c_start = np.arange(n_cb)[None, :] * CMP_STRIDE
    j_start = np.arange(LANES)[:, None] * SEL_BLOCK
    ovl = np.clip(np.minimum(c_start + CMP_BLOCK, j_start + SEL_BLOCK) - np.maximum(c_start, j_start), 0, None)
    ovl = (ovl.astype(np.float32) / CMP_STRIDE)
    ovl[n_sel:, :] = 0.0

    assert NSA_KV_HEADS == 2 and KV_WIDTH == LANES
    w1 = cmp_w1.astype(BF16)
    full = lambda shape: pl.BlockSpec(shape, lambda b, t: (0,) * len(shape))
    kernel = functools.partial(_nsa_kernel, seq=seq)
    return pl.pallas_call(
        kernel,
        grid=(bsz, n_steps),
        in_specs=[
            pl.BlockSpec(memory_space=pltpu.SMEM),
            pl.BlockSpec((1, q_rows, NSA_WIDTH), lambda b, t: (b, t, 0)),
            pl.BlockSpec((1, seq, 6 * KV_WIDTH), lambda b, t: (b, 0, 0)),
            pl.BlockSpec((1, q_rows, LANES), lambda b, t: (b, t, 0)),
            full((1, HEAD_DIM)),
            full((2, 1, CMP_BLOCK * HEAD_DIM)),
            full((2, CMP_BLOCK * HEAD_DIM, CMP_HIDDEN)),
            full((2, 1, CMP_HIDDEN)),
            full((2, CMP_HIDDEN, HEAD_DIM)),
            full((2, 1, HEAD_DIM)),
            full((NSA_WIDTH, q_rows)),
            full((K_TILE, Q_TILE)),
            full((K_TILE, Q_TILE)),
            full((2 * LANES, Q_TILE)),
            full((LANES, n_cb)),
        ],
        out_specs=pl.BlockSpec((1, q_rows, NSA_WIDTH), lambda b, t: (b, t, 0)),
        out_shape=jax.ShapeDtypeStruct((bsz, seq, NSA_WIDTH), BF16),
        scratch_shapes=[
            pltpu.VMEM((2, seq + 2 * CMP_STRIDE, KV_WIDTH), F32),
            pltpu.VMEM((NSA_KV_HEADS, n_cb, HEAD_DIM), BF16),
            pltpu.VMEM((NSA_KV_HEADS, HEAD_DIM, n_cb), BF16),
            pltpu.VMEM((NSA_KV_HEADS, seq, 2 * HEAD_DIM), BF16),
            pltpu.VMEM((NSA_KV_HEADS, V_ROWS, seq), BF16),
            pltpu.VMEM((NSA_KV_HEADS, WINDOW + seq, 2 * HEAD_DIM), BF16),
            pltpu.VMEM((NSA_KV_HEADS, V_ROWS, WINDOW + seq), BF16),
            pltpu.VMEM((NSA_KV_HEADS, 5, K_TILE, cols), F32),
            pltpu.VMEM((NSA_HEADS, 2 * LANES, Q_TILE), F32),
            pltpu.VMEM((NSA_WIDTH, q_rows), F32),
        ],
        compiler_params=pltpu.CompilerParams(
            dimension_semantics=("arbitrary", "arbitrary"), vmem_limit_bytes=VMEM_LIMIT_BYTES),
        name="nsa_mixer",
    )(rel_table, q, kv, gates, cmp_k_gain.reshape(1, HEAD_DIM),
      cmp_pe.reshape(2, 1, CMP_BLOCK * HEAD_DIM), w1,
      cmp_b1.reshape(2, 1, CMP_HIDDEN), cmp_w2.astype(BF16), cmp_b2.reshape(2, 1, HEAD_DIM),
      jnp.broadcast_to(out_gain[:, None], (NSA_WIDTH, q_rows)), jnp.asarray(bk0), jnp.asarray(bk1),
      jnp.asarray(bkc), jnp.asarray(ovl, BF16))


def _rwkv_kernel(zr_ref, zl_ref, zg_ref, w0_ref, w2_ref, a0_ref, a2_ref, g2_ref,
                 kk_ref, ka_ref, rk_ref, lng_ref, lnb_ref, tri_ref,
                 out_ref,
                 state_ref):
    t = pl.program_id(1)
    tt = WKV_TILE
    n = RWKV_HEAD_DIM
    nb = zr_ref.shape[0]
    nck_seq = tt // WKV_CHUNK
    nck = nb * nck_seq

    @pl.when(t == 0)
    def _reset():
        state_ref[...] = jnp.zeros_like(state_ref)

    stacked = lambda z_ref: jnp.concatenate([z_ref[i] for i in range(nb)], axis=0)
    zr = stacked(zr_ref)
    zl = stacked(zl_ref)
    zg = stacked(zg_ref)
    r = zr[:, 0:RWKV_WIDTH]
    k = zr[:, RWKV_WIDTH:2 * RWKV_WIDTH]
    v = zr[:, 2 * RWKV_WIDTH:3 * RWKV_WIDTH]

    xw = w0_ref[...] + _dot(jnp.tanh(zl[:, 0:DECAY_LORA]), w2_ref[...])
    lw = -math.exp(-0.5) * jax.nn.sigmoid(xw)
    a = jax.nn.sigmoid(a0_ref[...] + _dot(zl[:, DECAY_LORA:DECAY_LORA + ICLR_LORA], a2_ref[...]))
    gate = _dot(jax.nn.sigmoid(zg), g2_ref[...])

    cum = jnp.concatenate([_dot_exact_lhs(tri_ref[...], lw[i * tt:(i + 1) * tt]) for i in range(nb)],
                          axis=0)
    cum_end = cum.reshape(nck, WKV_CHUNK, RWKV_WIDTH)[:, WKV_CHUNK - 1:WKV_CHUNK, :]
    cum_end_b = jnp.broadcast_to(cum_end, (nck, WKV_CHUNK, RWKV_WIDTH)).reshape(nb * tt, RWKV_WIDTH)
    e_incl = jnp.exp(cum)
    e_excl = jnp.exp(cum - lw)
    e_neg = jnp.exp(-cum)
    e_rev = jnp.exp(cum_end_b - cum)
    g_end = jnp.exp(cum_end)

    cw = WKV_CHUNK
    lane = lax.broadcasted_iota(jnp.int32, (1, 2 * n), 1)
    lo = lane < n
    ri = lax.broadcasted_iota(jnp.int32, (cw, 4 * n), 0)
    ci = lax.broadcasted_iota(jnp.int32, (cw, 4 * n), 1) % n
    strict = ri > ci
    incl = ri >= ci
    eye_row = (ri == ci).astype(F32)
    i2 = lax.broadcasted_iota(jnp.int32, (2 * n, 2 * n), 0)
    j2 = lax.broadcasted_iota(jnp.int32, (2 * n, 2 * n), 1)
    same_head = (i2 // n) == (j2 // n)
    eye_2n = (i2 == j2).astype(F32)
    zeros_slab = jnp.zeros((cw, 2 * n), BF16)

    def seg_sum(u, f=lambda s: s):
        s_lo = jnp.sum(jnp.where(lo, u, 0.0), axis=-1, keepdims=True)
        s_hi = jnp.sum(jnp.where(lo, 0.0, u), axis=-1, keepdims=True)
        return jnp.where(lo, f(s_lo), f(s_hi))

    def block_diag(u):
        ua, ub = u[:, 0:2 * n], u[:, 2 * n:4 * n]
        parts = []
        for w, first in ((ua, True), (ub, False)):
            for keep_lo in (True, False):
                blk = jnp.where(lo, w, 0.0) if keep_lo else jnp.where(lo, 0.0, w)
                blk = blk.astype(BF16)
                parts.append(jnp.concatenate([blk, zeros_slab] if first else [zeros_slab, blk], axis=1))
        return jnp.concatenate(parts, axis=0)

    rows = lambda c: slice(c * cw, (c + 1) * cw)

    n_pairs = RWKV_HEADS // 2
    slab = lambda p: slice(2 * n * p, 2 * n * (p + 1))
    pre = []
    for p in range(n_pairs):
        sl = slab(p)
        rp, kp, vp, ap = r[:, sl], k[:, sl], v[:, sl], a[:, sl]
        kkh = kp * kk_ref[:, sl]
        kkn = kkh * seg_sum(kkh * kkh, lambda s: lax.rsqrt(jnp.maximum(s, 1e-24)))
        kmod = kp * (1.0 + (ap - 1.0) * ka_ref[:, sl])
        bv = kkn * ap
        a_t = -kkn * e_excl[:, sl]
        r_t = rp * e_incl[:, sl]
        b_hat = bv * e_neg[:, sl]
        k_hat = kmod * e_neg[:, sl]
        mms = []
        for c in range(nck):
            bh, kh = b_hat[rows(c)], k_hat[rows(c)]
            rhs = jnp.concatenate([jnp.where(lo, bh, 0.0), jnp.where(lo, 0.0, bh),
                                   jnp.where(lo, kh, 0.0), jnp.where(lo, 0.0, kh)], axis=0)
            mms.append(_dot_nt(jnp.concatenate([a_t[rows(c)], r_t[rows(c)]], axis=0), rhs))
        pre.append(dict(r=rp, v=vp, kmod=kmod, a_t=a_t, r_t=r_t, b_end=bv * e_rev[:, sl],
                        k_end=kmod * e_rev[:, sl], mms=mms))

    chains = [(p, c0) for p in range(n_pairs) for c0 in range(0, nck, 2)]

    def pair_row(c0, f):
        return jnp.concatenate([f(c0), f(c0 + 1)], axis=1)

    quads = []
    for p, c0 in chains:
        mms = pre[p]["mms"]
        quads.append((
            jnp.where(strict, pair_row(c0, lambda c: mms[c][0:cw, 0:2 * n]), 0.0),
            jnp.where(strict, pair_row(c0, lambda c: mms[c][0:cw, 2 * n:4 * n]), 0.0),
            jnp.where(incl, pair_row(c0, lambda c: mms[c][cw:2 * cw, 0:2 * n]), 0.0),
            jnp.where(incl, pair_row(c0, lambda c: mms[c][cw:2 * cw, 2 * n:4 * n]), 0.0)))
    tinvs = [eye_row + q[0] for q in quads]
    xs = [_dot(q[0], block_diag(q[0])) for q in quads]
    for _ in range(int(math.log2(cw)) - 2):
        txs = [_dot(jnp.concatenate([t_, x_], axis=0), block_diag(x_)) for t_, x_ in zip(tinvs, xs)]
        tinvs = [t_ + tx[0:cw] for t_, tx in zip(tinvs, txs)]
        xs = [tx[cw:2 * cw] for tx in txs]
    tinvs = [t_ + _dot(t_, block_diag(x_)) for t_, x_ in zip(tinvs, xs)]
    lvs = [_dot(jnp.concatenate([q[1], q[3]], axis=0),
                block_diag(pair_row(c0, lambda c: pre[p]["v"][rows(c)])))
           for (p, c0), q in zip(chains, quads)]
    w_rows = [_dot(t_, block_diag(pair_row(c0, lambda c: pre[p]["a_t"][rows(c)])))
              for (p, c0), t_ in zip(chains, tinvs)]
    u_rows = [_dot(t_, block_diag(lv[0:cw])) for t_, lv in zip(tinvs, lvs)]
    qp_rows = [pair_row(c0, lambda c: pre[p]["r_t"][rows(c)]) + _dot(q[2], block_diag(w_))
               for (p, c0), q, w_ in zip(chains, quads, w_rows)]
    y0_rows = [_dot(q[2], block_diag(u_)) + lv[cw:2 * cw] for q, u_, lv in zip(quads, u_rows, lvs)]
    phs = {}
    for ci, (p, c0) in enumerate(chains):
        for j, c in enumerate((c0, c0 + 1)):
            ls = slice(2 * n * j, 2 * n * (j + 1))
            lhs = jnp.concatenate([pre[p]["b_end"][rows(c)], pre[p]["k_end"][rows(c)]], axis=0)
            rhs = jnp.concatenate(
                [jnp.concatenate([w_rows[ci][:, ls], u_rows[ci][:, ls]], axis=1),
                 jnp.concatenate([jnp.zeros((cw, 2 * n), F32), pre[p]["v"][rows(c)]], axis=1)], axis=0)
            ph = _dot_tn(lhs, rhs)
            pmat = eye_2n * g_end[c, :, slab(p)] + jnp.where(same_head, ph[:, 0:2 * n], 0.0)
            phs[(p, c)] = (jnp.concatenate([qp_rows[ci][:, ls], pmat], axis=0),
                           jnp.where(same_head, ph[:, 2 * n:4 * n], 0.0), y0_rows[ci][:, ls])
    seqs = [(i, p) for i in range(nb) for p in range(n_pairs)]
    hstates = {ip: state_ref[ip[0], ip[1]] for ip in seqs}
    ys = {ip: [] for ip in seqs}
    for c in range(nck_seq):
        for i, p in seqs:
            qpm, hp, y0 = phs[(p, i * nck_seq + c)]
            qph = _dot(qpm, hstates[i, p])
            ys[i, p].append(qph[0:cw] + y0)
            hstates[i, p] = qph[cw:cw + 2 * n] + hp
    for i, p in seqs:
        state_ref[i, p] = hstates[i, p]
    for p in range(n_pairs):
        sl = slab(p)
        y = jnp.concatenate([u for i in range(nb) for u in ys[i, p]], axis=0)
        yc = y - seg_sum(y, lambda s: s * (1.0 / n))
        inv_std = seg_sum(yc * yc, lambda s: lax.rsqrt(s * (1.0 / n) + GN_EPS))
        yn = yc * inv_std * lng_ref[:, sl] + lnb_ref[:, sl]
        bonus = seg_sum(pre[p]["r"] * pre[p]["kmod"] * rk_ref[:, sl]) * pre[p]["v"]
        res = ((yn + bonus) * gate[:, sl]).astype(out_ref.dtype)
        for i in range(nb):
            out_ref[i, :, sl] = res[i * tt:(i + 1) * tt]


def _rwkv_mixer(zr, zl, zg, w0, w2, a0, a2, g2p, k_k, k_a, r_k, ln_g, ln_b):
    bsz, seq, _ = zr.shape
    tt = WKV_TILE
    assert seq % tt == 0 and tt % (2 * WKV_CHUNK) == 0 and 2 * RWKV_HEAD_DIM == LANES
    nb = WKV_BATCH if bsz % WKV_BATCH == 0 else 1
    idx = np.arange(tt)
    tri = ((idx[:, None] // WKV_CHUNK == idx[None, :] // WKV_CHUNK) & (idx[:, None] >= idx[None, :]))
    full = lambda shape: pl.BlockSpec(shape, lambda b, t: (0,) * len(shape))
    row = lambda a: a.reshape(1, -1)
    wl = DECAY_LORA + ICLR_LORA
    return pl.pallas_call(
        _rwkv_kernel,
        grid=(bsz // nb, seq // tt),
        in_specs=[
            pl.BlockSpec((nb, tt, 3 * RWKV_WIDTH), lambda b, t: (b, t, 0)),
            pl.BlockSpec((nb, tt, wl), lambda b, t: (b, t, 0)),
            pl.BlockSpec((nb, tt, GATE_LORA_PAD), lambda b, t: (b, t, 0)),
            full((1, RWKV_WIDTH)), full((DECAY_LORA, RWKV_WIDTH)),
            full((1, RWKV_WIDTH)), full((ICLR_LORA, RWKV_WIDTH)),
            full((GATE_LORA_PAD, RWKV_WIDTH)),
            full((1, RWKV_WIDTH)), full((1, RWKV_WIDTH)), full((1, RWKV_WIDTH)),
            full((1, RWKV_WIDTH)), full((1, RWKV_WIDTH)),
            full((tt, tt)),
        ],
        out_specs=pl.BlockSpec((nb, tt, RWKV_WIDTH), lambda b, t: (b, t, 0)),
        out_shape=jax.ShapeDtypeStruct((bsz, seq, RWKV_WIDTH), BF16),
        scratch_shapes=[
            pltpu.VMEM((nb, RWKV_HEADS // 2, 2 * RWKV_HEAD_DIM, 2 * RWKV_HEAD_DIM), F32),
        ],
        compiler_params=pltpu.CompilerParams(
            dimension_semantics=("arbitrary", "arbitrary"), vmem_limit_bytes=VMEM_LIMIT_BYTES),
        name="rwkv7_mixer",
    )(zr, zl, zg, row(w0), w2.astype(BF16), row(a0), a2.astype(BF16),
      g2p.astype(BF16), row(k_k), row(k_a), row(r_k), row(ln_g), row(ln_b), jnp.asarray(tri, BF16))


def _ffn_kernel(x_ref, yn_ref, yr_ref, won_ref, wor_ref, gain_ref, wg_ref, wu_ref, wd_ref, o_ref, *, ff_chunk):
    h = x_ref[...]
    h = h + jnp.dot(yn_ref[...], won_ref[...], preferred_element_type=F32)
    h = h + jnp.dot(yr_ref[...], wor_ref[...], preferred_element_type=F32)
    ms = jnp.mean(h * h, axis=-1, keepdims=True)
    hn = (h * lax.rsqrt(ms + RMS_EPS) * gain_ref[...]).astype(BF16)
    d_ff = wg_ref.shape[1]
    acc = None
    for c in range(d_ff // ff_chunk):
        cs = slice(c * ff_chunk, (c + 1) * ff_chunk)
        gte = jnp.dot(hn, wg_ref[:, cs], preferred_element_type=F32)
        up = jnp.dot(hn, wu_ref[:, cs], preferred_element_type=F32)
        act = (gte * jax.nn.sigmoid(gte) * up).astype(BF16)
        down = jnp.dot(act, wd_ref[cs, :], preferred_element_type=F32)
        acc = down if acc is None else acc + down
    o_ref[...] = h + acc


def _out_ffn(x2, y_nsa, y_rwkv, w_out, ffn_gain, w_gate, w_up, w_down):
    n, d = x2.shape
    d_ff = w_gate.shape[1]
    tm = min(ROW_TILE, n)
    ff_chunk = FF_CHUNK if d_ff % FF_CHUNK == 0 else d_ff
    const = lambda shape: pl.BlockSpec(shape, lambda i: (0, 0), pipeline_mode=pl.Buffered(1))
    return pl.pallas_call(
        functools.partial(_ffn_kernel, ff_chunk=ff_chunk),
        grid=(n // tm,),
        in_specs=[
            pl.BlockSpec((tm, d), lambda i: (i, 0)),
            pl.BlockSpec((tm, NSA_WIDTH), lambda i: (i, 0)),
            pl.BlockSpec((tm, RWKV_WIDTH), lambda i: (i, 0)),
            const((NSA_WIDTH, d)), const((RWKV_WIDTH, d)), const((1, d)),
            const((d, d_ff)), const((d, d_ff)), const((d_ff, d)),
        ],
        out_specs=pl.BlockSpec((tm, d), lambda i: (i, 0)),
        out_shape=jax.ShapeDtypeStruct((n, d), F32),
        compiler_params=pltpu.CompilerParams(
            dimension_semantics=("arbitrary",), vmem_limit_bytes=VMEM_LIMIT_BYTES),
        name="out_proj_ffn",
    )(x2, y_nsa, y_rwkv, w_out[:NSA_WIDTH].astype(BF16), w_out[NSA_WIDTH:].astype(BF16),
      ffn_gain.reshape(1, d), w_gate.astype(BF16), w_up.astype(BF16), w_down.astype(BF16))


def _layer(h, attn_gain, w_in, q_gain, k_gain, cmp_pe, cmp_w1, cmp_b1, cmp_w2, cmp_b2, out_gain, shift_mix,
           w0, w2, a0, a2, g2, k_k, k_a, r_k, ln_g, ln_b, w_out, ffn_gain, w_gate, w_up, w_down, rel_table):
    bsz, seq, d = h.shape
    nsa_cols = NSA_WIDTH + 6 * KV_WIDTH + 3 * NSA_HEADS
    gate0 = NSA_WIDTH + 6 * KV_WIDTH

    w_nsa, w_rw = w_in[:, :nsa_cols], w_in[:, nsa_cols:]
    gate_src = np.array([gate0 + h_ * 3 + br for br in range(3) for h_ in range(NSA_HEADS)])
    w_gates = jnp.zeros((d, LANES), F32).at[:, :3 * NSA_HEADS].set(w_nsa[:, gate_src])
    lora0 = 3 * RWKV_WIDTH
    g0 = lora0 + DECAY_LORA + ICLR_LORA
    w_g = jnp.zeros((d, GATE_LORA_PAD), F32).at[:, :GATE_LORA].set(w_rw[:, g0:g0 + GATE_LORA])
    widths = (NSA_WIDTH, 6 * KV_WIDTH, LANES, 3 * RWKV_WIDTH, DECAY_LORA + ICLR_LORA, GATE_LORA_PAD)
    w_cat = jnp.concatenate([w_nsa[:, :gate0], w_gates, w_rw[:, :g0], w_g], axis=1).astype(BF16)
    mix_g = jnp.zeros((GATE_LORA_PAD,), F32).at[:GATE_LORA].set(shift_mix[g0:g0 + GATE_LORA])
    g2p = jnp.zeros((GATE_LORA_PAD, RWKV_WIDTH), F32).at[:GATE_LORA].set(g2)

    x2 = h.reshape(bsz * seq, d)
    q, kv, gates, zr, zl, zg = _in_projection(x2, attn_gain.reshape(1, d), w_cat, widths, seq, q_gain, k_gain,
                                              shift_mix[:lora0], shift_mix[lora0:g0], mix_g)
    r3 = lambda u: u.reshape(bsz, seq, u.shape[-1])
    y_nsa = _nsa_mixer(r3(q), r3(kv), r3(gates), k_gain[0], cmp_pe, cmp_w1, cmp_b1, cmp_w2, cmp_b2,
                       out_gain, rel_table)
    y_rwkv = _rwkv_mixer(r3(zr), r3(zl), r3(zg), w0, w2, a0, a2, g2p, k_k, k_a, r_k.reshape(-1), ln_g, ln_b)
    out = _out_ffn(x2, y_nsa.reshape(bsz * seq, NSA_WIDTH), y_rwkv.reshape(bsz * seq, RWKV_WIDTH),
                   w_out, ffn_gain, w_gate, w_up, w_down)
    return out.reshape(bsz, seq, d)


def kernel(x, attn_norm_gain, w_in, nsa_q_gain, nsa_k_gain, cmp_pe, cmp_w1, cmp_b1, cmp_w2, cmp_b2, nsa_out_gain, rwkv_shift_mix, rwkv_w0, rwkv_w2, rwkv_a0, rwkv_a2, rwkv_g2, rwkv_k_k, rwkv_k_a, rwkv_r_k, rwkv_ln_gain, rwkv_ln_bias, w_out, ffn_norm_gain, w_gate, w_up, w_down, rel_bias_table):
    h = x
    for l in range(attn_norm_gain.shape[0]):
        h = _layer(h, attn_norm_gain[l], w_in[l], nsa_q_gain[l], nsa_k_gain[l], cmp_pe[l], cmp_w1[l], cmp_b1[l],
                   cmp_w2[l], cmp_b2[l], nsa_out_gain[l], rwkv_shift_mix[l], rwkv_w0[l], rwkv_w2[l], rwkv_a0[l],
                   rwkv_a2[l], rwkv_g2[l], rwkv_k_k[l], rwkv_k_a[l], rwkv_r_k[l], rwkv_ln_gain[l],
                   rwkv_ln_bias[l], w_out[l], ffn_norm_gain[l], w_gate[l], w_up[l], w_down[l], rel_bias_table)
    return h
```

```python
import functools
import math

import numpy as np
import jax
import jax.numpy as jnp
from jax import lax
from jax.experimental import pallas as pl
from jax.experimental.pallas import tpu as pltpu

HEAD_DIM = 64
NSA_HEADS = 8
NSA_KV_HEADS = 2
NSA_HPG = NSA_HEADS // NSA_KV_HEADS
NSA_WIDTH = NSA_HEADS * HEAD_DIM
KV_WIDTH = NSA_KV_HEADS * HEAD_DIM
CMP_BLOCK = 32
CMP_STRIDE = 16
CMP_HIDDEN = 256
SEL_BLOCK = 64
SEL_TOP_N = 16
SEL_LOCAL = 2
WINDOW = 512
ATTN_SCALE = HEAD_DIM ** -0.5
LOG2E = math.log2(math.e)
NEG_INF = -1e30
FORCE_SCORE = 1e9
REL_BUCKETS = 32
REL_MAX_DIST = 128
RWKV_HEADS = 8
RWKV_HEAD_DIM = 64
RWKV_WIDTH = RWKV_HEADS * RWKV_HEAD_DIM
DECAY_LORA = 64
ICLR_LORA = 64
GATE_LORA = 160
GATE_LORA_PAD = 256
GN_EPS = 64e-5
RMS_EPS = 1e-6

LANES = 128
Q_TILE = 128
TILES_PER_STEP = 2
V_ROWS = HEAD_DIM + 16
K_TILE = 128
WKV_TILE = 256
WKV_CHUNK = 64
WKV_BATCH = 2
ROW_TILE = 512
FF_CHUNK = 256
VMEM_LIMIT_BYTES = 56 * 1024 * 1024

F32 = jnp.float32
BF16 = jnp.bfloat16


def _dot(a, b):
    return jnp.dot(a.astype(BF16), b.astype(BF16), preferred_element_type=F32)


def _dot_nt(a, b):
    return lax.dot_general(a.astype(BF16), b.astype(BF16), (((1,), (1,)), ((), ())),
                           preferred_element_type=F32)


def _dot_tn(a, b):
    return lax.dot_general(a.astype(BF16), b.astype(BF16), (((0,), (0,)), ((), ())),
                           preferred_element_type=F32)


def _split3(x):
    hi = x.astype(BF16)
    r1 = x - hi.astype(F32)
    mid = r1.astype(BF16)
    lo = (r1 - mid.astype(F32)).astype(BF16)
    return hi, mid, lo


def _dot_exact_lhs(m_bf16, x):
    hi, mid, lo = _split3(x)
    acc = jnp.dot(m_bf16, lo, preferred_element_type=F32)
    acc = acc + jnp.dot(m_bf16, mid, preferred_element_type=F32)
    return acc + jnp.dot(m_bf16, hi, preferred_element_type=F32)


def _rel_bucket_np(dist):
    max_exact = REL_BUCKETS // 2
    d = np.maximum(dist, 0)
    ratio = np.maximum(d, 1).astype(np.float32) / np.float32(max_exact)
    log_ratio = np.log(ratio).astype(np.float32) / np.float32(math.log(REL_MAX_DIST / max_exact))
    large = np.minimum(max_exact + (log_ratio * np.float32(REL_BUCKETS - max_exact)).astype(np.int32),
                       REL_BUCKETS - 1)
    return np.where(d < max_exact, d, large).astype(np.int32)


def _inproj_kernel(x_ref, gain_ref, w_ref, qg_ref, kg_ref, mr_ref, ml_ref, mg_ref, kk_ref,
                   q_ref, kv_ref, gt_ref, zr_ref, zl_ref, zg_ref, kkn_ref,
                   cr_ref, cl_ref, cg_ref, *, widths, tiles_per_seq):
    i = pl.program_id(0)
    tm = x_ref.shape[0]

    @pl.when(i == 0)
    def _init_carries():
        for carry_ref in (cr_ref, cl_ref, cg_ref):
            carry_ref[...] = jnp.zeros_like(carry_ref)

    x = x_ref[...]
    ms = jnp.mean(x * x, axis=-1, keepdims=True)
    xn = (x * lax.rsqrt(ms + RMS_EPS) * gain_ref[...]).astype(BF16)
    offs = np.concatenate([[0], np.cumsum(widths)])
    seg = lambda k: jnp.dot(xn, w_ref[:, int(offs[k]):int(offs[k + 1])], preferred_element_type=F32)
    lo = lax.broadcasted_iota(jnp.int32, (1, LANES), 1) < HEAD_DIM

    def slab_rms(u, gain):
        s_lo = jnp.sum(jnp.where(lo, u * u, 0.0), axis=-1, keepdims=True)
        s_hi = jnp.sum(jnp.where(lo, 0.0, u * u), axis=-1, keepdims=True)
        inv = lambda s: lax.rsqrt(s * (1.0 / HEAD_DIM) + RMS_EPS)
        return u * jnp.where(lo, inv(s_lo), inv(s_hi)) * gain

    first = (i % tiles_per_seq) == 0
    row = lax.broadcasted_iota(jnp.int32, (tm, 1), 0)
    for k, z_ref, mix_ref, carry_ref in ((3, zr_ref, mr_ref, cr_ref), (4, zl_ref, ml_ref, cl_ref),
                                         (5, zg_ref, mg_ref, cg_ref)):
        z = seg(k)
        carry = jnp.where(first, 0.0, carry_ref[...])
        prev = jnp.where(row == 0, carry, pltpu.roll(z, 1, 0))
        carry_ref[...] = z[tm - 1:tm, :]
        zs = z + mix_ref[...] * (prev - z)
        z_ref[...] = zs
        if k == 3:
            for p in range(RWKV_WIDTH // LANES):
                sl = slice(p * LANES, (p + 1) * LANES)
                kkh = zs[:, RWKV_WIDTH + p * LANES:RWKV_WIDTH + (p + 1) * LANES] * kk_ref[:, sl]
                s_lo = jnp.sum(jnp.where(lo, kkh * kkh, 0.0), axis=-1, keepdims=True)
                s_hi = jnp.sum(jnp.where(lo, 0.0, kkh * kkh), axis=-1, keepdims=True)
                inv = lambda s: lax.rsqrt(jnp.maximum(s, 1e-24))
                kkn_ref[:, sl] = kkh * jnp.where(lo, inv(s_lo), inv(s_hi))
    q = seg(0)
    for p in range(NSA_WIDTH // LANES):
        sl = slice(p * LANES, (p + 1) * LANES)
        q_ref[:, sl] = slab_rms(q[:, sl], qg_ref[...]) * (ATTN_SCALE * LOG2E)
    kv = seg(1)
    for b in (2, 4, 0, 1, 3, 5):
        sl = slice(b * KV_WIDTH, (b + 1) * KV_WIDTH)
        kv_ref[:, sl] = slab_rms(kv[:, sl], kg_ref[b // 2 - 1:b // 2, :]) if b in (2, 4) else kv[:, sl]
    gt_ref[...] = seg(2)


def _in_projection(x2, gain, w_cat, widths, seq, q_gain, k_gain, mix_r, mix_l, mix_g, k_k):
    n, d = x2.shape
    tm = min(ROW_TILE, n)
    assert seq % tm == 0 and KV_WIDTH == LANES
    total = sum(widths)
    out_widths = tuple(widths) + (RWKV_WIDTH,)
    full = lambda shape: pl.BlockSpec(shape, lambda i: (0, 0))
    row = lambda a: a.reshape(1, -1)
    return pl.pallas_call(
        functools.partial(_inproj_kernel, widths=widths, tiles_per_seq=seq // tm),
        grid=(n // tm,),
        in_specs=[
            pl.BlockSpec((tm, d), lambda i: (i, 0)),
            full((1, d)), full((d, total)),
            full((1, LANES)), full((2, LANES)),
            full((1, widths[3])), full((1, widths[4])), full((1, widths[5])), full((1, RWKV_WIDTH)),
        ],
        out_specs=[pl.BlockSpec((tm, w), lambda i: (i, 0)) for w in out_widths],
        out_shape=[jax.ShapeDtypeStruct((n, w), F32) for w in out_widths],
        scratch_shapes=[pltpu.VMEM((1, w), F32) for w in widths[3:6]],
        compiler_params=pltpu.CompilerParams(
            dimension_semantics=("arbitrary",), vmem_limit_bytes=VMEM_LIMIT_BYTES),
        name="in_projection",
    )(x2, gain, w_cat, jnp.tile(q_gain, 2).reshape(1, LANES), jnp.tile(k_gain[1:3], (1, 2)),
      row(mix_r), row(mix_l), row(mix_g), row(k_k))


def _head_rms(u, gain):
    ms = jnp.mean(u * u, axis=-1, keepdims=True)
    return u * lax.rsqrt(ms + RMS_EPS) * gain


def _nsa_kernel(tab_ref,
                q_ref, kv_ref, gate_ref,
                kg_ref, pe_ref, w1_ref, b1_ref, w2_ref, b2_ref, og_ref,
                bk0_ref, bk1_ref, bkc_ref, ovl_ref,
                out_ref,
                kvpad_ref, kcmp_ref, vcmp_ref, ksel_ref, vsel_ref, kwin_ref, vwin_ref,
                btab_ref, ccan_ref, ybuf_ref,
                *, seq):
    b = pl.program_id(0)
    step = pl.program_id(1)
    n_cb = seq // CMP_STRIDE
    n_c = (seq - CMP_BLOCK) // CMP_STRIDE + 1
    n_sel = seq // SEL_BLOCK
    cols = NSA_HPG * Q_TILE

    @pl.when((b == 0) & (step == 0))
    def _build_bias_tables():
        bk0 = bk0_ref[...]
        bk1 = bk1_ref[...]
        bkc = bkc_ref[...]
        kk = lax.broadcasted_iota(jnp.int32, (K_TILE, Q_TILE), 0)
        qq = lax.broadcasted_iota(jnp.int32, (K_TILE, Q_TILE), 1)
        causal = jnp.where(kk <= qq, 0.0, NEG_INF).astype(F32)
        anti = jnp.where(kk > qq, 0.0, NEG_INF).astype(F32)
        for h in range(NSA_HEADS):
            t0 = jnp.zeros((K_TILE, Q_TILE), F32)
            t1 = jnp.zeros((K_TILE, Q_TILE), F32)
            tc = jnp.zeros((2 * LANES, Q_TILE), F32)
            for k in range(REL_BUCKETS):
                val = tab_ref[k, h]
                t0 = jnp.where(bk0 == k, val, t0)
                t1 = jnp.where(bk1 == k, val, t1)
                tc = jnp.where(bkc == k, val, tc)
            far = tab_ref[REL_BUCKETS - 1, h]
            g, p = divmod(h, NSA_HPG)
            cs = slice(p * Q_TILE, (p + 1) * Q_TILE)
            btab_ref[g, 0, :, cs] = jnp.zeros((K_TILE, Q_TILE), F32)
            btab_ref[g, 1, :, cs] = (t1 - far) * LOG2E
            btab_ref[g, 2, :, cs] = (t0 - far) * LOG2E + causal
            btab_ref[g, 3, :, cs] = anti
            btab_ref[g, 4, :, cs] = jnp.full((K_TILE, Q_TILE), NEG_INF, F32)
            ccan_ref[h] = tc * LOG2E
        blk = lax.broadcasted_iota(jnp.int32, (seq, HEAD_DIM), 0) // SEL_BLOCK
        col = lax.broadcasted_iota(jnp.int32, (seq, HEAD_DIM), 1)
        onehot = (blk == col).astype(BF16)
        padcol = (lax.broadcasted_iota(jnp.int32, (WINDOW, 2 * HEAD_DIM), 1) == HEAD_DIM).astype(BF16)
        for g in range(NSA_KV_HEADS):
            ksel_ref[g, :, HEAD_DIM:2 * HEAD_DIM] = onehot
            kwin_ref[g, 0:WINDOW, :] = padcol
            kwin_ref[g, WINDOW:WINDOW + seq, HEAD_DIM:2 * HEAD_DIM] = jnp.zeros((seq, HEAD_DIM), BF16)
            vwin_ref[g, 0:HEAD_DIM, 0:WINDOW] = jnp.zeros((HEAD_DIM, WINDOW), BF16)
            for vt_ref, width in ((vsel_ref, seq), (vwin_ref, WINDOW + seq)):
                ones_row = lax.broadcasted_iota(jnp.int32, (V_ROWS - HEAD_DIM, width), 0) == 0
                vt_ref[g, HEAD_DIM:V_ROWS, :] = ones_row.astype(BF16)

    @pl.when(step == 0)
    def _per_batch():
        for which in range(2):
            kvpad_ref[which, seq:seq + 2 * CMP_STRIDE, :] = jnp.zeros((2 * CMP_STRIDE, KV_WIDTH), F32)
            kvpad_ref[which, 0:seq, :] = kv_ref[0, :, which * KV_WIDTH:(which + 1) * KV_WIDTH]
            first = lax.broadcasted_iota(jnp.int32, (1, KV_WIDTH), 1) < HEAD_DIM
            slabs = [[], []]
            for l in range(0, CMP_BLOCK, 2):
                ra = kvpad_ref[which, pl.ds(l, n_cb, stride=CMP_STRIDE), :]
                rb = kvpad_ref[which, pl.ds(l + 1, n_cb, stride=CMP_STRIDE), :]
                slabs[0].append(jnp.where(first, ra, pltpu.roll(rb, HEAD_DIM, 1)))
                slabs[1].append(jnp.where(first, pltpu.roll(ra, HEAD_DIM, 1), rb))
            cmps = []
            for g in range(NSA_KV_HEADS):
                blk = jnp.concatenate(slabs[g], axis=1) + pe_ref[which]
                hid = jax.nn.gelu(_dot(blk, w1_ref[which]) + b1_ref[which], approximate=True)
                cmps.append(_dot(hid, w2_ref[which]) + b2_ref[which])
            if which == 0:
                for g in range(NSA_KV_HEADS):
                    kcmp_ref[g] = _head_rms(cmps[g], kg_ref[0:1, :]).astype(BF16)
            else:
                vct = jnp.concatenate(cmps, axis=1).T
                for g in range(NSA_KV_HEADS):
                    vcmp_ref[g] = vct[g * HEAD_DIM:(g + 1) * HEAD_DIM].astype(BF16)
        kv = kv_ref[0]
        for g in range(NSA_KV_HEADS):
            def col(i):
                c = (2 + i) * KV_WIDTH + g * HEAD_DIM
                return kv[:, c:c + HEAD_DIM]
            ksel_ref[g, :, 0:HEAD_DIM] = col(0).astype(BF16)
            kwin_ref[g, WINDOW:WINDOW + seq, 0:HEAD_DIM] = col(2).astype(BF16)
        for i, vt_ref, pad in ((1, vsel_ref, 0), (3, vwin_ref, WINDOW)):
            vt = kv[:, (2 + i) * KV_WIDTH:(3 + i) * KV_WIDTH].T
            for g in range(NSA_KV_HEADS):
                vt_ref[g, 0:HEAD_DIM, pad:pad + seq] = vt[g * HEAD_DIM:(g + 1) * HEAD_DIM].astype(BF16)

    n_selp = -(-n_sel // 8) * 8
    qT_all = q_ref[0].T
    gates_all = jax.nn.sigmoid(gate_ref[0].T)
    tiles = range(TILES_PER_STEP)
    groups = range(NSA_KV_HEADS)
    units = [(j, g) for j in tiles for g in groups]
    each = lambda f: {u: f(*u) for u in units}
    qts = [TILES_PER_STEP * step + j for j in tiles]
    tqs = [qts[j] * Q_TILE + lax.broadcasted_iota(jnp.int32, (1, Q_TILE), 1) for j in tiles]
    lanes = lambda j: slice(j * Q_TILE, (j + 1) * Q_TILE)

    def q_group(j, g):
        qs = [qT_all[h * HEAD_DIM:(h + 1) * HEAD_DIM, lanes(j)].astype(BF16)
              for h in range(g * NSA_HPG, (g + 1) * NSA_HPG)]
        return jnp.concatenate(qs, axis=1)

    q_cats = each(q_group)
    aug_row = lax.broadcasted_iota(jnp.int32, (HEAD_DIM, cols), 0)

    cc = lax.broadcasted_iota(jnp.int32, (n_cb, Q_TILE), 0)
    valid4 = [jnp.concatenate([(cc * CMP_STRIDE + (CMP_BLOCK - 1) <= tqs[j]) & (cc < n_c)] * NSA_HPG, axis=1)
              for j in tiles]
    c_starts = [pl.multiple_of(LANES - (CMP_STRIDE // 2) * qts[j], 8) for j in tiles]
    ss = each(lambda j, g: jnp.dot(kcmp_ref[g], q_cats[j, g], preferred_element_type=F32))
    ss = each(lambda j, g: jnp.where(valid4[j], ss[j, g] + jnp.concatenate(
        [ccan_ref[g * NSA_HPG + p, pl.ds(c_starts[j], n_cb), :] for p in range(NSA_HPG)], axis=1), NEG_INF))
    es = each(lambda j, g: jnp.exp2(ss[j, g] - jnp.max(ss[j, g], axis=0, keepdims=True)))
    pcs = each(lambda j, g: es[j, g] * (1.0 / jnp.sum(es[j, g], axis=0, keepdims=True))
               * valid4[j].astype(F32))
    o_cmps = each(lambda j, g: jnp.dot(vcmp_ref[g], pcs[j, g].astype(BF16), preferred_element_type=F32))
    psums = each(lambda j, g: functools.reduce(
        lambda u, w: u + w, [pcs[j, g][:, p * Q_TILE:(p + 1) * Q_TILE] for p in range(NSA_HPG)]))

    n_wb = WINDOW // K_TILE
    band = WINDOW + K_TILE
    band_kinds = (3,) + (0,) * (n_wb - 2) + (1, 2)
    w0s = [pl.multiple_of(qts[j] * K_TILE, K_TILE) for j in tiles]
    pad_rows = jnp.where(aug_row == 0, NEG_INF, 0.0).astype(BF16)

    def band_logits(g, s):
        blocks = []
        for i, kind in enumerate(band_kinds):
            si = s[i * K_TILE:(i + 1) * K_TILE]
            blocks.append(si if kind == 0 else si + btab_ref[g, kind])
        return jnp.concatenate(blocks, axis=0)

    sw = each(lambda j, g: jnp.dot(kwin_ref[g, pl.ds(w0s[j], band), :],
                                   jnp.concatenate([q_cats[j, g], pad_rows], axis=0),
                                   preferred_element_type=F32))
    sw = each(lambda j, g: band_logits(g, sw[j, g]))
    pws = each(lambda j, g: jnp.exp2(sw[j, g] - jnp.max(sw[j, g], axis=0, keepdims=True)))
    o_wins = each(lambda j, g: jnp.dot(vwin_ref[g, :, pl.ds(w0s[j], band)], pws[j, g].astype(BF16),
                                       preferred_element_type=F32))
    o_wins = each(lambda j, g: o_wins[j, g][0:HEAD_DIM] * (1.0 / o_wins[j, g][HEAD_DIM:HEAD_DIM + 1]))

    imps = each(lambda j, g: _dot_exact_lhs(ovl_ref[...], psums[j, g])[0:n_selp])
    jb = lax.broadcasted_iota(jnp.int32, (n_selp, Q_TILE), 0)

    def masked_score(j, g):
        qb = tqs[j] // SEL_BLOCK
        forced = (jb == 0) | ((jb <= qb) & (jb > qb - SEL_LOCAL))
        score = jnp.where(forced, FORCE_SCORE, imps[j, g])
        score = jnp.where(jb <= qb, score, -1.0)
        return jnp.where(jb < n_sel, score, -2.0)

    scores = each(masked_score)
    sub = 8
    sub_row = lax.broadcasted_iota(jnp.int32, (sub, Q_TILE), 0)
    score_rows = {u: [scores[u][r0:r0 + sub] for r0 in range(0, n_selp, sub)] for u in units}
    rank_rows = {u: [jnp.zeros((sub, Q_TILE), F32) for _ in range(0, n_selp, sub)] for u in units}
    for i in range(n_sel):
        for u in units:
            si = scores[u][i:i + 1, :]
            for v, s_v in enumerate(score_rows[u]):
                r0 = v * sub
                if r0 > i:
                    ahead = si >= s_v
                elif r0 + sub - 1 < i:
                    ahead = si > s_v
                else:
                    ahead = (si > s_v) | ((si == s_v) & (sub_row + r0 > i))
                rank_rows[u][v] = rank_rows[u][v] + ahead.astype(F32)
    ranks = {u: jnp.concatenate(rank_rows[u], axis=0) for u in units}

    def q_with_mask_rows(j, g):
        sel = (ranks[j, g] < float(min(SEL_TOP_N, n_sel))) & (jb < n_sel)
        nm = (sel.astype(F32) - 1.0) * (-NEG_INF)
        return jnp.concatenate([q_cats[j, g], jnp.concatenate([nm] * NSA_HPG, axis=1).astype(BF16),
                                jnp.zeros((HEAD_DIM - n_selp, cols), BF16)], axis=0)

    q_sels = each(q_with_mask_rows)

    pair = 2 * K_TILE

    n_pairs = qts[-1] // 2 + 1
    n_far = jnp.maximum((qts[0] - 1) // 2, 0)

    def sel_step(kp, carry, near):
        k0 = pl.multiple_of(kp * pair, pair)
        old = {u: carry[i] for i, u in enumerate(units)}

        def logits(j, g):
            s = jnp.dot(ksel_ref[g, pl.ds(k0, pair), :], q_sels[j, g], preferred_element_type=F32)
            if near:
                rel = qts[j] - 2 * kp
                kind_of = lambda d: jnp.where(d >= 2, 0, jnp.where(d == 1, 1, jnp.where(d == 0, 2, 4)))
                kind_a, kind_b = kind_of(rel), kind_of(rel - 1)
                s = s + jnp.concatenate([btab_ref[g, kind_a], btab_ref[g, kind_b]], axis=0)
            return s

        ss = each(logits)
        m_new = each(lambda j, g: jnp.maximum(old[j, g][0], jnp.max(ss[j, g], axis=0, keepdims=True)))
        alpha = each(lambda j, g: jnp.exp2(old[j, g][0] - m_new[j, g]))
        pv = each(lambda j, g: jnp.dot(vsel_ref[g, :, pl.ds(k0, pair)],
                                       jnp.exp2(ss[j, g] - m_new[j, g]).astype(BF16),
                                       preferred_element_type=F32))
        return tuple((m_new[u], alpha[u] * old[u][1] + pv[u]) for u in units)

    init = (jnp.full((1, cols), NEG_INF, F32), jnp.zeros((V_ROWS, cols), F32))
    sel_mid = lax.fori_loop(0, n_far, functools.partial(sel_step, near=False), tuple(init for _ in units))
    sel_out = lax.fori_loop(n_far, n_pairs, functools.partial(sel_step, near=True), sel_mid)

    ssqs = []
    for j in tiles:
        gates = gates_all[:, lanes(j)]
        ssq = jnp.zeros((1, Q_TILE), F32)
        for g in groups:
            _, acc_s = sel_out[units.index((j, g))]
            o_sel = acc_s[0:HEAD_DIM] * (1.0 / acc_s[HEAD_DIM:HEAD_DIM + 1])
            for p in range(NSA_HPG):
                h = g * NSA_HPG + p
                cs = slice(p * Q_TILE, (p + 1) * Q_TILE)
                o = (gates[h:h + 1] * o_cmps[j, g][:, cs]
                     + gates[NSA_HEADS + h:NSA_HEADS + h + 1] * o_sel[:, cs]
                     + gates[2 * NSA_HEADS + h:2 * NSA_HEADS + h + 1] * o_wins[j, g][:, cs])
                ssq = ssq + jnp.sum(o * o, axis=0, keepdims=True)
                ybuf_ref[h * HEAD_DIM:(h + 1) * HEAD_DIM, lanes(j)] = o
        ssqs.append(ssq)
    ssq_all = jnp.concatenate(ssqs, axis=1)
    yT = ybuf_ref[...] * lax.rsqrt(ssq_all / NSA_WIDTH + RMS_EPS) * og_ref[...]
    out_ref[0] = yT.T.astype(out_ref.dtype)


def _nsa_mixer(q, kv, gates, cmp_k_gain, cmp_pe, cmp_w1, cmp_b1, cmp_w2, cmp_b2, out_gain, rel_table):
    bsz, seq, _ = q.shape
    assert TILES_PER_STEP % 2 == 0 and seq % (TILES_PER_STEP * Q_TILE) == 0
    assert seq // CMP_STRIDE == LANES and seq // SEL_BLOCK <= HEAD_DIM
    n_steps = seq // (TILES_PER_STEP * Q_TILE)
    q_rows = TILES_PER_STEP * Q_TILE
    n_cb = seq // CMP_STRIDE
    n_sel = seq // SEL_BLOCK
    n_selp = -(-n_sel // 8) * 8
    cols = NSA_HPG * Q_TILE

    kk = np.arange(K_TILE)[:, None]
    qq = np.arange(Q_TILE)[None, :]
    bk0 = _rel_bucket_np(qq - kk)
    bk1 = _rel_bucket_np(K_TILE + qq - kk)
    cprime = np.arange(2 * LANES)[:, None] - LANES
    bkc = _rel_bucket_np(qq - (CMP_BLOCK - 1) - CMP_STRIDE * cprime)
    c_start = np.arange(n_cb)[None, :] * CMP_STRIDE
    j_start = np.arange(LANES)[:, None] * SEL_BLOCK
    ovl = np.clip(np.minimum(c_start + CMP_BLOCK, j_start + SEL_BLOCK) - np.maximum(c_start, j_start), 0, None)
    ovl = (ovl.astype(np.float32) / CMP_STRIDE)
    ovl[n_sel:, :] = 0.0

    assert NSA_KV_HEADS == 2 and KV_WIDTH == LANES
    w1 = cmp_w1.astype(BF16)
    full = lambda shape: pl.BlockSpec(shape, lambda b, t: (0,) * len(shape))
    kernel = functools.partial(_nsa_kernel, seq=seq)
    return pl.pallas_call(
        kernel,
        grid=(bsz, n_steps),
        in_specs=[
            pl.BlockSpec(memory_space=pltpu.SMEM),
            pl.BlockSpec((1, q_rows, NSA_WIDTH), lambda b, t: (b, t, 0)),
            pl.BlockSpec((1, seq, 6 * KV_WIDTH), lambda b, t: (b, 0, 0)),
            pl.BlockSpec((1, q_rows, LANES), lambda b, t: (b, t, 0)),
            full((1, HEAD_DIM)),
            full((2, 1, CMP_BLOCK * HEAD_DIM)),
            full((2, CMP_BLOCK * HEAD_DIM, CMP_HIDDEN)),
            full((2, 1, CMP_HIDDEN)),
            full((2, CMP_HIDDEN, HEAD_DIM)),
            full((2, 1, HEAD_DIM)),
            full((NSA_WIDTH, q_rows)),
            full((K_TILE, Q_TILE)),
            full((K_TILE, Q_TILE)),
            full((2 * LANES, Q_TILE)),
            full((LANES, n_cb)),
        ],
        out_specs=pl.BlockSpec((1, q_rows, NSA_WIDTH), lambda b, t: (b, t, 0)),
        out_shape=jax.ShapeDtypeStruct((bsz, seq, NSA_WIDTH), BF16),
        scratch_shapes=[
            pltpu.VMEM((2, seq + 2 * CMP_STRIDE, KV_WIDTH), F32),
            pltpu.VMEM((NSA_KV_HEADS, n_cb, HEAD_DIM), BF16),
            pltpu.VMEM((NSA_KV_HEADS, HEAD_DIM, n_cb), BF16),
            pltpu.VMEM((NSA_KV_HEADS, seq, 2 * HEAD_DIM), BF16),
            pltpu.VMEM((NSA_KV_HEADS, V_ROWS, seq), BF16),
            pltpu.VMEM((NSA_KV_HEADS, WINDOW + seq, 2 * HEAD_DIM), BF16),
            pltpu.VMEM((NSA_KV_HEADS, V_ROWS, WINDOW + seq), BF16),
            pltpu.VMEM((NSA_KV_HEADS, 5, K_TILE, cols), F32),
            pltpu.VMEM((NSA_HEADS, 2 * LANES, Q_TILE), F32),
            pltpu.VMEM((NSA_WIDTH, q_rows), F32),
        ],
        compiler_params=pltpu.CompilerParams(
            dimension_semantics=("arbitrary", "arbitrary"), vmem_limit_bytes=VMEM_LIMIT_BYTES),
        name="nsa_mixer",
    )(rel_table, q, kv, gates, cmp_k_gain.reshape(1, HEAD_DIM),
      cmp_pe.reshape(2, 1, CMP_BLOCK * HEAD_DIM), w1,
      cmp_b1.reshape(2, 1, CMP_HIDDEN), cmp_w2.astype(BF16), cmp_b2.reshape(2, 1, HEAD_DIM),
      jnp.broadcast_to(out_gain[:, None], (NSA_WIDTH, q_rows)), jnp.asarray(bk0), jnp.asarray(bk1),
      jnp.asarray(bkc), jnp.asarray(ovl, BF16))


def _rwkv_kernel(zr_ref, zl_ref, zg_ref, kkn_ref, w0_ref, w2_ref, a0_ref, a2_ref, g2_ref,
                 ka_ref, rk_ref, lng_ref, lnb_ref, tri_ref,
                 out_ref,
                 state_ref):
    t = pl.program_id(1)
    tt = WKV_TILE
    n = RWKV_HEAD_DIM
    nb = zr_ref.shape[0]
    nck_seq = tt // WKV_CHUNK
    nck = nb * nck_seq

    @pl.when(t == 0)
    def _reset():
        state_ref[...] = jnp.zeros_like(state_ref)

    stacked = lambda z_ref: jnp.concatenate([z_ref[i] for i in range(nb)], axis=0)
    zr = stacked(zr_ref)
    zl = stacked(zl_ref)
    zg = stacked(zg_ref)
    r = zr[:, 0:RWKV_WIDTH]
    k = zr[:, RWKV_WIDTH:2 * RWKV_WIDTH]
    v = zr[:, 2 * RWKV_WIDTH:3 * RWKV_WIDTH]

    xw = w0_ref[...] + _dot(jnp.tanh(zl[:, 0:DECAY_LORA]), w2_ref[...])
    lw = -math.exp(-0.5) * jax.nn.sigmoid(xw)
    a = jax.nn.sigmoid(a0_ref[...] + _dot(zl[:, DECAY_LORA:DECAY_LORA + ICLR_LORA], a2_ref[...]))
    gate = _dot(jax.nn.sigmoid(zg), g2_ref[...])

    cum = jnp.concatenate([_dot_exact_lhs(tri_ref[...], lw[i * tt:(i + 1) * tt]) for i in range(nb)],
                          axis=0)
    cum_end = cum.reshape(nck, WKV_CHUNK, RWKV_WIDTH)[:, WKV_CHUNK - 1:WKV_CHUNK, :]
    cum_end_b = jnp.broadcast_to(cum_end, (nck, WKV_CHUNK, RWKV_WIDTH)).reshape(nb * tt, RWKV_WIDTH)
    e_incl = jnp.exp(cum)
    e_excl = jnp.exp(cum - lw)
    e_neg = jnp.exp(-cum)
    e_rev = jnp.exp(cum_end_b - cum)
    g_end = jnp.exp(cum_end)

    cw = WKV_CHUNK
    lane = lax.broadcasted_iota(jnp.int32, (1, 2 * n), 1)
    lo = lane < n
    ri = lax.broadcasted_iota(jnp.int32, (cw, 4 * n), 0)
    ci = lax.broadcasted_iota(jnp.int32, (cw, 4 * n), 1) % n
    strict = ri > ci
    incl = ri >= ci
    eye_row = (ri == ci).astype(F32)
    i2 = lax.broadcasted_iota(jnp.int32, (2 * n, 2 * n), 0)
    j2 = lax.broadcasted_iota(jnp.int32, (2 * n, 2 * n), 1)
    same_head = (i2 // n) == (j2 // n)
    eye_2n = (i2 == j2).astype(F32)
    zeros_slab = jnp.zeros((cw, 2 * n), BF16)

    def seg_sum(u, f=lambda s: s):
        s_lo = jnp.sum(jnp.where(lo, u, 0.0), axis=-1, keepdims=True)
        s_hi = jnp.sum(jnp.where(lo, 0.0, u), axis=-1, keepdims=True)
        return jnp.where(lo, f(s_lo), f(s_hi))

    def block_diag(u):
        ua, ub = u[:, 0:2 * n], u[:, 2 * n:4 * n]
        parts = []
        for w, first in ((ua, True), (ub, False)):
            for keep_lo in (True, False):
                blk = jnp.where(lo, w, 0.0) if keep_lo else jnp.where(lo, 0.0, w)
                blk = blk.astype(BF16)
                parts.append(jnp.concatenate([blk, zeros_slab] if first else [zeros_slab, blk], axis=1))
        return jnp.concatenate(parts, axis=0)

    rows = lambda c: slice(c * cw, (c + 1) * cw)

    n_pairs = RWKV_HEADS // 2
    slab = lambda p: slice(2 * n * p, 2 * n * (p + 1))
    pre = []
    for p in range(n_pairs):
        sl = slab(p)
        rp, kp, vp, ap = r[:, sl], k[:, sl], v[:, sl], a[:, sl]
        kkn = jnp.concatenate([kkn_ref[i, :, sl] for i in range(nb)], axis=0)
        kmod = kp * (1.0 + (ap - 1.0) * ka_ref[:, sl])
        bv = kkn * ap
        a_t = -kkn * e_excl[:, sl]
        r_t = rp * e_incl[:, sl]
        b_hat = bv * e_neg[:, sl]
        k_hat = kmod * e_neg[:, sl]
        mms = []
        for c in range(nck):
            bh, kh = b_hat[rows(c)], k_hat[rows(c)]
            rhs = jnp.concatenate([jnp.where(lo, bh, 0.0), jnp.where(lo, 0.0, bh),
                                   jnp.where(lo, kh, 0.0), jnp.where(lo, 0.0, kh)], axis=0)
            mms.append(_dot_nt(jnp.concatenate([a_t[rows(c)], r_t[rows(c)]], axis=0), rhs))
        pre.append(dict(r=rp, v=vp, kmod=kmod, a_t=a_t, r_t=r_t, b_end=bv * e_rev[:, sl],
                        k_end=kmod * e_rev[:, sl], mms=mms))

    chains = [(p, c0) for p in range(n_pairs) for c0 in range(0, nck, 2)]

    def pair_row(c0, f):
        return jnp.concatenate([f(c0), f(c0 + 1)], axis=1)

    quads = []
    for p, c0 in chains:
        mms = pre[p]["mms"]
        quads.append((
            jnp.where(strict, pair_row(c0, lambda c: mms[c][0:cw, 0:2 * n]), 0.0),
            jnp.where(strict, pair_row(c0, lambda c: mms[c][0:cw, 2 * n:4 * n]), 0.0),
            jnp.where(incl, pair_row(c0, lambda c: mms[c][cw:2 * cw, 0:2 * n]), 0.0),
            jnp.where(incl, pair_row(c0, lambda c: mms[c][cw:2 * cw, 2 * n:4 * n]), 0.0)))
    tinvs = [eye_row + q[0] for q in quads]
    xs = [_dot(q[0], block_diag(q[0])) for q in quads]
    for _ in range(int(math.log2(cw)) - 2):
        txs = [_dot(jnp.concatenate([t_, x_], axis=0), block_diag(x_)) for t_, x_ in zip(tinvs, xs)]
        tinvs = [t_ + tx[0:cw] for t_, tx in zip(tinvs, txs)]
        xs = [tx[cw:2 * cw] for tx in txs]
    tinvs = [t_ + _dot(t_, block_diag(x_)) for t_, x_ in zip(tinvs, xs)]
    lvs = [_dot(jnp.concatenate([q[1], q[3]], axis=0),
                block_diag(pair_row(c0, lambda c: pre[p]["v"][rows(c)])))
           for (p, c0), q in zip(chains, quads)]
    w_rows = [_dot(t_, block_diag(pair_row(c0, lambda c: pre[p]["a_t"][rows(c)])))
              for (p, c0), t_ in zip(chains, tinvs)]
    u_rows = [_dot(t_, block_diag(lv[0:cw])) for t_, lv in zip(tinvs, lvs)]
    qp_rows = [pair_row(c0, lambda c: pre[p]["r_t"][rows(c)]) + _dot(q[2], block_diag(w_))
               for (p, c0), q, w_ in zip(chains, quads, w_rows)]
    y0_rows = [_dot(q[2], block_diag(u_)) + lv[cw:2 * cw] for q, u_, lv in zip(quads, u_rows, lvs)]
    phs = {}
    for ci, (p, c0) in enumerate(chains):
        for j, c in enumerate((c0, c0 + 1)):
            ls = slice(2 * n * j, 2 * n * (j + 1))
            lhs = jnp.concatenate([pre[p]["b_end"][rows(c)], pre[p]["k_end"][rows(c)]], axis=0)
            rhs = jnp.concatenate(
                [jnp.concatenate([w_rows[ci][:, ls], u_rows[ci][:, ls]], axis=1),
                 jnp.concatenate([jnp.zeros((cw, 2 * n), F32), pre[p]["v"][rows(c)]], axis=1)], axis=0)
            ph = _dot_tn(lhs, rhs)
            pmat = eye_2n * g_end[c, :, slab(p)] + jnp.where(same_head, ph[:, 0:2 * n], 0.0)
            phs[(p, c)] = (jnp.concatenate([qp_rows[ci][:, ls], pmat], axis=0),
                           jnp.where(same_head, ph[:, 2 * n:4 * n], 0.0), y0_rows[ci][:, ls])
    seqs = [(i, p) for i in range(nb) for p in range(n_pairs)]
    hstates = {ip: state_ref[ip[0], ip[1]] for ip in seqs}
    ys = {ip: [] for ip in seqs}
    for c in range(nck_seq):
        for i, p in seqs:
            qpm, hp, y0 = phs[(p, i * nck_seq + c)]
            qph = _dot(qpm, hstates[i, p])
            ys[i, p].append(qph[0:cw] + y0)
            hstates[i, p] = qph[cw:cw + 2 * n] + hp
    for i, p in seqs:
        state_ref[i, p] = hstates[i, p]
    for p in range(n_pairs):
        sl = slab(p)
        y = jnp.concatenate([u for i in range(nb) for u in ys[i, p]], axis=0)
        yc = y - seg_sum(y, lambda s: s * (1.0 / n))
        inv_std = seg_sum(yc * yc, lambda s: lax.rsqrt(s * (1.0 / n) + GN_EPS))
        yn = yc * inv_std * lng_ref[:, sl] + lnb_ref[:, sl]
        bonus = seg_sum(pre[p]["r"] * pre[p]["kmod"] * rk_ref[:, sl]) * pre[p]["v"]
        res = ((yn + bonus) * gate[:, sl]).astype(out_ref.dtype)
        for i in range(nb):
            out_ref[i, :, sl] = res[i * tt:(i + 1) * tt]


def _rwkv_mixer(zr, zl, zg, kkn, w0, w2, a0, a2, g2p, k_a, r_k, ln_g, ln_b):
    bsz, seq, _ = zr.shape
    tt = WKV_TILE
    assert seq % tt == 0 and tt % (2 * WKV_CHUNK) == 0 and 2 * RWKV_HEAD_DIM == LANES
    nb = WKV_BATCH if bsz % WKV_BATCH == 0 else 1
    idx = np.arange(tt)
    tri = ((idx[:, None] // WKV_CHUNK == idx[None, :] // WKV_CHUNK) & (idx[:, None] >= idx[None, :]))
    full = lambda shape: pl.BlockSpec(shape, lambda b, t: (0,) * len(shape))
    row = lambda a: a.reshape(1, -1)
    wl = DECAY_LORA + ICLR_LORA
    return pl.pallas_call(
        _rwkv_kernel,
        grid=(bsz // nb, seq // tt),
        in_specs=[
            pl.BlockSpec((nb, tt, 3 * RWKV_WIDTH), lambda b, t: (b, t, 0)),
            pl.BlockSpec((nb, tt, wl), lambda b, t: (b, t, 0)),
            pl.BlockSpec((nb, tt, GATE_LORA_PAD), lambda b, t: (b, t, 0)),
            pl.BlockSpec((nb, tt, RWKV_WIDTH), lambda b, t: (b, t, 0)),
            full((1, RWKV_WIDTH)), full((DECAY_LORA, RWKV_WIDTH)),
            full((1, RWKV_WIDTH)), full((ICLR_LORA, RWKV_WIDTH)),
            full((GATE_LORA_PAD, RWKV_WIDTH)),
            full((1, RWKV_WIDTH)), full((1, RWKV_WIDTH)),
            full((1, RWKV_WIDTH)), full((1, RWKV_WIDTH)),
            full((tt, tt)),
        ],
        out_specs=pl.BlockSpec((nb, tt, RWKV_WIDTH), lambda b, t: (b, t, 0)),
        out_shape=jax.ShapeDtypeStruct((bsz, seq, RWKV_WIDTH), BF16),
        scratch_shapes=[
            pltpu.VMEM((nb, RWKV_HEADS // 2, 2 * RWKV_HEAD_DIM, 2 * RWKV_HEAD_DIM), F32),
        ],
        compiler_params=pltpu.CompilerParams(
            dimension_semantics=("arbitrary", "arbitrary"), vmem_limit_bytes=VMEM_LIMIT_BYTES),
        name="rwkv7_mixer",
    )(zr, zl, zg, kkn, row(w0), w2.astype(BF16), row(a0), a2.astype(BF16),
      g2p.astype(BF16), row(k_a), row(r_k), row(ln_g), row(ln_b), jnp.asarray(tri, BF16))


def _ffn_kernel(x_ref, yn_ref, yr_ref, won_ref, wor_ref, gain_ref, wg_ref, wu_ref, wd_ref, o_ref, *, ff_chunk):
    h = x_ref[...]
    h = h + jnp.dot(yn_ref[...], won_ref[...], preferred_element_type=F32)
    h = h + jnp.dot(yr_ref[...], wor_ref[...], preferred_element_type=F32)
    ms = jnp.mean(h * h, axis=-1, keepdims=True)
    hn = (h * lax.rsqrt(ms + RMS_EPS) * gain_ref[...]).astype(BF16)
    d_ff = wg_ref.shape[1]
    acc = None
    for c in range(d_ff // ff_chunk):
        cs = slice(c * ff_chunk, (c + 1) * ff_chunk)
        gte = jnp.dot(hn, wg_ref[:, cs], preferred_element_type=F32)
        up = jnp.dot(hn, wu_ref[:, cs], preferred_element_type=F32)
        act = (gte * jax.nn.sigmoid(gte) * up).astype(BF16)
        down = jnp.dot(act, wd_ref[cs, :], preferred_element_type=F32)
        acc = down if acc is None else acc + down
    o_ref[...] = h + acc


def _out_ffn(x2, y_nsa, y_rwkv, w_out, ffn_gain, w_gate, w_up, w_down):
    n, d = x2.shape
    d_ff = w_gate.shape[1]
    tm = min(ROW_TILE, n)
    ff_chunk = FF_CHUNK if d_ff % FF_CHUNK == 0 else d_ff
    const = lambda shape: pl.BlockSpec(shape, lambda i: (0, 0), pipeline_mode=pl.Buffered(1))
    return pl.pallas_call(
        functools.partial(_ffn_kernel, ff_chunk=ff_chunk),
        grid=(n // tm,),
        in_specs=[
            pl.BlockSpec((tm, d), lambda i: (i, 0)),
            pl.BlockSpec((tm, NSA_WIDTH), lambda i: (i, 0)),
            pl.BlockSpec((tm, RWKV_WIDTH), lambda i: (i, 0)),
            const((NSA_WIDTH, d)), const((RWKV_WIDTH, d)), const((1, d)),
            const((d, d_ff)), const((d, d_ff)), const((d_ff, d)),
        ],
        out_specs=pl.BlockSpec((tm, d), lambda i: (i, 0)),
        out_shape=jax.ShapeDtypeStruct((n, d), F32),
        compiler_params=pltpu.CompilerParams(
            dimension_semantics=("arbitrary",), vmem_limit_bytes=VMEM_LIMIT_BYTES),
        name="out_proj_ffn",
    )(x2, y_nsa, y_rwkv, w_out[:NSA_WIDTH].astype(BF16), w_out[NSA_WIDTH:].astype(BF16),
      ffn_gain.reshape(1, d), w_gate.astype(BF16), w_up.astype(BF16), w_down.astype(BF16))


def _layer(h, attn_gain, w_in, q_gain, k_gain, cmp_pe, cmp_w1, cmp_b1, cmp_w2, cmp_b2, out_gain, shift_mix,
           w0, w2, a0, a2, g2, k_k, k_a, r_k, ln_g, ln_b, w_out, ffn_gain, w_gate, w_up, w_down, rel_table):
    bsz, seq, d = h.shape
    nsa_cols = NSA_WIDTH + 6 * KV_WIDTH + 3 * NSA_HEADS
    gate0 = NSA_WIDTH + 6 * KV_WIDTH

    w_nsa, w_rw = w_in[:, :nsa_cols], w_in[:, nsa_cols:]
    gate_src = np.array([gate0 + h_ * 3 + br for br in range(3) for h_ in range(NSA_HEADS)])
    w_gates = jnp.zeros((d, LANES), F32).at[:, :3 * NSA_HEADS].set(w_nsa[:, gate_src])
    lora0 = 3 * RWKV_WIDTH
    g0 = lora0 + DECAY_LORA + ICLR_LORA
    w_g = jnp.zeros((d, GATE_LORA_PAD), F32).at[:, :GATE_LORA].set(w_rw[:, g0:g0 + GATE_LORA])
    widths = (NSA_WIDTH, 6 * KV_WIDTH, LANES, 3 * RWKV_WIDTH, DECAY_LORA + ICLR_LORA, GATE_LORA_PAD)
    w_cat = jnp.concatenate([w_nsa[:, :gate0], w_gates, w_rw[:, :g0], w_g], axis=1).astype(BF16)
    mix_g = jnp.zeros((GATE_LORA_PAD,), F32).at[:GATE_LORA].set(shift_mix[g0:g0 + GATE_LORA])
    g2p = jnp.zeros((GATE_LORA_PAD, RWKV_WIDTH), F32).at[:GATE_LORA].set(g2)

    x2 = h.reshape(bsz * seq, d)
    q, kv, gates, zr, zl, zg, kkn = _in_projection(x2, attn_gain.reshape(1, d), w_cat, widths, seq, q_gain, k_gain,
                                                   shift_mix[:lora0], shift_mix[lora0:g0], mix_g, k_k)
    r3 = lambda u: u.reshape(bsz, seq, u.shape[-1])
    y_nsa = _nsa_mixer(r3(q), r3(kv), r3(gates), k_gain[0], cmp_pe, cmp_w1, cmp_b1, cmp_w2, cmp_b2,
                       out_gain, rel_table)
    y_rwkv = _rwkv_mixer(r3(zr), r3(zl), r3(zg), r3(kkn), w0, w2, a0, a2, g2p, k_a, r_k.reshape(-1), ln_g, ln_b)
    out = _out_ffn(x2, y_nsa.reshape(bsz * seq, NSA_WIDTH), y_rwkv.reshape(bsz * seq, RWKV_WIDTH),
                   w_out, ffn_gain, w_gate, w_up, w_down)
    return out.reshape(bsz, seq, d)


def kernel(x, attn_norm_gain, w_in, nsa_q_gain, nsa_k_gain, cmp_pe, cmp_w1, cmp_b1, cmp_w2, cmp_b2, nsa_out_gain, rwkv_shift_mix, rwkv_w0, rwkv_w2, rwkv_a0, rwkv_a2, rwkv_g2, rwkv_k_k, rwkv_k_a, rwkv_r_k, rwkv_ln_gain, rwkv_ln_bias, w_out, ffn_norm_gain, w_gate, w_up, w_down, rel_bias_table):
    h = x
    for l in range(attn_norm_gain.shape[0]):
        h = _layer(h, attn_norm_gain[l], w_in[l], nsa_q_gain[l], nsa_k_gain[l], cmp_pe[l], cmp_w1[l], cmp_b1[l],
                   cmp_w2[l], cmp_b2[l], nsa_out_gain[l], rwkv_shift_mix[l], rwkv_w0[l], rwkv_w2[l], rwkv_a0[l],
                   rwkv_a2[l], rwkv_g2[l], rwkv_k_k[l], rwkv_k_a[l], rwkv_r_k[l], rwkv_ln_gain[l],
                   rwkv_ln_bias[l], w_out[l], ffn_norm_gain[l], w_gate[l], w_up[l], w_down[l], rel_bias_table)
    return h
```

```python
import functools
import math

import numpy as np
import jax
import jax.numpy as jnp
from jax import lax
from jax.experimental import pallas as pl
from jax.experimental.pallas import tpu as pltpu

HEAD_DIM = 64
NSA_HEADS = 8
NSA_KV_HEADS = 2
NSA_HPG = NSA_HEADS // NSA_KV_HEADS
NSA_WIDTH = NSA_HEADS * HEAD_DIM
KV_WIDTH = NSA_KV_HEADS * HEAD_DIM
CMP_BLOCK = 32
CMP_STRIDE = 16
CMP_HIDDEN = 256
SEL_BLOCK = 64
SEL_TOP_N = 16
SEL_LOCAL = 2
WINDOW = 512
ATTN_SCALE = HEAD_DIM ** -0.5
LOG2E = math.log2(math.e)
NEG_INF = -1e30
FORCE_SCORE = 1e9
REL_BUCKETS = 32
REL_MAX_DIST = 128
RWKV_HEADS = 8
RWKV_HEAD_DIM = 64
RWKV_WIDTH = RWKV_HEADS * RWKV_HEAD_DIM
DECAY_LORA = 64
ICLR_LORA = 64
GATE_LORA = 160
GATE_LORA_PAD = 256
GN_EPS = 64e-5
RMS_EPS = 1e-6

LANES = 128
Q_TILE = 128
TILES_PER_STEP = 2
V_ROWS = HEAD_DIM + 16
K_TILE = 128
WKV_TILE = 256
WKV_CHUNK = 64
WKV_BATCH = 2
ROW_TILE = 512
FFN_ROW_TILE = 1024
FF_CHUNK = 256
VMEM_LIMIT_BYTES = 56 * 1024 * 1024

F32 = jnp.float32
BF16 = jnp.bfloat16


def _dot(a, b):
    return jnp.dot(a.astype(BF16), b.astype(BF16), preferred_element_type=F32)


def _dot_nt(a, b):
    return lax.dot_general(a.astype(BF16), b.astype(BF16), (((1,), (1,)), ((), ())),
                           preferred_element_type=F32)


def _dot_tn(a, b):
    return lax.dot_general(a.astype(BF16), b.astype(BF16), (((0,), (0,)), ((), ())),
                           preferred_element_type=F32)


def _split3(x):
    hi = x.astype(BF16)
    r1 = x - hi.astype(F32)
    mid = r1.astype(BF16)
    lo = (r1 - mid.astype(F32)).astype(BF16)
    return hi, mid, lo


def _dot_exact_lhs(m_bf16, x):
    hi, mid, lo = _split3(x)
    acc = jnp.dot(m_bf16, lo, preferred_element_type=F32)
    acc = acc + jnp.dot(m_bf16, mid, preferred_element_type=F32)
    return acc + jnp.dot(m_bf16, hi, preferred_element_type=F32)


def _rel_bucket_np(dist):
    max_exact = REL_BUCKETS // 2
    d = np.maximum(dist, 0)
    ratio = np.maximum(d, 1).astype(np.float32) / np.float32(max_exact)
    log_ratio = np.log(ratio).astype(np.float32) / np.float32(math.log(REL_MAX_DIST / max_exact))
    large = np.minimum(max_exact + (log_ratio * np.float32(REL_BUCKETS - max_exact)).astype(np.int32),
                       REL_BUCKETS - 1)
    return np.where(d < max_exact, d, large).astype(np.int32)


def _inproj_kernel(x_ref, gain_ref, w_ref, qg_ref, kg_ref, mr_ref, ml_ref, mg_ref, kk_ref,
                   q_ref, kv_ref, gt_ref, zr_ref, zl_ref, zg_ref, kkn_ref,
                   cr_ref, cl_ref, cg_ref, *, widths, tiles_per_seq):
    i = pl.program_id(0)
    tm = x_ref.shape[0]

    @pl.when(i == 0)
    def _init_carries():
        for carry_ref in (cr_ref, cl_ref, cg_ref):
            carry_ref[...] = jnp.zeros_like(carry_ref)

    x = x_ref[...]
    ms = jnp.mean(x * x, axis=-1, keepdims=True)
    xn = (x * lax.rsqrt(ms + RMS_EPS) * gain_ref[...]).astype(BF16)
    offs = np.concatenate([[0], np.cumsum(widths)])
    seg = lambda k: jnp.dot(xn, w_ref[:, int(offs[k]):int(offs[k + 1])], preferred_element_type=F32)
    lo = lax.broadcasted_iota(jnp.int32, (1, LANES), 1) < HEAD_DIM

    def slab_rms(u, gain):
        s_lo = jnp.sum(jnp.where(lo, u * u, 0.0), axis=-1, keepdims=True)
        s_hi = jnp.sum(jnp.where(lo, 0.0, u * u), axis=-1, keepdims=True)
        inv = lambda s: lax.rsqrt(s * (1.0 / HEAD_DIM) + RMS_EPS)
        return u * jnp.where(lo, inv(s_lo), inv(s_hi)) * gain

    first = (i % tiles_per_seq) == 0
    row = lax.broadcasted_iota(jnp.int32, (tm, 1), 0)
    for k, z_ref, mix_ref, carry_ref in ((3, zr_ref, mr_ref, cr_ref), (4, zl_ref, ml_ref, cl_ref),
                                         (5, zg_ref, mg_ref, cg_ref)):
        z = seg(k)
        carry = jnp.where(first, 0.0, carry_ref[...])
        prev = jnp.where(row == 0, carry, pltpu.roll(z, 1, 0))
        carry_ref[...] = z[tm - 1:tm, :]
        zs = z + mix_ref[...] * (prev - z)
        z_ref[...] = zs
        if k == 3:
            for p in range(RWKV_WIDTH // LANES):
                sl = slice(p * LANES, (p + 1) * LANES)
                kkh = zs[:, RWKV_WIDTH + p * LANES:RWKV_WIDTH + (p + 1) * LANES] * kk_ref[:, sl]
                s_lo = jnp.sum(jnp.where(lo, kkh * kkh, 0.0), axis=-1, keepdims=True)
                s_hi = jnp.sum(jnp.where(lo, 0.0, kkh * kkh), axis=-1, keepdims=True)
                inv = lambda s: lax.rsqrt(jnp.maximum(s, 1e-24))
                kkn_ref[:, sl] = kkh * jnp.where(lo, inv(s_lo), inv(s_hi))
    q = seg(0)
    for p in range(NSA_WIDTH // LANES):
        sl = slice(p * LANES, (p + 1) * LANES)
        q_ref[:, sl] = slab_rms(q[:, sl], qg_ref[...]) * (ATTN_SCALE * LOG2E)
    kv = seg(1)
    for b in (2, 4, 0, 1, 3, 5):
        sl = slice(b * KV_WIDTH, (b + 1) * KV_WIDTH)
        kv_ref[:, sl] = slab_rms(kv[:, sl], kg_ref[b // 2 - 1:b // 2, :]) if b in (2, 4) else kv[:, sl]
    gt_ref[...] = seg(2)


def _in_projection(x2, gain, w_cat, widths, seq, q_gain, k_gain, mix_r, mix_l, mix_g, k_k):
    n, d = x2.shape
    tm = min(ROW_TILE, n)
    assert seq % tm == 0 and KV_WIDTH == LANES
    total = sum(widths)
    out_widths = tuple(widths) + (RWKV_WIDTH,)
    full = lambda shape: pl.BlockSpec(shape, lambda i: (0, 0))
    row = lambda a: a.reshape(1, -1)
    return pl.pallas_call(
        functools.partial(_inproj_kernel, widths=widths, tiles_per_seq=seq // tm),
        grid=(n // tm,),
        in_specs=[
            pl.BlockSpec((tm, d), lambda i: (i, 0)),
            full((1, d)), full((d, total)),
            full((1, LANES)), full((2, LANES)),
            full((1, widths[3])), full((1, widths[4])), full((1, widths[5])), full((1, RWKV_WIDTH)),
        ],
        out_specs=[pl.BlockSpec((tm, w), lambda i: (i, 0)) for w in out_widths],
        out_shape=[jax.ShapeDtypeStruct((n, w), F32) for w in out_widths],
        scratch_shapes=[pltpu.VMEM((1, w), F32) for w in widths[3:6]],
        compiler_params=pltpu.CompilerParams(
            dimension_semantics=("arbitrary",), vmem_limit_bytes=VMEM_LIMIT_BYTES),
        name="in_projection",
    )(x2, gain, w_cat, jnp.tile(q_gain, 2).reshape(1, LANES), jnp.tile(k_gain[1:3], (1, 2)),
      row(mix_r), row(mix_l), row(mix_g), row(k_k))


def _head_rms(u, gain):
    ms = jnp.mean(u * u, axis=-1, keepdims=True)
    return u * lax.rsqrt(ms + RMS_EPS) * gain


def _nsa_kernel(tab_ref,
                q_ref, kv_ref, gate_ref,
                kg_ref, pe_ref, w1_ref, b1_ref, w2_ref, b2_ref, og_ref,
                bk0_ref, bk1_ref, bkc_ref, ovl_ref,
                out_ref,
                kvpad_ref, kcmp_ref, vcmp_ref, ksel_ref, vsel_ref, kwin_ref, vwin_ref,
                btab_ref, ccan_ref, ybuf_ref,
                *, seq):
    b = pl.program_id(0)
    step = pl.program_id(1)
    n_cb = seq // CMP_STRIDE
    n_c = (seq - CMP_BLOCK) // CMP_STRIDE + 1
    n_sel = seq // SEL_BLOCK
    cols = NSA_HPG * Q_TILE

    @pl.when((b == 0) & (step == 0))
    def _build_bias_tables():
        bk0 = bk0_ref[...]
        bk1 = bk1_ref[...]
        bkc = bkc_ref[...]
        kk = lax.broadcasted_iota(jnp.int32, (K_TILE, Q_TILE), 0)
        qq = lax.broadcasted_iota(jnp.int32, (K_TILE, Q_TILE), 1)
        causal = jnp.where(kk <= qq, 0.0, NEG_INF).astype(F32)
        anti = jnp.where(kk > qq, 0.0, NEG_INF).astype(F32)
        for h in range(NSA_HEADS):
            t0 = jnp.zeros((K_TILE, Q_TILE), F32)
            t1 = jnp.zeros((K_TILE, Q_TILE), F32)
            tc = jnp.zeros((2 * LANES, Q_TILE), F32)
            for k in range(REL_BUCKETS):
                val = tab_ref[k, h]
                t0 = jnp.where(bk0 == k, val, t0)
                t1 = jnp.where(bk1 == k, val, t1)
                tc = jnp.where(bkc == k, val, tc)
            far = tab_ref[REL_BUCKETS - 1, h]
            g, p = divmod(h, NSA_HPG)
            cs = slice(p * Q_TILE, (p + 1) * Q_TILE)
            btab_ref[g, 0, :, cs] = jnp.zeros((K_TILE, Q_TILE), F32)
            btab_ref[g, 1, :, cs] = (t1 - far) * LOG2E
            btab_ref[g, 2, :, cs] = (t0 - far) * LOG2E + causal
            btab_ref[g, 3, :, cs] = anti
            btab_ref[g, 4, :, cs] = jnp.full((K_TILE, Q_TILE), NEG_INF, F32)
            ccan_ref[h] = tc * LOG2E
        blk = lax.broadcasted_iota(jnp.int32, (seq, HEAD_DIM), 0) // SEL_BLOCK
        col = lax.broadcasted_iota(jnp.int32, (seq, HEAD_DIM), 1)
        onehot = (blk == col).astype(BF16)
        padcol = (lax.broadcasted_iota(jnp.int32, (WINDOW, 2 * HEAD_DIM), 1) == HEAD_DIM).astype(BF16)
        for g in range(NSA_KV_HEADS):
            ksel_ref[g, :, HEAD_DIM:2 * HEAD_DIM] = onehot
            kwin_ref[g, 0:WINDOW, :] = padcol
            kwin_ref[g, WINDOW:WINDOW + seq, HEAD_DIM:2 * HEAD_DIM] = jnp.zeros((seq, HEAD_DIM), BF16)
            vwin_ref[g, 0:HEAD_DIM, 0:WINDOW] = jnp.zeros((HEAD_DIM, WINDOW), BF16)
            for vt_ref, width in ((vsel_ref, seq), (vwin_ref, WINDOW + seq)):
                ones_row = lax.broadcasted_iota(jnp.int32, (V_ROWS - HEAD_DIM, width), 0) == 0
                vt_ref[g, HEAD_DIM:V_ROWS, :] = ones_row.astype(BF16)

    @pl.when(step == 0)
    def _per_batch():
        for which in range(2):
            kvpad_ref[which, seq:seq + 2 * CMP_STRIDE, :] = jnp.zeros((2 * CMP_STRIDE, KV_WIDTH), F32)
            kvpad_ref[which, 0:seq, :] = kv_ref[0, :, which * KV_WIDTH:(which + 1) * KV_WIDTH]
            first = lax.broadcasted_iota(jnp.int32, (1, KV_WIDTH), 1) < HEAD_DIM
            slabs = [[], []]
            for l in range(0, CMP_BLOCK, 2):
                ra = kvpad_ref[which, pl.ds(l, n_cb, stride=CMP_STRIDE), :]
                rb = kvpad_ref[which, pl.ds(l + 1, n_cb, stride=CMP_STRIDE), :]
                slabs[0].append(jnp.where(first, ra, pltpu.roll(rb, HEAD_DIM, 1)))
                slabs[1].append(jnp.where(first, pltpu.roll(ra, HEAD_DIM, 1), rb))
            cmps = []
            for g in range(NSA_KV_HEADS):
                blk = jnp.concatenate(slabs[g], axis=1) + pe_ref[which]
                hid = jax.nn.gelu(_dot(blk, w1_ref[which]) + b1_ref[which], approximate=True)
                cmps.append(_dot(hid, w2_ref[which]) + b2_ref[which])
            if which == 0:
                for g in range(NSA_KV_HEADS):
                    kcmp_ref[g] = _head_rms(cmps[g], kg_ref[0:1, :]).astype(BF16)
            else:
                vct = jnp.concatenate(cmps, axis=1).T
                for g in range(NSA_KV_HEADS):
                    vcmp_ref[g] = vct[g * HEAD_DIM:(g + 1) * HEAD_DIM].astype(BF16)
        kv = kv_ref[0]
        for g in range(NSA_KV_HEADS):
            def col(i):
                c = (2 + i) * KV_WIDTH + g * HEAD_DIM
                return kv[:, c:c + HEAD_DIM]
            ksel_ref[g, :, 0:HEAD_DIM] = col(0).astype(BF16)
            kwin_ref[g, WINDOW:WINDOW + seq, 0:HEAD_DIM] = col(2).astype(BF16)
        for i, vt_ref, pad in ((1, vsel_ref, 0), (3, vwin_ref, WINDOW)):
            vt = kv[:, (2 + i) * KV_WIDTH:(3 + i) * KV_WIDTH].T
            for g in range(NSA_KV_HEADS):
                vt_ref[g, 0:HEAD_DIM, pad:pad + seq] = vt[g * HEAD_DIM:(g + 1) * HEAD_DIM].astype(BF16)

    n_selp = -(-n_sel // 8) * 8
    qT_all = q_ref[0].T
    gates_all = jax.nn.sigmoid(gate_ref[0].T)
    tiles = range(TILES_PER_STEP)
    groups = range(NSA_KV_HEADS)
    units = [(j, g) for j in tiles for g in groups]
    each = lambda f: {u: f(*u) for u in units}
    qts = [TILES_PER_STEP * step + j for j in tiles]
    tqs = [qts[j] * Q_TILE + lax.broadcasted_iota(jnp.int32, (1, Q_TILE), 1) for j in tiles]
    lanes = lambda j: slice(j * Q_TILE, (j + 1) * Q_TILE)

    def q_group(j, g):
        qs = [qT_all[h * HEAD_DIM:(h + 1) * HEAD_DIM, lanes(j)].astype(BF16)
              for h in range(g * NSA_HPG, (g + 1) * NSA_HPG)]
        return jnp.concatenate(qs, axis=1)

    q_cats = each(q_group)
    aug_row = lax.broadcasted_iota(jnp.int32, (HEAD_DIM, cols), 0)

    cc = lax.broadcasted_iota(jnp.int32, (n_cb, Q_TILE), 0)
    valid4 = [jnp.concatenate([(cc * CMP_STRIDE + (CMP_BLOCK - 1) <= tqs[j]) & (cc < n_c)] * NSA_HPG, axis=1)
              for j in tiles]
    c_starts = [pl.multiple_of(LANES - (CMP_STRIDE // 2) * qts[j], 8) for j in tiles]
    ss = each(lambda j, g: jnp.dot(kcmp_ref[g], q_cats[j, g], preferred_element_type=F32))
    ss = each(lambda j, g: jnp.where(valid4[j], ss[j, g] + jnp.concatenate(
        [ccan_ref[g * NSA_HPG + p, pl.ds(c_starts[j], n_cb), :] for p in range(NSA_HPG)], axis=1), NEG_INF))
    es = each(lambda j, g: jnp.exp2(ss[j, g] - jnp.max(ss[j, g], axis=0, keepdims=True)))
    pcs = each(lambda j, g: es[j, g] * (1.0 / jnp.sum(es[j, g], axis=0, keepdims=True))
               * valid4[j].astype(F32))
    o_cmps = each(lambda j, g: jnp.dot(vcmp_ref[g], pcs[j, g].astype(BF16), preferred_element_type=F32))
    psums = each(lambda j, g: functools.reduce(
        lambda u, w: u + w, [pcs[j, g][:, p * Q_TILE:(p + 1) * Q_TILE] for p in range(NSA_HPG)]))

    n_wb = WINDOW // K_TILE
    band = WINDOW + K_TILE
    band_kinds = (3,) + (0,) * (n_wb - 2) + (1, 2)
    w0s = [pl.multiple_of(qts[j] * K_TILE, K_TILE) for j in tiles]
    pad_rows = jnp.where(aug_row == 0, NEG_INF, 0.0).astype(BF16)

    def band_logits(g, s):
        blocks = []
        for i, kind in enumerate(band_kinds):
            si = s[i * K_TILE:(i + 1) * K_TILE]
            blocks.append(si if kind == 0 else si + btab_ref[g, kind])
        return jnp.concatenate(blocks, axis=0)

    sw = each(lambda j, g: jnp.dot(kwin_ref[g, pl.ds(w0s[j], band), :],
                                   jnp.concatenate([q_cats[j, g], pad_rows], axis=0),
                                   preferred_element_type=F32))
    sw = each(lambda j, g: band_logits(g, sw[j, g]))
    pws = each(lambda j, g: jnp.exp2(sw[j, g] - jnp.max(sw[j, g], axis=0, keepdims=True)))
    o_wins = each(lambda j, g: jnp.dot(vwin_ref[g, :, pl.ds(w0s[j], band)], pws[j, g].astype(BF16),
                                       preferred_element_type=F32))
    o_wins = each(lambda j, g: o_wins[j, g][0:HEAD_DIM] * (1.0 / o_wins[j, g][HEAD_DIM:HEAD_DIM + 1]))

    imps = each(lambda j, g: _dot_exact_lhs(ovl_ref[...], psums[j, g])[0:n_selp])
    jb = lax.broadcasted_iota(jnp.int32, (n_selp, Q_TILE), 0)

    def masked_score(j, g):
        qb = tqs[j] // SEL_BLOCK
        forced = (jb == 0) | ((jb <= qb) & (jb > qb - SEL_LOCAL))
        score = jnp.where(forced, FORCE_SCORE, imps[j, g])
        score = jnp.where(jb <= qb, score, -1.0)
        return jnp.where(jb < n_sel, score, -2.0)

    scores = each(masked_score)
    sub = 8
    sub_row = lax.broadcasted_iota(jnp.int32, (sub, Q_TILE), 0)
    score_rows = {u: [scores[u][r0:r0 + sub] for r0 in range(0, n_selp, sub)] for u in units}
    rank_rows = {u: [jnp.zeros((sub, Q_TILE), F32) for _ in range(0, n_selp, sub)] for u in units}
    for i in range(n_sel):
        for u in units:
            si = scores[u][i:i + 1, :]
            for v, s_v in enumerate(score_rows[u]):
                r0 = v * sub
                if r0 > i:
                    ahead = si >= s_v
                elif r0 + sub - 1 < i:
                    ahead = si > s_v
                else:
                    ahead = (si > s_v) | ((si == s_v) & (sub_row + r0 > i))
                rank_rows[u][v] = rank_rows[u][v] + ahead.astype(F32)
    ranks = {u: jnp.concatenate(rank_rows[u], axis=0) for u in units}

    def q_with_mask_rows(j, g):
        sel = (ranks[j, g] < float(min(SEL_TOP_N, n_sel))) & (jb < n_sel)
        nm = (sel.astype(F32) - 1.0) * (-NEG_INF)
        return jnp.concatenate([q_cats[j, g], jnp.concatenate([nm] * NSA_HPG, axis=1).astype(BF16),
                                jnp.zeros((HEAD_DIM - n_selp, cols), BF16)], axis=0)

    q_sels = each(q_with_mask_rows)

    pair = 2 * K_TILE

    n_pairs = qts[-1] // 2 + 1
    n_far = jnp.maximum((qts[0] - 1) // 2, 0)

    def sel_step(kp, carry, near):
        k0 = pl.multiple_of(kp * pair, pair)
        old = {u: carry[i] for i, u in enumerate(units)}

        def logits(j, g):
            s = jnp.dot(ksel_ref[g, pl.ds(k0, pair), :], q_sels[j, g], preferred_element_type=F32)
            if near:
                rel = qts[j] - 2 * kp
                kind_of = lambda d: jnp.where(d >= 2, 0, jnp.where(d == 1, 1, jnp.where(d == 0, 2, 4)))
                kind_a, kind_b = kind_of(rel), kind_of(rel - 1)
                s = s + jnp.concatenate([btab_ref[g, kind_a], btab_ref[g, kind_b]], axis=0)
            return s

        ss = each(logits)
        m_new = each(lambda j, g: jnp.maximum(old[j, g][0], jnp.max(ss[j, g], axis=0, keepdims=True)))
        alpha = each(lambda j, g: jnp.exp2(old[j, g][0] - m_new[j, g]))
        pv = each(lambda j, g: jnp.dot(vsel_ref[g, :, pl.ds(k0, pair)],
                                       jnp.exp2(ss[j, g] - m_new[j, g]).astype(BF16),
                                       preferred_element_type=F32))
        return tuple((m_new[u], alpha[u] * old[u][1] + pv[u]) for u in units)

    init = (jnp.full((1, cols), NEG_INF, F32), jnp.zeros((V_ROWS, cols), F32))
    sel_mid = lax.fori_loop(0, n_far, functools.partial(sel_step, near=False), tuple(init for _ in units))
    sel_out = lax.fori_loop(n_far, n_pairs, functools.partial(sel_step, near=True), sel_mid)

    ssqs = []
    for j in tiles:
        gates = gates_all[:, lanes(j)]
        ssq = jnp.zeros((1, Q_TILE), F32)
        for g in groups:
            _, acc_s = sel_out[units.index((j, g))]
            o_sel = acc_s[0:HEAD_DIM] * (1.0 / acc_s[HEAD_DIM:HEAD_DIM + 1])
            for p in range(NSA_HPG):
                h = g * NSA_HPG + p
                cs = slice(p * Q_TILE, (p + 1) * Q_TILE)
                o = (gates[h:h + 1] * o_cmps[j, g][:, cs]
                     + gates[NSA_HEADS + h:NSA_HEADS + h + 1] * o_sel[:, cs]
                     + gates[2 * NSA_HEADS + h:2 * NSA_HEADS + h + 1] * o_wins[j, g][:, cs])
                ssq = ssq + jnp.sum(o * o, axis=0, keepdims=True)
                ybuf_ref[h * HEAD_DIM:(h + 1) * HEAD_DIM, lanes(j)] = o
        ssqs.append(ssq)
    ssq_all = jnp.concatenate(ssqs, axis=1)
    yT = ybuf_ref[...] * lax.rsqrt(ssq_all / NSA_WIDTH + RMS_EPS) * og_ref[...]
    out_ref[0] = yT.T.astype(out_ref.dtype)


def _nsa_mixer(q, kv, gates, cmp_k_gain, cmp_pe, cmp_w1, cmp_b1, cmp_w2, cmp_b2, out_gain, rel_table):
    bsz, seq, _ = q.shape
    assert TILES_PER_STEP % 2 == 0 and seq % (TILES_PER_STEP * Q_TILE) == 0
    assert seq // CMP_STRIDE == LANES and seq // SEL_BLOCK <= HEAD_DIM
    n_steps = seq // (TILES_PER_STEP * Q_TILE)
    q_rows = TILES_PER_STEP * Q_TILE
    n_cb = seq // CMP_STRIDE
    n_sel = seq // SEL_BLOCK
    n_selp = -(-n_sel // 8) * 8
    cols = NSA_HPG * Q_TILE

    kk = np.arange(K_TILE)[:, None]
    qq = np.arange(Q_TILE)[None, :]
    bk0 = _rel_bucket_np(qq - kk)
    bk1 = _rel_bucket_np(K_TILE + qq - kk)
    cprime = np.arange(2 * LANES)[:, None] - LANES
    bkc = _rel_bucket_np(qq - (CMP_BLOCK - 1) - CMP_STRIDE * cprime)
    c_start = np.arange(n_cb)[None, :] * CMP_STRIDE
    j_start = np.arange(LANES)[:, None] * SEL_BLOCK
    ovl = np.clip(np.minimum(c_start + CMP_BLOCK, j_start + SEL_BLOCK) - np.maximum(c_start, j_start), 0, None)
    ovl = (ovl.astype(np.float32) / CMP_STRIDE)
    ovl[n_sel:, :] = 0.0

    assert NSA_KV_HEADS == 2 and KV_WIDTH == LANES
    w1 = cmp_w1.astype(BF16)
    full = lambda shape: pl.BlockSpec(shape, lambda b, t: (0,) * len(shape))
    kernel = functools.partial(_nsa_kernel, seq=seq)
    return pl.pallas_call(
        kernel,
        grid=(bsz, n_steps),
        in_specs=[
            pl.BlockSpec(memory_space=pltpu.SMEM),
            pl.BlockSpec((1, q_rows, NSA_WIDTH), lambda b, t: (b, t, 0)),
            pl.BlockSpec((1, seq, 6 * KV_WIDTH), lambda b, t: (b, 0, 0)),
            pl.BlockSpec((1, q_rows, LANES), lambda b, t: (b, t, 0)),
            full((1, HEAD_DIM)),
            full((2, 1, CMP_BLOCK * HEAD_DIM)),
            full((2, CMP_BLOCK * HEAD_DIM, CMP_HIDDEN)),
            full((2, 1, CMP_HIDDEN)),
            full((2, CMP_HIDDEN, HEAD_DIM)),
            full((2, 1, HEAD_DIM)),
            full((NSA_WIDTH, q_rows)),
            full((K_TILE, Q_TILE)),
            full((K_TILE, Q_TILE)),
            full((2 * LANES, Q_TILE)),
            full((LANES, n_cb)),
        ],
        out_specs=pl.BlockSpec((1, q_rows, NSA_WIDTH), lambda b, t: (b, t, 0)),
        out_shape=jax.ShapeDtypeStruct((bsz, seq, NSA_WIDTH), BF16),
        scratch_shapes=[
            pltpu.VMEM((2, seq + 2 * CMP_STRIDE, KV_WIDTH), F32),
            pltpu.VMEM((NSA_KV_HEADS, n_cb, HEAD_DIM), BF16),
            pltpu.VMEM((NSA_KV_HEADS, HEAD_DIM, n_cb), BF16),
            pltpu.VMEM((NSA_KV_HEADS, seq, 2 * HEAD_DIM), BF16),
            pltpu.VMEM((NSA_KV_HEADS, V_ROWS, seq), BF16),
            pltpu.VMEM((NSA_KV_HEADS, WINDOW + seq, 2 * HEAD_DIM), BF16),
            pltpu.VMEM((NSA_KV_HEADS, V_ROWS, WINDOW + seq), BF16),
            pltpu.VMEM((NSA_KV_HEADS, 5, K_TILE, cols), F32),
            pltpu.VMEM((NSA_HEADS, 2 * LANES, Q_TILE), F32),
            pltpu.VMEM((NSA_WIDTH, q_rows), F32),
        ],
        compiler_params=pltpu.CompilerParams(
            dimension_semantics=("arbitrary", "arbitrary"), vmem_limit_bytes=VMEM_LIMIT_BYTES),
        name="nsa_mixer",
    )(rel_table, q, kv, gates, cmp_k_gain.reshape(1, HEAD_DIM),
      cmp_pe.reshape(2, 1, CMP_BLOCK * HEAD_DIM), w1,
      cmp_b1.reshape(2, 1, CMP_HIDDEN), cmp_w2.astype(BF16), cmp_b2.reshape(2, 1, HEAD_DIM),
      jnp.broadcast_to(out_gain[:, None], (NSA_WIDTH, q_rows)), jnp.asarray(bk0), jnp.asarray(bk1),
      jnp.asarray(bkc), jnp.asarray(ovl, BF16))


def _rwkv_kernel(zr_ref, zl_ref, zg_ref, kkn_ref, w0_ref, w2_ref, a0_ref, a2_ref, g2_ref,
                 ka_ref, rk_ref, lng_ref, lnb_ref, tri_ref,
                 out_ref,
                 state_ref):
    t = pl.program_id(1)
    tt = WKV_TILE
    n = RWKV_HEAD_DIM
    nb = zr_ref.shape[0]
    nck_seq = tt // WKV_CHUNK
    nck = nb * nck_seq

    @pl.when(t == 0)
    def _reset():
        state_ref[...] = jnp.zeros_like(state_ref)

    stacked = lambda z_ref: jnp.concatenate([z_ref[i] for i in range(nb)], axis=0)
    zr = stacked(zr_ref)
    zl = stacked(zl_ref)
    zg = stacked(zg_ref)
    r = zr[:, 0:RWKV_WIDTH]
    k = zr[:, RWKV_WIDTH:2 * RWKV_WIDTH]
    v = zr[:, 2 * RWKV_WIDTH:3 * RWKV_WIDTH]

    xw = w0_ref[...] + _dot(jnp.tanh(zl[:, 0:DECAY_LORA]), w2_ref[...])
    lw = -math.exp(-0.5) * jax.nn.sigmoid(xw)
    a = jax.nn.sigmoid(a0_ref[...] + _dot(zl[:, DECAY_LORA:DECAY_LORA + ICLR_LORA], a2_ref[...]))
    gate = _dot(jax.nn.sigmoid(zg), g2_ref[...])

    cum = jnp.concatenate([_dot_exact_lhs(tri_ref[...], lw[i * tt:(i + 1) * tt]) for i in range(nb)],
                          axis=0)
    cum_end = cum.reshape(nck, WKV_CHUNK, RWKV_WIDTH)[:, WKV_CHUNK - 1:WKV_CHUNK, :]
    cum_end_b = jnp.broadcast_to(cum_end, (nck, WKV_CHUNK, RWKV_WIDTH)).reshape(nb * tt, RWKV_WIDTH)
    e_incl = jnp.exp(cum)
    e_excl = jnp.exp(cum - lw)
    e_neg = jnp.exp(-cum)
    e_rev = jnp.exp(cum_end_b - cum)
    g_end = jnp.exp(cum_end)

    cw = WKV_CHUNK
    lane = lax.broadcasted_iota(jnp.int32, (1, 2 * n), 1)
    lo = lane < n
    ri = lax.broadcasted_iota(jnp.int32, (cw, 4 * n), 0)
    ci = lax.broadcasted_iota(jnp.int32, (cw, 4 * n), 1) % n
    strict = ri > ci
    incl = ri >= ci
    eye_row = (ri == ci).astype(F32)
    i2 = lax.broadcasted_iota(jnp.int32, (2 * n, 2 * n), 0)
    j2 = lax.broadcasted_iota(jnp.int32, (2 * n, 2 * n), 1)
    same_head = (i2 // n) == (j2 // n)
    eye_2n = (i2 == j2).astype(F32)
    zeros_slab = jnp.zeros((cw, 2 * n), BF16)

    def seg_sum(u, f=lambda s: s):
        s_lo = jnp.sum(jnp.where(lo, u, 0.0), axis=-1, keepdims=True)
        s_hi = jnp.sum(jnp.where(lo, 0.0, u), axis=-1, keepdims=True)
        return jnp.where(lo, f(s_lo), f(s_hi))

    def block_diag(u):
        ua, ub = u[:, 0:2 * n], u[:, 2 * n:4 * n]
        parts = []
        for w, first in ((ua, True), (ub, False)):
            for keep_lo in (True, False):
                blk = jnp.where(lo, w, 0.0) if keep_lo else jnp.where(lo, 0.0, w)
                blk = blk.astype(BF16)
                parts.append(jnp.concatenate([blk, zeros_slab] if first else [zeros_slab, blk], axis=1))
        return jnp.concatenate(parts, axis=0)

    rows = lambda c: slice(c * cw, (c + 1) * cw)

    n_pairs = RWKV_HEADS // 2
    slab = lambda p: slice(2 * n * p, 2 * n * (p + 1))
    pre = []
    for p in range(n_pairs):
        sl = slab(p)
        rp, kp, vp, ap = r[:, sl], k[:, sl], v[:, sl], a[:, sl]
        kkn = jnp.concatenate([kkn_ref[i, :, sl] for i in range(nb)], axis=0)
        kmod = kp * (1.0 + (ap - 1.0) * ka_ref[:, sl])
        bv = kkn * ap
        a_t = -kkn * e_excl[:, sl]
        r_t = rp * e_incl[:, sl]
        b_hat = bv * e_neg[:, sl]
        k_hat = kmod * e_neg[:, sl]
        mms = []
        for c in range(nck):
            bh, kh = b_hat[rows(c)], k_hat[rows(c)]
            rhs = jnp.concatenate([jnp.where(lo, bh, 0.0), jnp.where(lo, 0.0, bh),
                                   jnp.where(lo, kh, 0.0), jnp.where(lo, 0.0, kh)], axis=0)
            mms.append(_dot_nt(jnp.concatenate([a_t[rows(c)], r_t[rows(c)]], axis=0), rhs))
        pre.append(dict(r=rp, v=vp, kmod=kmod, a_t=a_t, r_t=r_t, b_end=bv * e_rev[:, sl],
                        k_end=kmod * e_rev[:, sl], mms=mms))

    chains = [(p, c0) for p in range(n_pairs) for c0 in range(0, nck, 2)]

    def pair_row(c0, f):
        return jnp.concatenate([f(c0), f(c0 + 1)], axis=1)

    quads = []
    for p, c0 in chains:
        mms = pre[p]["mms"]
        quads.append((
            jnp.where(strict, pair_row(c0, lambda c: mms[c][0:cw, 0:2 * n]), 0.0),
            jnp.where(strict, pair_row(c0, lambda c: mms[c][0:cw, 2 * n:4 * n]), 0.0),
            jnp.where(incl, pair_row(c0, lambda c: mms[c][cw:2 * cw, 0:2 * n]), 0.0),
            jnp.where(incl, pair_row(c0, lambda c: mms[c][cw:2 * cw, 2 * n:4 * n]), 0.0)))
    tinvs = [eye_row + q[0] for q in quads]
    xs = [_dot(q[0], block_diag(q[0])) for q in quads]
    for _ in range(int(math.log2(cw)) - 2):
        txs = [_dot(jnp.concatenate([t_, x_], axis=0), block_diag(x_)) for t_, x_ in zip(tinvs, xs)]
        tinvs = [t_ + tx[0:cw] for t_, tx in zip(tinvs, txs)]
        xs = [tx[cw:2 * cw] for tx in txs]
    tinvs = [t_ + _dot(t_, block_diag(x_)) for t_, x_ in zip(tinvs, xs)]
    lvs = [_dot(jnp.concatenate([q[1], q[3]], axis=0),
                block_diag(pair_row(c0, lambda c: pre[p]["v"][rows(c)])))
           for (p, c0), q in zip(chains, quads)]
    w_rows = [_dot(t_, block_diag(pair_row(c0, lambda c: pre[p]["a_t"][rows(c)])))
              for (p, c0), t_ in zip(chains, tinvs)]
    u_rows = [_dot(t_, block_diag(lv[0:cw])) for t_, lv in zip(tinvs, lvs)]
    qp_rows = [pair_row(c0, lambda c: pre[p]["r_t"][rows(c)]) + _dot(q[2], block_diag(w_))
               for (p, c0), q, w_ in zip(chains, quads, w_rows)]
    y0_rows = [_dot(q[2], block_diag(u_)) + lv[cw:2 * cw] for q, u_, lv in zip(quads, u_rows, lvs)]
    phs = {}
    for ci, (p, c0) in enumerate(chains):
        for j, c in enumerate((c0, c0 + 1)):
            ls = slice(2 * n * j, 2 * n * (j + 1))
            lhs = jnp.concatenate([pre[p]["b_end"][rows(c)], pre[p]["k_end"][rows(c)]], axis=0)
            rhs = jnp.concatenate(
                [jnp.concatenate([w_rows[ci][:, ls], u_rows[ci][:, ls]], axis=1),
                 jnp.concatenate([jnp.zeros((cw, 2 * n), F32), pre[p]["v"][rows(c)]], axis=1)], axis=0)
            ph = _dot_tn(lhs, rhs)
            pmat = eye_2n * g_end[c, :, slab(p)] + jnp.where(same_head, ph[:, 0:2 * n], 0.0)
            phs[(p, c)] = (jnp.concatenate([qp_rows[ci][:, ls], pmat], axis=0),
                           jnp.where(same_head, ph[:, 2 * n:4 * n], 0.0), y0_rows[ci][:, ls])
    seqs = [(i, p) for i in range(nb) for p in range(n_pairs)]
    hstates = {ip: state_ref[ip[0], ip[1]] for ip in seqs}
    ys = {ip: [] for ip in seqs}
    for c in range(nck_seq):
        for i, p in seqs:
            qpm, hp, y0 = phs[(p, i * nck_seq + c)]
            qph = _dot(qpm, hstates[i, p])
            ys[i, p].append(qph[0:cw] + y0)
            hstates[i, p] = qph[cw:cw + 2 * n] + hp
    for i, p in seqs:
        state_ref[i, p] = hstates[i, p]
    for p in range(n_pairs):
        sl = slab(p)
        y = jnp.concatenate([u for i in range(nb) for u in ys[i, p]], axis=0)
        yc = y - seg_sum(y, lambda s: s * (1.0 / n))
        inv_std = seg_sum(yc * yc, lambda s: lax.rsqrt(s * (1.0 / n) + GN_EPS))
        yn = yc * inv_std * lng_ref[:, sl] + lnb_ref[:, sl]
        bonus = seg_sum(pre[p]["r"] * pre[p]["kmod"] * rk_ref[:, sl]) * pre[p]["v"]
        res = ((yn + bonus) * gate[:, sl]).astype(out_ref.dtype)
        for i in range(nb):
            out_ref[i, :, sl] = res[i * tt:(i + 1) * tt]


def _rwkv_mixer(zr, zl, zg, kkn, w0, w2, a0, a2, g2p, k_a, r_k, ln_g, ln_b):
    bsz, seq, _ = zr.shape
    tt = WKV_TILE
    assert seq % tt == 0 and tt % (2 * WKV_CHUNK) == 0 and 2 * RWKV_HEAD_DIM == LANES
    nb = WKV_BATCH if bsz % WKV_BATCH == 0 else 1
    idx = np.arange(tt)
    tri = ((idx[:, None] // WKV_CHUNK == idx[None, :] // WKV_CHUNK) & (idx[:, None] >= idx[None, :]))
    full = lambda shape: pl.BlockSpec(shape, lambda b, t: (0,) * len(shape))
    row = lambda a: a.reshape(1, -1)
    wl = DECAY_LORA + ICLR_LORA
    return pl.pallas_call(
        _rwkv_kernel,
        grid=(bsz // nb, seq // tt),
        in_specs=[
            pl.BlockSpec((nb, tt, 3 * RWKV_WIDTH), lambda b, t: (b, t, 0)),
            pl.BlockSpec((nb, tt, wl), lambda b, t: (b, t, 0)),
            pl.BlockSpec((nb, tt, GATE_LORA_PAD), lambda b, t: (b, t, 0)),
            pl.BlockSpec((nb, tt, RWKV_WIDTH), lambda b, t: (b, t, 0)),
            full((1, RWKV_WIDTH)), full((DECAY_LORA, RWKV_WIDTH)),
            full((1, RWKV_WIDTH)), full((ICLR_LORA, RWKV_WIDTH)),
            full((GATE_LORA_PAD, RWKV_WIDTH)),
            full((1, RWKV_WIDTH)), full((1, RWKV_WIDTH)),
            full((1, RWKV_WIDTH)), full((1, RWKV_WIDTH)),
            full((tt, tt)),
        ],
        out_specs=pl.BlockSpec((nb, tt, RWKV_WIDTH), lambda b, t: (b, t, 0)),
        out_shape=jax.ShapeDtypeStruct((bsz, seq, RWKV_WIDTH), BF16),
        scratch_shapes=[
            pltpu.VMEM((nb, RWKV_HEADS // 2, 2 * RWKV_HEAD_DIM, 2 * RWKV_HEAD_DIM), F32),
        ],
        compiler_params=pltpu.CompilerParams(
            dimension_semantics=("arbitrary", "arbitrary"), vmem_limit_bytes=VMEM_LIMIT_BYTES),
        name="rwkv7_mixer",
    )(zr, zl, zg, kkn, row(w0), w2.astype(BF16), row(a0), a2.astype(BF16),
      g2p.astype(BF16), row(k_a), row(r_k), row(ln_g), row(ln_b), jnp.asarray(tri, BF16))


def _ffn_kernel(x_ref, yn_ref, yr_ref, won_ref, wor_ref, gain_ref, wg_ref, wu_ref, wd_ref, o_ref, *, ff_chunk):
    h = x_ref[...]
    h = h + jnp.dot(yn_ref[...], won_ref[...], preferred_element_type=F32)
    h = h + jnp.dot(yr_ref[...], wor_ref[...], preferred_element_type=F32)
    ms = jnp.mean(h * h, axis=-1, keepdims=True)
    hn = (h * lax.rsqrt(ms + RMS_EPS) * gain_ref[...]).astype(BF16)
    d_ff = wg_ref.shape[1]
    acc = None
    for c in range(d_ff // ff_chunk):
        cs = slice(c * ff_chunk, (c + 1) * ff_chunk)
        gte = jnp.dot(hn, wg_ref[:, cs], preferred_element_type=F32)
        up = jnp.dot(hn, wu_ref[:, cs], preferred_element_type=F32)
        act = (gte * jax.nn.sigmoid(gte) * up).astype(BF16)
        down = jnp.dot(act, wd_ref[cs, :], preferred_element_type=F32)
        acc = down if acc is None else acc + down
    o_ref[...] = h + acc


def _out_ffn(x2, y_nsa, y_rwkv, w_out, ffn_gain, w_gate, w_up, w_down):
    n, d = x2.shape
    d_ff = w_gate.shape[1]
    tm = min(FFN_ROW_TILE, n)
    ff_chunk = FF_CHUNK if d_ff % FF_CHUNK == 0 else d_ff
    const = lambda shape: pl.BlockSpec(shape, lambda i: (0, 0), pipeline_mode=pl.Buffered(1))
    return pl.pallas_call(
        functools.partial(_ffn_kernel, ff_chunk=ff_chunk),
        grid=(n // tm,),
        in_specs=[
            pl.BlockSpec((tm, d), lambda i: (i, 0)),
            pl.BlockSpec((tm, NSA_WIDTH), lambda i: (i, 0)),
            pl.BlockSpec((tm, RWKV_WIDTH), lambda i: (i, 0)),
            const((NSA_WIDTH, d)), const((RWKV_WIDTH, d)), const((1, d)),
            const((d, d_ff)), const((d, d_ff)), const((d_ff, d)),
        ],
        out_specs=pl.BlockSpec((tm, d), lambda i: (i, 0)),
        out_shape=jax.ShapeDtypeStruct((n, d), F32),
        compiler_params=pltpu.CompilerParams(
            dimension_semantics=("arbitrary",), vmem_limit_bytes=VMEM_LIMIT_BYTES),
        name="out_proj_ffn",
    )(x2, y_nsa, y_rwkv, w_out[:NSA_WIDTH].astype(BF16), w_out[NSA_WIDTH:].astype(BF16),
      ffn_gain.reshape(1, d), w_gate.astype(BF16), w_up.astype(BF16), w_down.astype(BF16))


def _layer(h, attn_gain, w_in, q_gain, k_gain, cmp_pe, cmp_w1, cmp_b1, cmp_w2, cmp_b2, out_gain, shift_mix,
           w0, w2, a0, a2, g2, k_k, k_a, r_k, ln_g, ln_b, w_out, ffn_gain, w_gate, w_up, w_down, rel_table):
    bsz, seq, d = h.shape
    nsa_cols = NSA_WIDTH + 6 * KV_WIDTH + 3 * NSA_HEADS
    gate0 = NSA_WIDTH + 6 * KV_WIDTH

    w_bf = w_in.astype(BF16)
    w_nsa, w_rw = w_bf[:, :nsa_cols], w_bf[:, nsa_cols:]
    gate_src = np.array([gate0 + h_ * 3 + br for br in range(3) for h_ in range(NSA_HEADS)])
    w_gates = jnp.zeros((d, LANES), BF16).at[:, :3 * NSA_HEADS].set(w_nsa[:, gate_src])
    lora0 = 3 * RWKV_WIDTH
    g0 = lora0 + DECAY_LORA + ICLR_LORA
    w_g = jnp.zeros((d, GATE_LORA_PAD), BF16).at[:, :GATE_LORA].set(w_rw[:, g0:g0 + GATE_LORA])
    widths = (NSA_WIDTH, 6 * KV_WIDTH, LANES, 3 * RWKV_WIDTH, DECAY_LORA + ICLR_LORA, GATE_LORA_PAD)
    w_cat = jnp.concatenate([w_nsa[:, :gate0], w_gates, w_rw[:, :g0], w_g], axis=1)
    mix_g = jnp.zeros((GATE_LORA_PAD,), F32).at[:GATE_LORA].set(shift_mix[g0:g0 + GATE_LORA])
    g2p = jnp.zeros((GATE_LORA_PAD, RWKV_WIDTH), F32).at[:GATE_LORA].set(g2)

    x2 = h.reshape(bsz * seq, d)
    q, kv, gates, zr, zl, zg, kkn = _in_projection(x2, attn_gain.reshape(1, d), w_cat, widths, seq, q_gain, k_gain,
                                                   shift_mix[:lora0], shift_mix[lora0:g0], mix_g, k_k)
    r3 = lambda u: u.reshape(bsz, seq, u.shape[-1])
    y_nsa = _nsa_mixer(r3(q), r3(kv), r3(gates), k_gain[0], cmp_pe, cmp_w1, cmp_b1, cmp_w2, cmp_b2,
                       out_gain, rel_table)
    y_rwkv = _rwkv_mixer(r3(zr), r3(zl), r3(zg), r3(kkn), w0, w2, a0, a2, g2p, k_a, r_k.reshape(-1), ln_g, ln_b)
    out = _out_ffn(x2, y_nsa.reshape(bsz * seq, NSA_WIDTH), y_rwkv.reshape(bsz * seq, RWKV_WIDTH),
                   w_out, ffn_gain, w_gate, w_up, w_down)
    return out.reshape(bsz, seq, d)


def kernel(x, attn_norm_gain, w_in, nsa_q_gain, nsa_k_gain, cmp_pe, cmp_w1, cmp_b1, cmp_w2, cmp_b2, nsa_out_gain, rwkv_shift_mix, rwkv_w0, rwkv_w2, rwkv_a0, rwkv_a2, rwkv_g2, rwkv_k_k, rwkv_k_a, rwkv_r_k, rwkv_ln_gain, rwkv_ln_bias, w_out, ffn_norm_gain, w_gate, w_up, w_down, rel_bias_table):
    h = x
    for l in range(attn_norm_gain.shape[0]):
        h = _layer(h, attn_norm_gain[l], w_in[l], nsa_q_gain[l], nsa_k_gain[l], cmp_pe[l], cmp_w1[l], cmp_b1[l],
                   cmp_w2[l], cmp_b2[l], nsa_out_gain[l], rwkv_shift_mix[l], rwkv_w0[l], rwkv_w2[l], rwkv_a0[l],
                   rwkv_a2[l], rwkv_g2[l], rwkv_k_k[l], rwkv_k_a[l], rwkv_r_k[l], rwkv_ln_gain[l],
                   rwkv_ln_bias[l], w_out[l], ffn_norm_gain[l], w_gate[l], w_up[l], w_down[l], rel_bias_table)
    return h
```

```python
import functools
import math

import numpy as np
import jax
import jax.numpy as jnp
from jax import lax
from jax.experimental import pallas as pl
from jax.experimental.pallas import tpu as pltpu

HEAD_DIM = 64
NSA_HEADS = 8
NSA_KV_HEADS = 2
NSA_HPG = NSA_HEADS // NSA_KV_HEADS
NSA_WIDTH = NSA_HEADS * HEAD_DIM
KV_WIDTH = NSA_KV_HEADS * HEAD_DIM
CMP_BLOCK = 32
CMP_STRIDE = 16
CMP_HIDDEN = 256
SEL_BLOCK = 64
SEL_TOP_N = 16
SEL_LOCAL = 2
WINDOW = 512
ATTN_SCALE = HEAD_DIM ** -0.5
LOG2E = math.log2(math.e)
NEG_INF = -1e30
FORCE_SCORE = 1e9
REL_BUCKETS = 32
REL_MAX_DIST = 128
RWKV_HEADS = 8
RWKV_HEAD_DIM = 64
RWKV_WIDTH = RWKV_HEADS * RWKV_HEAD_DIM
DECAY_LORA = 64
ICLR_LORA = 64
GATE_LORA = 160
GATE_LORA_PAD = 256
GN_EPS = 64e-5
RMS_EPS = 1e-6

LANES = 128
Q_TILE = 128
TILES_PER_STEP = 2
V_ROWS = HEAD_DIM + 16
K_TILE = 128
WKV_TILE = 256
WKV_CHUNK = 64
WKV_BATCH = 2
ROW_TILE = 512
FFN_ROW_TILE = 1024
FF_CHUNK = 256
VMEM_LIMIT_BYTES = 56 * 1024 * 1024

F32 = jnp.float32
BF16 = jnp.bfloat16


def _dot(a, b):
    return jnp.dot(a.astype(BF16), b.astype(BF16), preferred_element_type=F32)


def _dot_nt(a, b):
    return lax.dot_general(a.astype(BF16), b.astype(BF16), (((1,), (1,)), ((), ())),
                           preferred_element_type=F32)


def _dot_tn(a, b):
    return lax.dot_general(a.astype(BF16), b.astype(BF16), (((0,), (0,)), ((), ())),
                           preferred_element_type=F32)


def _split3(x):
    hi = x.astype(BF16)
    r1 = x - hi.astype(F32)
    mid = r1.astype(BF16)
    lo = (r1 - mid.astype(F32)).astype(BF16)
    return hi, mid, lo


def _dot_exact_lhs(m_bf16, x):
    hi, mid, lo = _split3(x)
    acc = jnp.dot(m_bf16, lo, preferred_element_type=F32)
    acc = acc + jnp.dot(m_bf16, mid, preferred_element_type=F32)
    return acc + jnp.dot(m_bf16, hi, preferred_element_type=F32)


def _rel_bucket_np(dist):
    max_exact = REL_BUCKETS // 2
    d = np.maximum(dist, 0)
    ratio = np.maximum(d, 1).astype(np.float32) / np.float32(max_exact)
    log_ratio = np.log(ratio).astype(np.float32) / np.float32(math.log(REL_MAX_DIST / max_exact))
    large = np.minimum(max_exact + (log_ratio * np.float32(REL_BUCKETS - max_exact)).astype(np.int32),
                       REL_BUCKETS - 1)
    return np.where(d < max_exact, d, large).astype(np.int32)


def _inproj_kernel(x_ref, gain_ref, w_ref, qg_ref, kg_ref, mr_ref, ml_ref, mg_ref, kk_ref,
                   q_ref, kv_ref, gt_ref, zr_ref, zl_ref, zg_ref, kkn_ref,
                   cr_ref, cl_ref, cg_ref, *, widths, tiles_per_seq):
    i = pl.program_id(0)
    tm = x_ref.shape[0]

    @pl.when(i == 0)
    def _init_carries():
        for carry_ref in (cr_ref, cl_ref, cg_ref):
            carry_ref[...] = jnp.zeros_like(carry_ref)

    x = x_ref[...]
    ms = jnp.mean(x * x, axis=-1, keepdims=True)
    xn = (x * lax.rsqrt(ms + RMS_EPS) * gain_ref[...]).astype(BF16)
    offs = np.concatenate([[0], np.cumsum(widths)])
    seg = lambda k: jnp.dot(xn, w_ref[:, int(offs[k]):int(offs[k + 1])], preferred_element_type=F32)
    lo = lax.broadcasted_iota(jnp.int32, (1, LANES), 1) < HEAD_DIM

    def slab_rms(u, gain):
        s_lo = jnp.sum(jnp.where(lo, u * u, 0.0), axis=-1, keepdims=True)
        s_hi = jnp.sum(jnp.where(lo, 0.0, u * u), axis=-1, keepdims=True)
        inv = lambda s: lax.rsqrt(s * (1.0 / HEAD_DIM) + RMS_EPS)
        return u * jnp.where(lo, inv(s_lo), inv(s_hi)) * gain

    first = (i % tiles_per_seq) == 0
    row = lax.broadcasted_iota(jnp.int32, (tm, 1), 0)
    for k, z_ref, mix_ref, carry_ref in ((3, zr_ref, mr_ref, cr_ref), (4, zl_ref, ml_ref, cl_ref),
                                         (5, zg_ref, mg_ref, cg_ref)):
        z = seg(k)
        carry = jnp.where(first, 0.0, carry_ref[...])
        prev = jnp.where(row == 0, carry, pltpu.roll(z, 1, 0))
        carry_ref[...] = z[tm - 1:tm, :]
        zs = z + mix_ref[...] * (prev - z)
        z_ref[...] = zs
        if k == 3:
            for p in range(RWKV_WIDTH // LANES):
                sl = slice(p * LANES, (p + 1) * LANES)
                kkh = zs[:, RWKV_WIDTH + p * LANES:RWKV_WIDTH + (p + 1) * LANES] * kk_ref[:, sl]
                s_lo = jnp.sum(jnp.where(lo, kkh * kkh, 0.0), axis=-1, keepdims=True)
                s_hi = jnp.sum(jnp.where(lo, 0.0, kkh * kkh), axis=-1, keepdims=True)
                inv = lambda s: lax.rsqrt(jnp.maximum(s, 1e-24))
                kkn_ref[:, sl] = kkh * jnp.where(lo, inv(s_lo), inv(s_hi))
    q = seg(0)
    for p in range(NSA_WIDTH // LANES):
        sl = slice(p * LANES, (p + 1) * LANES)
        q_ref[:, sl] = slab_rms(q[:, sl], qg_ref[...]) * (ATTN_SCALE * LOG2E)
    kv = seg(1)
    for b in (2, 4, 0, 1, 3, 5):
        sl = slice(b * KV_WIDTH, (b + 1) * KV_WIDTH)
        kv_ref[:, sl] = slab_rms(kv[:, sl], kg_ref[b // 2 - 1:b // 2, :]) if b in (2, 4) else kv[:, sl]
    gt_ref[...] = seg(2)


def _in_projection(x2, gain, w_cat, widths, seq, q_gain, k_gain, mix_r, mix_l, mix_g, k_k):
    n, d = x2.shape
    tm = min(ROW_TILE, n)
    assert seq % tm == 0 and KV_WIDTH == LANES
    total = sum(widths)
    out_widths = tuple(widths) + (RWKV_WIDTH,)
    full = lambda shape: pl.BlockSpec(shape, lambda i: (0, 0))
    row = lambda a: a.reshape(1, -1)
    return pl.pallas_call(
        functools.partial(_inproj_kernel, widths=widths, tiles_per_seq=seq // tm),
        grid=(n // tm,),
        in_specs=[
            pl.BlockSpec((tm, d), lambda i: (i, 0)),
            full((1, d)), full((d, total)),
            full((1, LANES)), full((2, LANES)),
            full((1, widths[3])), full((1, widths[4])), full((1, widths[5])), full((1, RWKV_WIDTH)),
        ],
        out_specs=[pl.BlockSpec((tm, w), lambda i: (i, 0)) for w in out_widths],
        out_shape=[jax.ShapeDtypeStruct((n, w), F32) for w in out_widths],
        scratch_shapes=[pltpu.VMEM((1, w), F32) for w in widths[3:6]],
        compiler_params=pltpu.CompilerParams(
            dimension_semantics=("arbitrary",), vmem_limit_bytes=VMEM_LIMIT_BYTES),
        name="in_projection",
    )(x2, gain, w_cat, jnp.tile(q_gain, 2).reshape(1, LANES), jnp.tile(k_gain[1:3], (1, 2)),
      row(mix_r), row(mix_l), row(mix_g), row(k_k))


def _head_rms(u, gain):
    ms = jnp.mean(u * u, axis=-1, keepdims=True)
    return u * lax.rsqrt(ms + RMS_EPS) * gain


def _nsa_kernel(tab_ref,
                q_ref, kv_ref, gate_ref,
                kg_ref, pe_ref, w1_ref, b1_ref, w2_ref, b2_ref, og_ref,
                bk0_ref, bk1_ref, bkc_ref, ovl_ref,
                out_ref,
                kvpad_ref, kcmp_ref, vcmp_ref, ksel_ref, vsel_ref, kwin_ref, vwin_ref,
                btab_ref, ccan_ref, ybuf_ref,
                *, seq):
    b = pl.program_id(0)
    step = pl.program_id(1)
    n_cb = seq // CMP_STRIDE
    n_c = (seq - CMP_BLOCK) // CMP_STRIDE + 1
    n_sel = seq // SEL_BLOCK
    cols = NSA_HPG * Q_TILE

    @pl.when((b == 0) & (step == 0))
    def _build_bias_tables():
        bk0 = bk0_ref[...]
        bk1 = bk1_ref[...]
        bkc = bkc_ref[...]
        kk = lax.broadcasted_iota(jnp.int32, (K_TILE, Q_TILE), 0)
        qq = lax.broadcasted_iota(jnp.int32, (K_TILE, Q_TILE), 1)
        causal = jnp.where(kk <= qq, 0.0, NEG_INF).astype(F32)
        anti = jnp.where(kk > qq, 0.0, NEG_INF).astype(F32)
        for h in range(NSA_HEADS):
            t0 = jnp.zeros((K_TILE, Q_TILE), F32)
            t1 = jnp.zeros((K_TILE, Q_TILE), F32)
            tc = jnp.zeros((2 * LANES, Q_TILE), F32)
            for k in range(REL_BUCKETS):
                val = tab_ref[k, h]
                t0 = jnp.where(bk0 == k, val, t0)
                t1 = jnp.where(bk1 == k, val, t1)
                tc = jnp.where(bkc == k, val, tc)
            far = tab_ref[REL_BUCKETS - 1, h]
            g, p = divmod(h, NSA_HPG)
            cs = slice(p * Q_TILE, (p + 1) * Q_TILE)
            btab_ref[g, 0, :, cs] = jnp.zeros((K_TILE, Q_TILE), F32)
            btab_ref[g, 1, :, cs] = (t1 - far) * LOG2E
            btab_ref[g, 2, :, cs] = (t0 - far) * LOG2E + causal
            btab_ref[g, 3, :, cs] = anti
            ccan_ref[h] = tc * LOG2E
        blk = lax.broadcasted_iota(jnp.int32, (seq, HEAD_DIM), 0) // SEL_BLOCK
        col = lax.broadcasted_iota(jnp.int32, (seq, HEAD_DIM), 1)
        onehot = (blk == col).astype(BF16)
        padcol = (lax.broadcasted_iota(jnp.int32, (WINDOW, 2 * HEAD_DIM), 1) == HEAD_DIM).astype(BF16)
        for g in range(NSA_KV_HEADS):
            ksel_ref[g, :, HEAD_DIM:2 * HEAD_DIM] = onehot
            kwin_ref[g, 0:WINDOW, :] = padcol
            kwin_ref[g, WINDOW:WINDOW + seq, HEAD_DIM:2 * HEAD_DIM] = jnp.zeros((seq, HEAD_DIM), BF16)
            vwin_ref[g, 0:HEAD_DIM, 0:WINDOW] = jnp.zeros((HEAD_DIM, WINDOW), BF16)
            for vt_ref, width in ((vsel_ref, seq), (vwin_ref, WINDOW + seq)):
                ones_row = lax.broadcasted_iota(jnp.int32, (V_ROWS - HEAD_DIM, width), 0) == 0
                vt_ref[g, HEAD_DIM:V_ROWS, :] = ones_row.astype(BF16)

    @pl.when(step == 0)
    def _per_batch():
        for which in range(2):
            kvpad_ref[which, seq:seq + 2 * CMP_STRIDE, :] = jnp.zeros((2 * CMP_STRIDE, KV_WIDTH), F32)
            kvpad_ref[which, 0:seq, :] = kv_ref[0, :, which * KV_WIDTH:(which + 1) * KV_WIDTH]
            first = lax.broadcasted_iota(jnp.int32, (1, KV_WIDTH), 1) < HEAD_DIM
            slabs = [[], []]
            for l in range(0, CMP_BLOCK, 2):
                ra = kvpad_ref[which, pl.ds(l, n_cb, stride=CMP_STRIDE), :]
                rb = kvpad_ref[which, pl.ds(l + 1, n_cb, stride=CMP_STRIDE), :]
                slabs[0].append(jnp.where(first, ra, pltpu.roll(rb, HEAD_DIM, 1)))
                slabs[1].append(jnp.where(first, pltpu.roll(ra, HEAD_DIM, 1), rb))
            cmps = []
            for g in range(NSA_KV_HEADS):
                blk = jnp.concatenate(slabs[g], axis=1) + pe_ref[which]
                hid = jax.nn.gelu(_dot(blk, w1_ref[which]) + b1_ref[which], approximate=True)
                cmps.append(_dot(hid, w2_ref[which]) + b2_ref[which])
            if which == 0:
                for g in range(NSA_KV_HEADS):
                    kcmp_ref[g] = _head_rms(cmps[g], kg_ref[0:1, :]).astype(BF16)
            else:
                vct = jnp.concatenate(cmps, axis=1).T
                for g in range(NSA_KV_HEADS):
                    vcmp_ref[g] = vct[g * HEAD_DIM:(g + 1) * HEAD_DIM].astype(BF16)
        kv = kv_ref[0]
        for g in range(NSA_KV_HEADS):
            def col(i):
                c = (2 + i) * KV_WIDTH + g * HEAD_DIM
                return kv[:, c:c + HEAD_DIM]
            ksel_ref[g, :, 0:HEAD_DIM] = col(0).astype(BF16)
            kwin_ref[g, WINDOW:WINDOW + seq, 0:HEAD_DIM] = col(2).astype(BF16)
        for i, vt_ref, pad in ((1, vsel_ref, 0), (3, vwin_ref, WINDOW)):
            vt = kv[:, (2 + i) * KV_WIDTH:(3 + i) * KV_WIDTH].T
            for g in range(NSA_KV_HEADS):
                vt_ref[g, 0:HEAD_DIM, pad:pad + seq] = vt[g * HEAD_DIM:(g + 1) * HEAD_DIM].astype(BF16)

    n_selp = -(-n_sel // 8) * 8
    qT_all = q_ref[0].T
    gates_all = jax.nn.sigmoid(gate_ref[0].T)
    tiles = range(TILES_PER_STEP)
    groups = range(NSA_KV_HEADS)
    units = [(j, g) for j in tiles for g in groups]
    each = lambda f: {u: f(*u) for u in units}
    qts = [TILES_PER_STEP * step + j for j in tiles]
    tqs = [qts[j] * Q_TILE + lax.broadcasted_iota(jnp.int32, (1, Q_TILE), 1) for j in tiles]
    lanes = lambda j: slice(j * Q_TILE, (j + 1) * Q_TILE)

    def q_group(j, g):
        qs = [qT_all[h * HEAD_DIM:(h + 1) * HEAD_DIM, lanes(j)].astype(BF16)
              for h in range(g * NSA_HPG, (g + 1) * NSA_HPG)]
        return jnp.concatenate(qs, axis=1)

    q_cats = each(q_group)
    aug_row = lax.broadcasted_iota(jnp.int32, (HEAD_DIM, cols), 0)

    cc = lax.broadcasted_iota(jnp.int32, (n_cb, Q_TILE), 0)
    valid4 = [jnp.concatenate([(cc * CMP_STRIDE + (CMP_BLOCK - 1) <= tqs[j]) & (cc < n_c)] * NSA_HPG, axis=1)
              for j in tiles]
    c_starts = [pl.multiple_of(LANES - (CMP_STRIDE // 2) * qts[j], 8) for j in tiles]
    ss = each(lambda j, g: jnp.dot(kcmp_ref[g], q_cats[j, g], preferred_element_type=F32))
    ss = each(lambda j, g: jnp.where(valid4[j], ss[j, g] + jnp.concatenate(
        [ccan_ref[g * NSA_HPG + p, pl.ds(c_starts[j], n_cb), :] for p in range(NSA_HPG)], axis=1), NEG_INF))
    es = each(lambda j, g: jnp.exp2(ss[j, g] - jnp.max(ss[j, g], axis=0, keepdims=True)))
    pcs = each(lambda j, g: es[j, g] * (1.0 / jnp.sum(es[j, g], axis=0, keepdims=True))
               * valid4[j].astype(F32))
    o_cmps = each(lambda j, g: jnp.dot(vcmp_ref[g], pcs[j, g].astype(BF16), preferred_element_type=F32))
    psums = each(lambda j, g: functools.reduce(
        lambda u, w: u + w, [pcs[j, g][:, p * Q_TILE:(p + 1) * Q_TILE] for p in range(NSA_HPG)]))

    n_wb = WINDOW // K_TILE
    band = WINDOW + K_TILE
    band_kinds = (3,) + (0,) * (n_wb - 2) + (1, 2)
    w0s = [pl.multiple_of(qts[j] * K_TILE, K_TILE) for j in tiles]
    pad_rows = jnp.where(aug_row == 0, NEG_INF, 0.0).astype(BF16)

    def band_logits(g, s):
        blocks = []
        for i, kind in enumerate(band_kinds):
            si = s[i * K_TILE:(i + 1) * K_TILE]
            blocks.append(si if kind == 0 else si + btab_ref[g, kind])
        return jnp.concatenate(blocks, axis=0)

    sw = each(lambda j, g: jnp.dot(kwin_ref[g, pl.ds(w0s[j], band), :],
                                   jnp.concatenate([q_cats[j, g], pad_rows], axis=0),
                                   preferred_element_type=F32))
    sw = each(lambda j, g: band_logits(g, sw[j, g]))
    pws = each(lambda j, g: jnp.exp2(sw[j, g] - jnp.max(sw[j, g], axis=0, keepdims=True)))
    o_wins = each(lambda j, g: jnp.dot(vwin_ref[g, :, pl.ds(w0s[j], band)], pws[j, g].astype(BF16),
                                       preferred_element_type=F32))
    o_wins = each(lambda j, g: o_wins[j, g][0:HEAD_DIM] * (1.0 / o_wins[j, g][HEAD_DIM:HEAD_DIM + 1]))

    imps = each(lambda j, g: _dot_exact_lhs(ovl_ref[...], psums[j, g])[0:n_selp])
    jb = lax.broadcasted_iota(jnp.int32, (n_selp, Q_TILE), 0)

    def masked_score(j, g):
        qb = tqs[j] // SEL_BLOCK
        forced = (jb == 0) | ((jb <= qb) & (jb > qb - SEL_LOCAL))
        score = jnp.where(forced, FORCE_SCORE, imps[j, g])
        score = jnp.where(jb <= qb, score, -1.0)
        return jnp.where(jb < n_sel, score, -2.0)

    scores = each(masked_score)
    sub = 8
    sub_row = lax.broadcasted_iota(jnp.int32, (sub, Q_TILE), 0)
    score_rows = {u: [scores[u][r0:r0 + sub] for r0 in range(0, n_selp, sub)] for u in units}
    rank_rows = {u: [jnp.zeros((sub, Q_TILE), F32) for _ in range(0, n_selp, sub)] for u in units}
    for i in range(n_sel):
        for u in units:
            si = scores[u][i:i + 1, :]
            for v, s_v in enumerate(score_rows[u]):
                r0 = v * sub
                if r0 > i:
                    ahead = si >= s_v
                elif r0 + sub - 1 < i:
                    ahead = si > s_v
                else:
                    ahead = (si > s_v) | ((si == s_v) & (sub_row + r0 > i))
                rank_rows[u][v] = rank_rows[u][v] + ahead.astype(F32)
    ranks = {u: jnp.concatenate(rank_rows[u], axis=0) for u in units}

    def q_with_mask_rows(j, g):
        sel = (ranks[j, g] < float(min(SEL_TOP_N, n_sel))) & (jb < n_sel)
        nm = (sel.astype(F32) - 1.0) * (-NEG_INF)
        return jnp.concatenate([q_cats[j, g], jnp.concatenate([nm] * NSA_HPG, axis=1).astype(BF16),
                                jnp.zeros((HEAD_DIM - n_selp, cols), BF16)], axis=0)

    q_sels = each(q_with_mask_rows)

    pair = 2 * K_TILE

    assert TILES_PER_STEP == 2
    n_pairs = qts[0] // 2 + 1
    n_far = jnp.maximum(n_pairs - 2, 0)
    plan_far = ((0, 0), (0, 0))
    plan_near = ((0, 1), (0, 0))
    plan_diag = ((2,), (1, 2))

    def sel_step(kp, carry, plan):
        k0 = pl.multiple_of(kp * pair, pair)
        old = {u: carry[i] for i, u in enumerate(units)}
        n_keys = lambda j: len(plan[j]) * K_TILE

        def logits(j, g):
            s = jnp.dot(ksel_ref[g, pl.ds(k0, n_keys(j)), :], q_sels[j, g], preferred_element_type=F32)
            if any(plan[j]):
                parts = [s[i * K_TILE:(i + 1) * K_TILE] for i in range(len(plan[j]))]
                parts = [p_ if kind == 0 else p_ + btab_ref[g, kind] for p_, kind in zip(parts, plan[j])]
                s = jnp.concatenate(parts, axis=0)
            return s

        ss = each(logits)
        m_new = each(lambda j, g: jnp.maximum(old[j, g][0], jnp.max(ss[j, g], axis=0, keepdims=True)))
        alpha = each(lambda j, g: jnp.exp2(old[j, g][0] - m_new[j, g]))
        pv = each(lambda j, g: jnp.dot(vsel_ref[g, :, pl.ds(k0, n_keys(j))],
                                       jnp.exp2(ss[j, g] - m_new[j, g]).astype(BF16),
                                       preferred_element_type=F32))
        return tuple((m_new[u], alpha[u] * old[u][1] + pv[u]) for u in units)

    init = (jnp.full((1, cols), NEG_INF, F32), jnp.zeros((V_ROWS, cols), F32))
    sel_far = lax.fori_loop(0, n_far, functools.partial(sel_step, plan=plan_far), tuple(init for _ in units))
    sel_near = lax.fori_loop(n_far, n_pairs - 1, functools.partial(sel_step, plan=plan_near), sel_far)
    sel_out = sel_step(n_pairs - 1, sel_near, plan_diag)

    ssqs = []
    for j in tiles:
        gates = gates_all[:, lanes(j)]
        ssq = jnp.zeros((1, Q_TILE), F32)
        for g in groups:
            _, acc_s = sel_out[units.index((j, g))]
            o_sel = acc_s[0:HEAD_DIM] * (1.0 / acc_s[HEAD_DIM:HEAD_DIM + 1])
            for p in range(NSA_HPG):
                h = g * NSA_HPG + p
                cs = slice(p * Q_TILE, (p + 1) * Q_TILE)
                o = (gates[h:h + 1] * o_cmps[j, g][:, cs]
                     + gates[NSA_HEADS + h:NSA_HEADS + h + 1] * o_sel[:, cs]
                     + gates[2 * NSA_HEADS + h:2 * NSA_HEADS + h + 1] * o_wins[j, g][:, cs])
                ssq = ssq + jnp.sum(o * o, axis=0, keepdims=True)
                ybuf_ref[h * HEAD_DIM:(h + 1) * HEAD_DIM, lanes(j)] = o
        ssqs.append(ssq)
    ssq_all = jnp.concatenate(ssqs, axis=1)
    yT = ybuf_ref[...] * lax.rsqrt(ssq_all / NSA_WIDTH + RMS_EPS) * og_ref[...]
    out_ref[0] = yT.T.astype(out_ref.dtype)


def _nsa_mixer(q, kv, gates, cmp_k_gain, cmp_pe, cmp_w1, cmp_b1, cmp_w2, cmp_b2, out_gain, rel_table):
    bsz, seq, _ = q.shape
    assert TILES_PER_STEP % 2 == 0 and seq % (TILES_PER_STEP * Q_TILE) == 0
    assert seq // CMP_STRIDE == LANES and seq // SEL_BLOCK <= HEAD_DIM
    n_steps = seq // (TILES_PER_STEP * Q_TILE)
    q_rows = TILES_PER_STEP * Q_TILE
    n_cb = seq // CMP_STRIDE
    n_sel = seq // SEL_BLOCK
    n_selp = -(-n_sel // 8) * 8
    cols = NSA_HPG * Q_TILE

    kk = np.arange(K_TILE)[:, None]
    qq = np.arange(Q_TILE)[None, :]
    bk0 = _rel_bucket_np(qq - kk)
    bk1 = _rel_bucket_np(K_TILE + qq - kk)
    cprime = np.arange(2 * LANES)[:, None] - LANES
    bkc = _rel_bucket_np(qq - (CMP_BLOCK - 1) - CMP_STRIDE * cprime)
    c_start = np.arange(n_cb)[None, :] * CMP_STRIDE
    j_start = np.arange(LANES)[:, None] * SEL_BLOCK
    ovl = np.clip(np.minimum(c_start + CMP_BLOCK, j_start + SEL_BLOCK) - np.maximum(c_start, j_start), 0, None)
    ovl = (ovl.astype(np.float32) / CMP_STRIDE)
    ovl[n_sel:, :] = 0.0

    assert NSA_KV_HEADS == 2 and KV_WIDTH == LANES
    w1 = cmp_w1.astype(BF16)
    full = lambda shape: pl.BlockSpec(shape, lambda b, t: (0,) * len(shape))
    kernel = functools.partial(_nsa_kernel, seq=seq)
    return pl.pallas_call(
        kernel,
        grid=(bsz, n_steps),
        in_specs=[
            pl.BlockSpec(memory_space=pltpu.SMEM),
            pl.BlockSpec((1, q_rows, NSA_WIDTH), lambda b, t: (b, t, 0)),
            pl.BlockSpec((1, seq, 6 * KV_WIDTH), lambda b, t: (b, 0, 0)),
            pl.BlockSpec((1, q_rows, LANES), lambda b, t: (b, t, 0)),
            full((1, HEAD_DIM)),
            full((2, 1, CMP_BLOCK * HEAD_DIM)),
            full((2, CMP_BLOCK * HEAD_DIM, CMP_HIDDEN)),
            full((2, 1, CMP_HIDDEN)),
            full((2, CMP_HIDDEN, HEAD_DIM)),
            full((2, 1, HEAD_DIM)),
            full((NSA_WIDTH, q_rows)),
            full((K_TILE, Q_TILE)),
            full((K_TILE, Q_TILE)),
            full((2 * LANES, Q_TILE)),
            full((LANES, n_cb)),
        ],
        out_specs=pl.BlockSpec((1, q_rows, NSA_WIDTH), lambda b, t: (b, t, 0)),
        out_shape=jax.ShapeDtypeStruct((bsz, seq, NSA_WIDTH), BF16),
        scratch_shapes=[
            pltpu.VMEM((2, seq + 2 * CMP_STRIDE, KV_WIDTH), F32),
            pltpu.VMEM((NSA_KV_HEADS, n_cb, HEAD_DIM), BF16),
            pltpu.VMEM((NSA_KV_HEADS, HEAD_DIM, n_cb), BF16),
            pltpu.VMEM((NSA_KV_HEADS, seq, 2 * HEAD_DIM), BF16),
            pltpu.VMEM((NSA_KV_HEADS, V_ROWS, seq), BF16),
            pltpu.VMEM((NSA_KV_HEADS, WINDOW + seq, 2 * HEAD_DIM), BF16),
            pltpu.VMEM((NSA_KV_HEADS, V_ROWS, WINDOW + seq), BF16),
            pltpu.VMEM((NSA_KV_HEADS, 4, K_TILE, cols), F32),
            pltpu.VMEM((NSA_HEADS, 2 * LANES, Q_TILE), F32),
            pltpu.VMEM((NSA_WIDTH, q_rows), F32),
        ],
        compiler_params=pltpu.CompilerParams(
            dimension_semantics=("arbitrary", "arbitrary"), vmem_limit_bytes=VMEM_LIMIT_BYTES),
        name="nsa_mixer",
    )(rel_table, q, kv, gates, cmp_k_gain.reshape(1, HEAD_DIM),
      cmp_pe.reshape(2, 1, CMP_BLOCK * HEAD_DIM), w1,
      cmp_b1.reshape(2, 1, CMP_HIDDEN), cmp_w2.astype(BF16), cmp_b2.reshape(2, 1, HEAD_DIM),
      jnp.broadcast_to(out_gain[:, None], (NSA_WIDTH, q_rows)), jnp.asarray(bk0), jnp.asarray(bk1),
      jnp.asarray(bkc), jnp.asarray(ovl, BF16))


def _rwkv_kernel(zr_ref, zl_ref, zg_ref, kkn_ref, w0_ref, w2_ref, a0_ref, a2_ref, g2_ref,
                 ka_ref, rk_ref, lng_ref, lnb_ref, tri_ref,
                 out_ref,
                 state_ref):
    t = pl.program_id(1)
    tt = WKV_TILE
    n = RWKV_HEAD_DIM
    nb = zr_ref.shape[0]
    nck_seq = tt // WKV_CHUNK
    nck = nb * nck_seq

    @pl.when(t == 0)
    def _reset():
        state_ref[...] = jnp.zeros_like(state_ref)

    stacked = lambda z_ref: jnp.concatenate([z_ref[i] for i in range(nb)], axis=0)
    zr = stacked(zr_ref)
    zl = stacked(zl_ref)
    zg = stacked(zg_ref)
    r = zr[:, 0:RWKV_WIDTH]
    k = zr[:, RWKV_WIDTH:2 * RWKV_WIDTH]
    v = zr[:, 2 * RWKV_WIDTH:3 * RWKV_WIDTH]

    xw = w0_ref[...] + _dot(jnp.tanh(zl[:, 0:DECAY_LORA]), w2_ref[...])
    lw = -math.exp(-0.5) * jax.nn.sigmoid(xw)
    a = jax.nn.sigmoid(a0_ref[...] + _dot(zl[:, DECAY_LORA:DECAY_LORA + ICLR_LORA], a2_ref[...]))
    gate = _dot(jax.nn.sigmoid(zg), g2_ref[...])

    cum = jnp.concatenate([_dot_exact_lhs(tri_ref[...], lw[i * tt:(i + 1) * tt]) for i in range(nb)],
                          axis=0)
    cum_end = cum.reshape(nck, WKV_CHUNK, RWKV_WIDTH)[:, WKV_CHUNK - 1:WKV_CHUNK, :]
    cum_end_b = jnp.broadcast_to(cum_end, (nck, WKV_CHUNK, RWKV_WIDTH)).reshape(nb * tt, RWKV_WIDTH)
    e_incl = jnp.exp(cum)
    e_excl = jnp.exp(cum - lw)
    e_neg = jnp.exp(-cum)
    e_rev = jnp.exp(cum_end_b - cum)
    g_end = jnp.exp(cum_end)

    cw = WKV_CHUNK
    lane = lax.broadcasted_iota(jnp.int32, (1, 2 * n), 1)
    lo = lane < n
    ri = lax.broadcasted_iota(jnp.int32, (cw, 4 * n), 0)
    ci = lax.broadcasted_iota(jnp.int32, (cw, 4 * n), 1) % n
    strict = ri > ci
    incl = ri >= ci
    eye_row = (ri == ci).astype(F32)
    i2 = lax.broadcasted_iota(jnp.int32, (2 * n, 2 * n), 0)
    j2 = lax.broadcasted_iota(jnp.int32, (2 * n, 2 * n), 1)
    same_head = (i2 // n) == (j2 // n)
    eye_2n = (i2 == j2).astype(F32)
    zeros_slab = jnp.zeros((cw, 2 * n), BF16)

    def seg_sum(u, f=lambda s: s):
        s_lo = jnp.sum(jnp.where(lo, u, 0.0), axis=-1, keepdims=True)
        s_hi = jnp.sum(jnp.where(lo, 0.0, u), axis=-1, keepdims=True)
        return jnp.where(lo, f(s_lo), f(s_hi))

    def block_diag(u):
        ua, ub = u[:, 0:2 * n], u[:, 2 * n:4 * n]
        parts = []
        for w, first in ((ua, True), (ub, False)):
            for keep_lo in (True, False):
                blk = jnp.where(lo, w, 0.0) if keep_lo else jnp.where(lo, 0.0, w)
                blk = blk.astype(BF16)
                parts.append(jnp.concatenate([blk, zeros_slab] if first else [zeros_slab, blk], axis=1))
        return jnp.concatenate(parts, axis=0)

    rows = lambda c: slice(c * cw, (c + 1) * cw)

    n_pairs = RWKV_HEADS // 2
    slab = lambda p: slice(2 * n * p, 2 * n * (p + 1))
    pre = []
    for p in range(n_pairs):
        sl = slab(p)
        rp, kp, vp, ap = r[:, sl], k[:, sl], v[:, sl], a[:, sl]
        kkn = jnp.concatenate([kkn_ref[i, :, sl] for i in range(nb)], axis=0)
        kmod = kp * (1.0 + (ap - 1.0) * ka_ref[:, sl])
        bv = kkn * ap
        a_t = -kkn * e_excl[:, sl]
        r_t = rp * e_incl[:, sl]
        b_hat = bv * e_neg[:, sl]
        k_hat = kmod * e_neg[:, sl]
        mms = []
        for c in range(nck):
            bh, kh = b_hat[rows(c)], k_hat[rows(c)]
            rhs = jnp.concatenate([jnp.where(lo, bh, 0.0), jnp.where(lo, 0.0, bh),
                                   jnp.where(lo, kh, 0.0), jnp.where(lo, 0.0, kh)], axis=0)
            mms.append(_dot_nt(jnp.concatenate([a_t[rows(c)], r_t[rows(c)]], axis=0), rhs))
        pre.append(dict(r=rp, v=vp, kmod=kmod, a_t=a_t, r_t=r_t, b_end=bv * e_rev[:, sl],
                        k_end=kmod * e_rev[:, sl], mms=mms))

    chains = [(p, c0) for p in range(n_pairs) for c0 in range(0, nck, 2)]

    def pair_row(c0, f):
        return jnp.concatenate([f(c0), f(c0 + 1)], axis=1)

    quads = []
    for p, c0 in chains:
        mms = pre[p]["mms"]
        quads.append((
            jnp.where(strict, pair_row(c0, lambda c: mms[c][0:cw, 0:2 * n]), 0.0),
            jnp.where(strict, pair_row(c0, lambda c: mms[c][0:cw, 2 * n:4 * n]), 0.0),
            jnp.where(incl, pair_row(c0, lambda c: mms[c][cw:2 * cw, 0:2 * n]), 0.0),
            jnp.where(incl, pair_row(c0, lambda c: mms[c][cw:2 * cw, 2 * n:4 * n]), 0.0)))
    tinvs = [eye_row + q[0] for q in quads]
    xs = [_dot(q[0], block_diag(q[0])) for q in quads]
    for _ in range(int(math.log2(cw)) - 2):
        txs = [_dot(jnp.concatenate([t_, x_], axis=0), block_diag(x_)) for t_, x_ in zip(tinvs, xs)]
        tinvs = [t_ + tx[0:cw] for t_, tx in zip(tinvs, txs)]
        xs = [tx[cw:2 * cw] for tx in txs]
    tinvs = [t_ + _dot(t_, block_diag(x_)) for t_, x_ in zip(tinvs, xs)]
    lvs = [_dot(jnp.concatenate([q[1], q[3]], axis=0),
                block_diag(pair_row(c0, lambda c: pre[p]["v"][rows(c)])))
           for (p, c0), q in zip(chains, quads)]
    w_rows = [_dot(t_, block_diag(pair_row(c0, lambda c: pre[p]["a_t"][rows(c)])))
              for (p, c0), t_ in zip(chains, tinvs)]
    u_rows = [_dot(t_, block_diag(lv[0:cw])) for t_, lv in zip(tinvs, lvs)]
    qp_rows = [pair_row(c0, lambda c: pre[p]["r_t"][rows(c)]) + _dot(q[2], block_diag(w_))
               for (p, c0), q, w_ in zip(chains, quads, w_rows)]
    y0_rows = [_dot(q[2], block_diag(u_)) + lv[cw:2 * cw] for q, u_, lv in zip(quads, u_rows, lvs)]
    phs = {}
    for ci, (p, c0) in enumerate(chains):
        for j, c in enumerate((c0, c0 + 1)):
            ls = slice(2 * n * j, 2 * n * (j + 1))
            lhs = jnp.concatenate([pre[p]["b_end"][rows(c)], pre[p]["k_end"][rows(c)]], axis=0)
            rhs = jnp.concatenate(
                [jnp.concatenate([w_rows[ci][:, ls], u_rows[ci][:, ls]], axis=1),
                 jnp.concatenate([jnp.zeros((cw, 2 * n), F32), pre[p]["v"][rows(c)]], axis=1)], axis=0)
            ph = _dot_tn(lhs, rhs)
            pmat = eye_2n * g_end[c, :, slab(p)] + jnp.where(same_head, ph[:, 0:2 * n], 0.0)
            phs[(p, c)] = (jnp.concatenate([qp_rows[ci][:, ls], pmat], axis=0),
                           jnp.where(same_head, ph[:, 2 * n:4 * n], 0.0), y0_rows[ci][:, ls])
    seqs = [(i, p) for i in range(nb) for p in range(n_pairs)]
    hstates = {ip: state_ref[ip[0], ip[1]] for ip in seqs}
    ys = {ip: [] for ip in seqs}
    for c in range(nck_seq):
        for i, p in seqs:
            qpm, hp, y0 = phs[(p, i * nck_seq + c)]
            qph = _dot(qpm, hstates[i, p])
            ys[i, p].append(qph[0:cw] + y0)
            hstates[i, p] = qph[cw:cw + 2 * n] + hp
    for i, p in seqs:
        state_ref[i, p] = hstates[i, p]
    for p in range(n_pairs):
        sl = slab(p)
        y = jnp.concatenate([u for i in range(nb) for u in ys[i, p]], axis=0)
        yc = y - seg_sum(y, lambda s: s * (1.0 / n))
        inv_std = seg_sum(yc * yc, lambda s: lax.rsqrt(s * (1.0 / n) + GN_EPS))
        yn = yc * inv_std * lng_ref[:, sl] + lnb_ref[:, sl]
        bonus = seg_sum(pre[p]["r"] * pre[p]["kmod"] * rk_ref[:, sl]) * pre[p]["v"]
        res = ((yn + bonus) * gate[:, sl]).astype(out_ref.dtype)
        for i in range(nb):
            out_ref[i, :, sl] = res[i * tt:(i + 1) * tt]


def _rwkv_mixer(zr, zl, zg, kkn, w0, w2, a0, a2, g2p, k_a, r_k, ln_g, ln_b):
    bsz, seq, _ = zr.shape
    tt = WKV_TILE
    assert seq % tt == 0 and tt % (2 * WKV_CHUNK) == 0 and 2 * RWKV_HEAD_DIM == LANES
    nb = WKV_BATCH if bsz % WKV_BATCH == 0 else 1
    idx = np.arange(tt)
    tri = ((idx[:, None] // WKV_CHUNK == idx[None, :] // WKV_CHUNK) & (idx[:, None] >= idx[None, :]))
    full = lambda shape: pl.BlockSpec(shape, lambda b, t: (0,) * len(shape))
    row = lambda a: a.reshape(1, -1)
    wl = DECAY_LORA + ICLR_LORA
    return pl.pallas_call(
        _rwkv_kernel,
        grid=(bsz // nb, seq // tt),
        in_specs=[
            pl.BlockSpec((nb, tt, 3 * RWKV_WIDTH), lambda b, t: (b, t, 0)),
            pl.BlockSpec((nb, tt, wl), lambda b, t: (b, t, 0)),
            pl.BlockSpec((nb, tt, GATE_LORA_PAD), lambda b, t: (b, t, 0)),
            pl.BlockSpec((nb, tt, RWKV_WIDTH), lambda b, t: (b, t, 0)),
            full((1, RWKV_WIDTH)), full((DECAY_LORA, RWKV_WIDTH)),
            full((1, RWKV_WIDTH)), full((ICLR_LORA, RWKV_WIDTH)),
            full((GATE_LORA_PAD, RWKV_WIDTH)),
            full((1, RWKV_WIDTH)), full((1, RWKV_WIDTH)),
            full((1, RWKV_WIDTH)), full((1, RWKV_WIDTH)),
            full((tt, tt)),
        ],
        out_specs=pl.BlockSpec((nb, tt, RWKV_WIDTH), lambda b, t: (b, t, 0)),
        out_shape=jax.ShapeDtypeStruct((bsz, seq, RWKV_WIDTH), BF16),
        scratch_shapes=[
            pltpu.VMEM((nb, RWKV_HEADS // 2, 2 * RWKV_HEAD_DIM, 2 * RWKV_HEAD_DIM), F32),
        ],
        compiler_params=pltpu.CompilerParams(
            dimension_semantics=("arbitrary", "arbitrary"), vmem_limit_bytes=VMEM_LIMIT_BYTES),
        name="rwkv7_mixer",
    )(zr, zl, zg, kkn, row(w0), w2.astype(BF16), row(a0), a2.astype(BF16),
      g2p.astype(BF16), row(k_a), row(r_k), row(ln_g), row(ln_b), jnp.asarray(tri, BF16))


def _ffn_kernel(x_ref, yn_ref, yr_ref, won_ref, wor_ref, gain_ref, wg_ref, wu_ref, wd_ref, o_ref, *, ff_chunk):
    h = x_ref[...]
    h = h + jnp.dot(yn_ref[...], won_ref[...], preferred_element_type=F32)
    h = h + jnp.dot(yr_ref[...], wor_ref[...], preferred_element_type=F32)
    ms = jnp.mean(h * h, axis=-1, keepdims=True)
    hn = (h * lax.rsqrt(ms + RMS_EPS) * gain_ref[...]).astype(BF16)
    d_ff = wg_ref.shape[1]
    acc = None
    for c in range(d_ff // ff_chunk):
        cs = slice(c * ff_chunk, (c + 1) * ff_chunk)
        gte = jnp.dot(hn, wg_ref[:, cs], preferred_element_type=F32)
        up = jnp.dot(hn, wu_ref[:, cs], preferred_element_type=F32)
        act = (gte * jax.nn.sigmoid(gte) * up).astype(BF16)
        down = jnp.dot(act, wd_ref[cs, :], preferred_element_type=F32)
        acc = down if acc is None else acc + down
    o_ref[...] = h + acc


def _out_ffn(x2, y_nsa, y_rwkv, w_out, ffn_gain, w_gate, w_up, w_down):
    n, d = x2.shape
    d_ff = w_gate.shape[1]
    tm = min(FFN_ROW_TILE, n)
    ff_chunk = FF_CHUNK if d_ff % FF_CHUNK == 0 else d_ff
    const = lambda shape: pl.BlockSpec(shape, lambda i: (0, 0), pipeline_mode=pl.Buffered(1))
    return pl.pallas_call(
        functools.partial(_ffn_kernel, ff_chunk=ff_chunk),
        grid=(n // tm,),
        in_specs=[
            pl.BlockSpec((tm, d), lambda i: (i, 0)),
            pl.BlockSpec((tm, NSA_WIDTH), lambda i: (i, 0)),
            pl.BlockSpec((tm, RWKV_WIDTH), lambda i: (i, 0)),
            const((NSA_WIDTH, d)), const((RWKV_WIDTH, d)), const((1, d)),
            const((d, d_ff)), const((d, d_ff)), const((d_ff, d)),
        ],
        out_specs=pl.BlockSpec((tm, d), lambda i: (i, 0)),
        out_shape=jax.ShapeDtypeStruct((n, d), F32),
        compiler_params=pltpu.CompilerParams(
            dimension_semantics=("arbitrary",), vmem_limit_bytes=VMEM_LIMIT_BYTES),
        name="out_proj_ffn",
    )(x2, y_nsa, y_rwkv, w_out[:NSA_WIDTH].astype(BF16), w_out[NSA_WIDTH:].astype(BF16),
      ffn_gain.reshape(1, d), w_gate.astype(BF16), w_up.astype(BF16), w_down.astype(BF16))


def _layer(h, attn_gain, w_in, q_gain, k_gain, cmp_pe, cmp_w1, cmp_b1, cmp_w2, cmp_b2, out_gain, shift_mix,
           w0, w2, a0, a2, g2, k_k, k_a, r_k, ln_g, ln_b, w_out, ffn_gain, w_gate, w_up, w_down, rel_table):
    bsz, seq, d = h.shape
    nsa_cols = NSA_WIDTH + 6 * KV_WIDTH + 3 * NSA_HEADS
    gate0 = NSA_WIDTH + 6 * KV_WIDTH

    w_bf = w_in.astype(BF16)
    w_nsa, w_rw = w_bf[:, :nsa_cols], w_bf[:, nsa_cols:]
    gate_src = np.array([gate0 + h_ * 3 + br for br in range(3) for h_ in range(NSA_HEADS)])
    w_gates = jnp.zeros((d, LANES), BF16).at[:, :3 * NSA_HEADS].set(w_nsa[:, gate_src])
    lora0 = 3 * RWKV_WIDTH
    g0 = lora0 + DECAY_LORA + ICLR_LORA
    w_g = jnp.zeros((d, GATE_LORA_PAD), BF16).at[:, :GATE_LORA].set(w_rw[:, g0:g0 + GATE_LORA])
    widths = (NSA_WIDTH, 6 * KV_WIDTH, LANES, 3 * RWKV_WIDTH, DECAY_LORA + ICLR_LORA, GATE_LORA_PAD)
    w_cat = jnp.concatenate([w_nsa[:, :gate0], w_gates, w_rw[:, :g0], w_g], axis=1)
    mix_g = jnp.zeros((GATE_LORA_PAD,), F32).at[:GATE_LORA].set(shift_mix[g0:g0 + GATE_LORA])
    g2p = jnp.zeros((GATE_LORA_PAD, RWKV_WIDTH), F32).at[:GATE_LORA].set(g2)

    x2 = h.reshape(bsz * seq, d)
    q, kv, gates, zr, zl, zg, kkn = _in_projection(x2, attn_gain.reshape(1, d), w_cat, widths, seq, q_gain, k_gain,
                                                   shift_mix[:lora0], shift_mix[lora0:g0], mix_g, k_k)
    r3 = lambda u: u.reshape(bsz, seq, u.shape[-1])
    y_nsa = _nsa_mixer(r3(q), r3(kv), r3(gates), k_gain[0], cmp_pe, cmp_w1, cmp_b1, cmp_w2, cmp_b2,
                       out_gain, rel_table)
    y_rwkv = _rwkv_mixer(r3(zr), r3(zl), r3(zg), r3(kkn), w0, w2, a0, a2, g2p, k_a, r_k.reshape(-1), ln_g, ln_b)
    out = _out_ffn(x2, y_nsa.reshape(bsz * seq, NSA_WIDTH), y_rwkv.reshape(bsz * seq, RWKV_WIDTH),
                   w_out, ffn_gain, w_gate, w_up, w_down)
    return out.reshape(bsz, seq, d)


def kernel(x, attn_norm_gain, w_in, nsa_q_gain, nsa_k_gain, cmp_pe, cmp_w1, cmp_b1, cmp_w2, cmp_b2, nsa_out_gain, rwkv_shift_mix, rwkv_w0, rwkv_w2, rwkv_a0, rwkv_a2, rwkv_g2, rwkv_k_k, rwkv_k_a, rwkv_r_k, rwkv_ln_gain, rwkv_ln_bias, w_out, ffn_norm_gain, w_gate, w_up, w_down, rel_bias_table):
    h = x
    for l in range(attn_norm_gain.shape[0]):
        h = _layer(h, attn_norm_gain[l], w_in[l], nsa_q_gain[l], nsa_k_gain[l], cmp_pe[l], cmp_w1[l], cmp_b1[l],
                   cmp_w2[l], cmp_b2[l], nsa_out_gain[l], rwkv_shift_mix[l], rwkv_w0[l], rwkv_w2[l], rwkv_a0[l],
                   rwkv_a2[l], rwkv_g2[l], rwkv_k_k[l], rwkv_k_a[l], rwkv_r_k[l], rwkv_ln_gain[l],
                   rwkv_ln_bias[l], w_out[l], ffn_norm_gain[l], w_gate[l], w_up[l], w_down[l], rel_bias_table)
    return h
```
